```python
import jax
import jax.numpy as jnp
from jax import lax
import numpy as np

D_MODEL = 1024
BATCH = 1
SEQ = 16384
DEPTH = 2

GRID_W = 64
CTX_LEN = 256
EPS = 1e-6

F_GROUPS = 4
F_DIM = 64
F_WIDTH = F_GROUPS * F_DIM

MLA_HEADS = 6
MLA_Q_LORA = 192
MLA_KV_LORA = 128
MLA_NOPE = 64
MLA_ROPE = 32
MLA_V = 64
MLA_QK = MLA_NOPE + MLA_ROPE
MLA_WIDTH = MLA_HEADS * MLA_V
MLA_IN = MLA_Q_LORA + MLA_KV_LORA + MLA_ROPE
ROPE_BASE = 10000.0
Q_BLOCK = 128

GLA_HEADS = 4
GLA_DK = 48
GLA_DV = 96
GLA_GATE_RANK = 16
GLA_GATE_NORM = 16.0
GLA_CHUNK = 64
GLA_WIDTH = GLA_HEADS * GLA_DV
GLA_QK_W = GLA_HEADS * GLA_DK
GLA_IN = 2 * GLA_QK_W + 2 * GLA_WIDTH + GLA_GATE_RANK

MIX_WIDTH = F_WIDTH + MLA_WIDTH + GLA_WIDTH
IN_COLS = F_WIDTH + MLA_IN + GLA_IN
D_FF = 2816
CONV_W = 3
N_MOD = 6

kernel_name = "hybrid_fourier_mla_gla_dit_block"


def rms_norm(x, w):
    xf = x.astype(jnp.float32)
    y = xf * lax.rsqrt(jnp.mean(xf * xf, axis=-1, keepdims=True) + EPS)
    return y.astype(x.dtype) * w


def modulate(h, shift, scale):
    return h * (1 + scale) + shift


def adaln(cond, w_mod, b_mod):
    m = jax.nn.silu(cond) @ w_mod + b_mod
    return jnp.split(m, N_MOD, axis=-1)


def to_heads(t, n_heads, d):
    B, L, _ = t.shape
    return t.reshape(B, L, n_heads, d).transpose(0, 2, 1, 3)


def from_heads(t):
    B, H, L, d = t.shape
    return t.transpose(0, 2, 1, 3).reshape(B, L, H * d)


def axial_angles(n_tokens):
    rows = n_tokens // GRID_W
    row = jnp.repeat(jnp.arange(rows, dtype=jnp.float32), GRID_W)
    col = jnp.tile(jnp.arange(GRID_W, dtype=jnp.float32), rows)
    axis_dims = MLA_ROPE // 2
    inv_freq = jnp.power(ROPE_BASE, -jnp.arange(0, axis_dims, 2, dtype=jnp.float32) / axis_dims)
    return row[:, None] * inv_freq, col[:, None] * inv_freq


def rotate_half_pairs(t, ang):
    t1, t2 = jnp.split(t, 2, axis=-1)
    cos = jnp.cos(ang).astype(t.dtype)
    sin = jnp.sin(ang).astype(t.dtype)
    return jnp.concatenate([t1 * cos - t2 * sin, t1 * sin + t2 * cos], axis=-1)


def rope_part(t, ang_r, ang_c):
    t_nope, t_rope = t[..., :MLA_NOPE], t[..., MLA_NOPE:]
    t_r, t_c = jnp.split(t_rope, 2, axis=-1)
    return jnp.concatenate([t_nope, rotate_half_pairs(t_r, ang_r), rotate_half_pairs(t_c, ang_c)], axis=-1)


def fourier_mix(f):
    B, L, _ = f.shape
    fg = f.astype(jnp.float32).reshape(B, L, F_GROUPS, F_DIM)
    y = jnp.fft.fft2(fg, axes=(1, 3), norm="ortho").real
    return y.reshape(B, L, F_WIDTH).astype(f.dtype)


def mla_q(p, q_lora_norm_w, w_uq, q_norm_w, ang_r, ang_c):
    B, L, _ = p.shape
    c_q = p[..., :MLA_Q_LORA]
    q = (rms_norm(c_q, q_lora_norm_w) @ w_uq).reshape(B, L, MLA_HEADS, MLA_QK)
    q = rms_norm(q, q_norm_w).transpose(0, 2, 1, 3)
    if ang_r is not None:
        q = rope_part(q, ang_r, ang_c)
    return q


def mla_kv(p, kv_lora_norm_w, w_ukv, k_norm_w, ang_r, ang_c):
    B, L, _ = p.shape
    c_kv = p[..., MLA_Q_LORA:MLA_Q_LORA + MLA_KV_LORA]
    k_pe = p[..., MLA_Q_LORA + MLA_KV_LORA:]
    kv = (rms_norm(c_kv, kv_lora_norm_w) @ w_ukv).reshape(B, L, MLA_HEADS, MLA_NOPE + MLA_V)
    k_nope, v = kv[..., :MLA_NOPE], kv[..., MLA_NOPE:]
    k = jnp.concatenate([k_nope, jnp.broadcast_to(k_pe[:, :, None, :], (B, L, MLA_HEADS, MLA_ROPE))], axis=-1)
    k = rms_norm(k, k_norm_w).transpose(0, 2, 1, 3)
    if ang_r is not None:
        k = rope_part(k, ang_r, ang_c)
    return k, v.transpose(0, 2, 1, 3)


def blocked_attention(q, k, v):
    B, H, L, dk = q.shape
    nb = L // Q_BLOCK
    scale = MLA_QK ** -0.5
    qb = q.reshape(B, H, nb, Q_BLOCK, dk).transpose(2, 0, 1, 3, 4)

    def one_block(q_blk):
        s = jnp.einsum("bhqd,bhkd->bhqk", q_blk, k).astype(jnp.float32) * scale
        pr = jax.nn.softmax(s, axis=-1).astype(v.dtype)
        return jnp.einsum("bhqk,bhkv->bhqv", pr, v)

    ob = lax.map(one_block, qb)
    return ob.transpose(1, 2, 0, 3, 4).reshape(B, H, L, v.shape[-1])


def gla_chunked(q, k, v, g, s0):
    B, H, T, dk = q.shape
    dv = v.shape[-1]
    n = T // GLA_CHUNK
    f32 = jnp.float32
    qc = q.astype(f32).reshape(B, H, n, GLA_CHUNK, dk)
    kc = k.astype(f32).reshape(B, H, n, GLA_CHUNK, dk)
    vc = v.astype(f32).reshape(B, H, n, GLA_CHUNK, dv)
    b = jnp.cumsum(g.astype(f32).reshape(B, H, n, GLA_CHUNK, dk), axis=3)
    b_last = b[:, :, :, -1:, :]
    q_in = qc * jnp.exp(b)
    k_in = kc * jnp.exp(-b)
    k_out = kc * jnp.exp(b_last - b)
    mask = jnp.tril(jnp.ones((GLA_CHUNK, GLA_CHUNK), dtype=bool))
    a = jnp.where(mask, jnp.einsum("bhncd,bhnsd->bhncs", q_in, k_in), 0.0)
    o_intra = jnp.einsum("bhncs,bhnsv->bhncv", a, vc)
    u = jnp.einsum("bhnsd,bhnsv->bhndv", k_out, vc)
    decay = jnp.exp(b_last[:, :, :, 0, :])

    def step(s, inp):
        dec, inc = inp
        return dec[..., None] * s + inc, s

    s_fin, s_prev = lax.scan(step, s0.astype(f32), (jnp.moveaxis(decay, 2, 0), jnp.moveaxis(u, 2, 0)))
    s_prev = jnp.moveaxis(s_prev, 0, 2)
    o_inter = jnp.einsum("bhncd,bhndv->bhncv", q_in, s_prev)
    return (o_intra + o_inter).reshape(B, H, T, dv), s_fin


def gla_parts(p):
    q, k, v, gk_lr, g = jnp.split(p, [GLA_QK_W, 2 * GLA_QK_W, 2 * GLA_QK_W + GLA_WIDTH,
                                      2 * GLA_QK_W + GLA_WIDTH + GLA_GATE_RANK], axis=-1)
    return (to_heads(q, GLA_HEADS, GLA_DK) * (GLA_DK ** -0.5), to_heads(k, GLA_HEADS, GLA_DK),
            to_heads(v, GLA_HEADS, GLA_DV), gk_lr, g)


def gla_log_decay(gk_lr, w_gk, b_gk):
    g = jax.nn.log_sigmoid((gk_lr @ w_gk + b_gk).astype(jnp.float32)) / GLA_GATE_NORM
    return to_heads(g, GLA_HEADS, GLA_DK)


def gla_direction(parts_x, parts_c, w_gk, b_gk, reverse):
    flip = (lambda t: jnp.flip(t, axis=2)) if reverse else (lambda t: t)
    qx, kx, vx, lx, _ = parts_x
    qc, kc, vc, lc, _ = parts_c
    B = qx.shape[0]
    s0 = jnp.zeros((B, GLA_HEADS, GLA_DK, GLA_DV), jnp.float32)
    o_c, s_c = gla_chunked(flip(qc), flip(kc), flip(vc), flip(gla_log_decay(lc, w_gk, b_gk)), s0)
    o_x, _ = gla_chunked(flip(qx), flip(kx), flip(vx), flip(gla_log_decay(lx, w_gk, b_gk)), s_c)
    return flip(o_x), flip(o_c)


def gla_output(o, g, norm_w):
    B, H, T, dv = o.shape
    o = rms_norm(o.transpose(0, 2, 1, 3).astype(g.dtype), norm_w).reshape(B, T, H * dv)
    return o * jax.nn.silu(g)


def token_mixer(h_x, h_c, ang_r, ang_c, w_in, mla_q_lora_norm_w, mla_w_uq, mla_kv_lora_norm_w,
                mla_w_ukv, mla_q_norm_w, mla_k_norm_w, gla_w_gk_fwd, gla_b_gk_fwd, gla_w_gk_bwd,
                gla_b_gk_bwd, gla_norm_w, w_out, need_ctx_out):
    p_x = h_x @ w_in
    p_c = h_c @ w_in
    f_x, m_x, g_x = jnp.split(p_x, [F_WIDTH, F_WIDTH + MLA_IN], axis=-1)
    f_c, m_c, g_c = jnp.split(p_c, [F_WIDTH, F_WIDTH + MLA_IN], axis=-1)
    k_c, v_c = mla_kv(m_c, mla_kv_lora_norm_w, mla_w_ukv, mla_k_norm_w, None, None)
    k_x, v_x = mla_kv(m_x, mla_kv_lora_norm_w, mla_w_ukv, mla_k_norm_w, ang_r, ang_c)
    q_x = mla_q(m_x, mla_q_lora_norm_w, mla_w_uq, mla_q_norm_w, ang_r, ang_c)
    a_x = blocked_attention(q_x, jnp.concatenate([k_c, k_x], axis=2), jnp.concatenate([v_c, v_x], axis=2))
    parts_x, parts_c = gla_parts(g_x), gla_parts(g_c)
    ox_f, oc_f = gla_direction(parts_x, parts_c, gla_w_gk_fwd, gla_b_gk_fwd, False)
    ox_b, oc_b = gla_direction(parts_x, parts_c, gla_w_gk_bwd, gla_b_gk_bwd, True)
    lin_x = gla_output(ox_f + ox_b, parts_x[4], gla_norm_w)
    y_x = jnp.concatenate([fourier_mix(f_x), from_heads(a_x), lin_x], axis=-1) @ w_out
    if not need_ctx_out:
        return y_x, None
    q_c = mla_q(m_c, mla_q_lora_norm_w, mla_w_uq, mla_q_norm_w, None, None)
    a_c = blocked_attention(q_c, k_c, v_c)
    lin_c = gla_output(oc_f + oc_b, parts_c[4], gla_norm_w)
    y_c = jnp.concatenate([fourier_mix(f_c), from_heads(a_c), lin_c], axis=-1) @ w_out
    return y_x, y_c


def conv_ffn(h, w_12, conv_w, conv_b, w_down):
    u = h @ w_12
    up = jnp.pad(u, ((0, 0), (1, 1), (0, 0)))
    u = up[:, :-2] * conv_w[0] + up[:, 1:-1] * conv_w[1] + up[:, 2:] * conv_w[2] + conv_b
    a, b = jnp.split(u, 2, axis=-1)
    return (jax.nn.silu(a) * b) @ w_down


def setup_inputs(seed: int = 0) -> dict:
    key = jax.random.key(seed)
    keys = iter(jax.random.split(key, 32))

    def nrm(shape, scale):
        return jax.random.normal(next(keys), shape, jnp.float32) * scale

    def gain(shape):
        return 1.0 + 0.02 * jax.random.normal(next(keys), shape, jnp.float32)

    D = D_MODEL
    return {
        "x": nrm((BATCH, SEQ, D), 1.0),
        "c": nrm((BATCH, D), 1.0),
        "ctx": nrm((BATCH, CTX_LEN, D), 1.0),
        "c_ctx": nrm((D,), 1.0),
        "mod_w": nrm((DEPTH, D, N_MOD * D), D ** -0.5),
        "mod_b": nrm((DEPTH, N_MOD * D), 0.02),
        "norm1_w": gain((DEPTH, D)),
        "norm2_w": gain((DEPTH, D)),
        "w_in": nrm((DEPTH, D, IN_COLS), D ** -0.5),
        "mla_q_lora_norm_w": gain((DEPTH, MLA_Q_LORA)),
        "mla_w_uq": nrm((DEPTH, MLA_Q_LORA, MLA_HEADS * MLA_QK), MLA_Q_LORA ** -0.5),
        "mla_kv_lora_norm_w": gain((DEPTH, MLA_KV_LORA)),
        "mla_w_ukv": nrm((DEPTH, MLA_KV_LORA, MLA_HEADS * (MLA_NOPE + MLA_V)), MLA_KV_LORA ** -0.5),
        "mla_q_norm_w": gain((DEPTH, MLA_QK)),
        "mla_k_norm_w": gain((DEPTH, MLA_QK)),
        "gla_w_gk_fwd": nrm((DEPTH, GLA_GATE_RANK, GLA_QK_W), GLA_GATE_RANK ** -0.5),
        "gla_b_gk_fwd": nrm((DEPTH, GLA_QK_W), 0.02),
        "gla_w_gk_bwd": nrm((DEPTH, GLA_GATE_RANK, GLA_QK_W), GLA_GATE_RANK ** -0.5),
        "gla_b_gk_bwd": nrm((DEPTH, GLA_QK_W), 0.02),
        "gla_norm_w": gain((DEPTH, GLA_DV)),
        "w_out": nrm((DEPTH, MIX_WIDTH, D), MIX_WIDTH ** -0.5),
        "ffn_w_12": nrm((DEPTH, D, 2 * D_FF), D ** -0.5),
        "ffn_conv_w": nrm((DEPTH, CONV_W, 2 * D_FF), CONV_W ** -0.5),
        "ffn_conv_b": nrm((DEPTH, 2 * D_FF), 0.02),
        "ffn_w_down": nrm((DEPTH, D_FF, D), D_FF ** -0.5),
    }


def reference(x, c, ctx, c_ctx, mod_w, mod_b, norm1_w, norm2_w, w_in, mla_q_lora_norm_w, mla_w_uq,
              mla_kv_lora_norm_w, mla_w_ukv, mla_q_norm_w, mla_k_norm_w, gla_w_gk_fwd, gla_b_gk_fwd,
              gla_w_gk_bwd, gla_b_gk_bwd, gla_norm_w, w_out, ffn_w_12, ffn_conv_w, ffn_conv_b, ffn_w_down):
    ang_r, ang_c = axial_angles(x.shape[1])
    xc = ctx
    for l in range(DEPTH):
        last = l == DEPTH - 1
        sh1, sc1, g1, sh2, sc2, g2 = [m[:, None, :] for m in adaln(c, mod_w[l], mod_b[l])]
        csh1, csc1, cg1, csh2, csc2, cg2 = adaln(c_ctx, mod_w[l], mod_b[l])
        h_x = modulate(rms_norm(x, norm1_w[l]), sh1, sc1)
        h_c = modulate(rms_norm(xc, norm1_w[l]), csh1, csc1)
        y_x, y_c = token_mixer(h_x, h_c, ang_r, ang_c, w_in[l], mla_q_lora_norm_w[l], mla_w_uq[l],
                               mla_kv_lora_norm_w[l], mla_w_ukv[l], mla_q_norm_w[l], mla_k_norm_w[l],
                               gla_w_gk_fwd[l], gla_b_gk_fwd[l], gla_w_gk_bwd[l], gla_b_gk_bwd[l],
                               gla_norm_w[l], w_out[l], not last)
        x = x + g1 * y_x
        h_x = modulate(rms_norm(x, norm2_w[l]), sh2, sc2)
        x = x + g2 * conv_ffn(h_x, ffn_w_12[l], ffn_conv_w[l], ffn_conv_b[l], ffn_w_down[l])
        if not last:
            xc = xc + cg1 * y_c
            h_c = modulate(rms_norm(xc, norm2_w[l]), csh2, csc2)
            xc = xc + cg2 * conv_ffn(h_c, ffn_w_12[l], ffn_conv_w[l], ffn_conv_b[l], ffn_w_down[l])
    return x
```

```python
import functools
import math

import numpy as np
import jax
import jax.numpy as jnp
from jax import lax
from jax.experimental import pallas as pl
from jax.experimental.pallas import tpu as pltpu

D_MODEL = 1024
DEPTH = 2
GRID_W = 64
EPS = 1e-6

F_GROUPS = 4
F_DIM = 64
F_WIDTH = F_GROUPS * F_DIM

MLA_HEADS = 6
MLA_Q_LORA = 192
MLA_KV_LORA = 128
MLA_NOPE = 64
MLA_ROPE = 32
MLA_V = 64
MLA_QK = MLA_NOPE + MLA_ROPE
MLA_WIDTH = MLA_HEADS * MLA_V
MLA_IN = MLA_Q_LORA + MLA_KV_LORA + MLA_ROPE
ROPE_BASE = 10000.0

GLA_HEADS = 4
GLA_DK = 48
GLA_DV = 96
GLA_GATE_RANK = 16
GLA_GATE_NORM = 16.0
GLA_CHUNK = 64
GLA_WIDTH = GLA_HEADS * GLA_DV
GLA_QK_W = GLA_HEADS * GLA_DK

MIX_WIDTH = F_WIDTH + MLA_WIDTH + GLA_WIDTH
D_FF = 2816
N_MOD = 6

LANE = 128
SUBLANE_BF16 = 16
VMEM_LIMIT = 48 * 1024 * 1024

HEAD_SLOT = LANE
MLA_SLOTS = MLA_HEADS * HEAD_SLOT
GLA_SLOTS = GLA_HEADS * HEAD_SLOT

COL_F = 0
COL_CQ = COL_F + F_WIDTH
COL_CKV = COL_CQ + 256
COL_KPE = COL_CKV + MLA_KV_LORA
COL_GQ = COL_KPE + LANE
COL_GK = COL_GQ + GLA_SLOTS
COL_GV = COL_GK + GLA_SLOTS
COL_GG = COL_GV + GLA_SLOTS
COL_LR = COL_GG + GLA_SLOTS
IN_PAD = COL_LR + LANE

MIX_PAD = F_WIDTH + MLA_WIDTH + GLA_SLOTS

FF_CHUNK = 256
FF_NCHUNK = D_FF // FF_CHUNK
FF_HALO = SUBLANE_BF16

LOG2E = 1.4426950408889634

ROW_TILE = 256
BF = jnp.bfloat16
F32 = jnp.float32


def _cparams(*sem):
    return pltpu.CompilerParams(dimension_semantics=sem, vmem_limit_bytes=VMEM_LIMIT)


def _dot(a, b):
    return jnp.dot(a, b, preferred_element_type=F32)


def _dot_nt(a, b):
    return lax.dot_general(a, b, (((1,), (1,)), ((), ())), preferred_element_type=F32)


def _dot_tn(a, b):
    return lax.dot_general(a, b, (((0,), (0,)), ((), ())), preferred_element_type=F32)


def _full(shape):
    n = len(shape)
    return pl.BlockSpec(shape, lambda *_: (0,) * n)


def _layer(shape, l):
    n = len(shape)
    return pl.BlockSpec((None,) + tuple(shape), lambda *_: (l,) + (0,) * n)


def _rot_partner(n_rope):
    q = n_rope // 4
    src = np.zeros(n_rope, np.int64)
    sgn = np.zeros(n_rope, np.float32)
    for base in (0, 2 * q):
        for j in range(q):
            src[base + j] = base + q + j
            sgn[base + j] = -1.0
            src[base + q + j] = base + j
            sgn[base + q + j] = 1.0
    return src, sgn


def _take_cols(w, src, sgn=None):
    src = np.asarray(src)
    idx = np.where(src >= 0, src, 0)
    scale = (src >= 0).astype(np.float32)
    if sgn is not None:
        scale = scale * sgn
    return jnp.take(w, jnp.asarray(idx, jnp.int32), axis=-1) * jnp.asarray(scale)


def _pad_rows(w, n):
    pad = [(0, 0)] * w.ndim
    pad[-2] = (0, n - w.shape[-2])
    return jnp.pad(w, pad)


def _in_proj_layout():
    src = -np.ones(IN_PAD, np.int64)
    sgn = np.ones(IN_PAD, np.float32)
    src[COL_F:COL_F + F_WIDTH] = np.arange(F_WIDTH)
    o = F_WIDTH
    src[COL_CQ:COL_CQ + MLA_Q_LORA] = o + np.arange(MLA_Q_LORA)
    o += MLA_Q_LORA
    src[COL_CKV:COL_CKV + MLA_KV_LORA] = o + np.arange(MLA_KV_LORA)
    o += MLA_KV_LORA
    src[COL_KPE:COL_KPE + MLA_ROPE] = o + np.arange(MLA_ROPE)
    psrc, psgn = _rot_partner(MLA_ROPE)
    src[COL_KPE + MLA_ROPE:COL_KPE + 2 * MLA_ROPE] = o + psrc
    sgn[COL_KPE + MLA_ROPE:COL_KPE + 2 * MLA_ROPE] = psgn
    o += MLA_ROPE
    for h in range(GLA_HEADS):
        src[COL_GQ + h * HEAD_SLOT:COL_GQ + h * HEAD_SLOT + GLA_DK] = o + h * GLA_DK + np.arange(GLA_DK)
    o += GLA_QK_W
    for h in range(GLA_HEADS):
        src[COL_GK + h * HEAD_SLOT:COL_GK + h * HEAD_SLOT + GLA_DK] = o + h * GLA_DK + np.arange(GLA_DK)
    o += GLA_QK_W
    for h in range(GLA_HEADS):
        src[COL_GV + h * HEAD_SLOT:COL_GV + h * HEAD_SLOT + GLA_DV] = o + h * GLA_DV + np.arange(GLA_DV)
    o += GLA_WIDTH
    src[COL_LR:COL_LR + GLA_GATE_RANK] = o + np.arange(GLA_GATE_RANK)
    o += GLA_GATE_RANK
    for h in range(GLA_HEADS):
        src[COL_GG + h * HEAD_SLOT:COL_GG + h * HEAD_SLOT + GLA_DV] = o + h * GLA_DV + np.arange(GLA_DV)
    return src, sgn


def _head_slots(n_heads, d_src, d_take, src_off=0):
    src = -np.ones(n_heads * HEAD_SLOT, np.int64)
    for h in range(n_heads):
        src[h * HEAD_SLOT:h * HEAD_SLOT + d_take] = h * d_src + src_off + np.arange(d_take)
    return src


def _prep_weights(w):
    f32 = lambda a: a.astype(F32)
    out = {}
    src, sgn = _in_proj_layout()
    out["w_in"] = _take_cols(w["w_in"], src, sgn).astype(BF)

    m = np.arange(F_DIM)
    ang = 2.0 * np.pi * np.outer(m, m) / F_DIM
    c64, s64 = np.cos(ang), np.sin(ang)
    dft = np.zeros((F_WIDTH, 2 * F_WIDTH), np.float32)
    for g in range(F_GROUPS):
        dft[g * F_DIM:(g + 1) * F_DIM, g * F_DIM:(g + 1) * F_DIM] = c64
        dft[g * F_DIM:(g + 1) * F_DIM, F_WIDTH + g * F_DIM:F_WIDTH + (g + 1) * F_DIM] = s64
    out["dft64"] = jnp.asarray(dft, F32).astype(BF)

    psrc, psgn = _rot_partner(MLA_ROPE)
    q_src = _head_slots(MLA_HEADS, MLA_QK, MLA_QK)
    q_part = -np.ones(MLA_SLOTS, np.int64)
    q_psg = np.ones(MLA_SLOTS, np.float32)
    g_part = -np.ones(MLA_SLOTS, np.int64)
    for h in range(MLA_HEADS):
        lo = h * HEAD_SLOT + MLA_NOPE
        q_part[lo:lo + MLA_ROPE] = h * MLA_QK + MLA_NOPE + psrc
        q_psg[lo:lo + MLA_ROPE] = psgn
        g_part[lo:lo + MLA_ROPE] = MLA_NOPE + psrc
    g_src = np.where(q_src >= 0, q_src % MLA_QK, -1)
    wq = jnp.concatenate([_take_cols(w["mla_w_uq"], q_src), _take_cols(w["mla_w_uq"], q_part, q_psg)], axis=-1)
    out["w_uq"] = _pad_rows(wq, 256).astype(BF)
    out["q_lora_w"] = jnp.pad(f32(w["mla_q_lora_norm_w"]), ((0, 0), (0, 256 - MLA_Q_LORA)))[:, None, :]
    out["kv_lora_w"] = f32(w["mla_kv_lora_norm_w"])[:, None, :]
    out["q_gain"] = _take_cols(w["mla_q_norm_w"], g_src)[:, None, :]
    out["q_gain_p"] = _take_cols(w["mla_q_norm_w"], g_part)[:, None, :]
    out["k_gain"] = _take_cols(w["mla_k_norm_w"], g_src)[:, None, :]
    out["k_gain_p"] = _take_cols(w["mla_k_norm_w"], g_part)[:, None, :]

    kn_src = _head_slots(MLA_HEADS, MLA_NOPE + MLA_V, MLA_NOPE)
    v_src = np.concatenate([h * (MLA_NOPE + MLA_V) + MLA_NOPE + np.arange(MLA_V) for h in range(MLA_HEADS)])
    out["w_ukv"] = jnp.concatenate([_take_cols(w["mla_w_ukv"], kn_src), _take_cols(w["mla_w_ukv"], v_src)],
                                   axis=-1).astype(BF)

    e2 = np.zeros((LANE, 2 * MLA_SLOTS), np.float32)
    for h in range(MLA_HEADS):
        for j in range(MLA_ROPE):
            e2[j, h * HEAD_SLOT + MLA_NOPE + j] = 1.0
            e2[MLA_ROPE + j, MLA_SLOTS + h * HEAD_SLOT + MLA_NOPE + j] = 1.0
    out["kpe_place"] = jnp.asarray(e2, BF)

    gk_src = _head_slots(GLA_HEADS, GLA_DK, GLA_DK)
    wgk = jnp.concatenate([_take_cols(w["gla_w_gk_fwd"], gk_src), _take_cols(w["gla_w_gk_bwd"], gk_src)], axis=-1)
    out["w_gk"] = _pad_rows(wgk, LANE).astype(BF)
    out["b_gk"] = jnp.concatenate([_take_cols(w["gla_b_gk_fwd"], gk_src), _take_cols(w["gla_b_gk_bwd"], gk_src)],
                                  axis=-1)[:, None, :]
    gv_src = _head_slots(GLA_HEADS, GLA_DV, GLA_DV)
    out["gla_gain"] = _take_cols(w["gla_norm_w"], np.where(gv_src >= 0, gv_src % GLA_DV, -1))[:, None, :]

    row_src = np.concatenate([np.arange(F_WIDTH + MLA_WIDTH), np.where(gv_src >= 0, F_WIDTH + MLA_WIDTH + gv_src, -1)])
    w_out_t = _take_cols(jnp.swapaxes(w["w_out"], -1, -2), row_src)
    out["w_out"] = jnp.swapaxes(w_out_t, -1, -2).astype(BF)

    nd = w["ffn_w_12"].shape[0]
    w12 = w["ffn_w_12"].reshape(nd, D_MODEL, 2, FF_NCHUNK, FF_CHUNK)
    out["w12"] = jnp.transpose(w12, (0, 3, 1, 2, 4)).reshape(nd, FF_NCHUNK, D_MODEL, 2 * FF_CHUNK).astype(BF)
    cw = w["ffn_conv_w"].reshape(nd, 3, 2, FF_NCHUNK, FF_CHUNK)
    cb = w["ffn_conv_b"].reshape(nd, 1, 2, FF_NCHUNK, FF_CHUNK)
    cwb = jnp.concatenate([cw, cb], axis=1)
    out["conv"] = jnp.transpose(cwb, (0, 3, 1, 2, 4)).reshape(nd, FF_NCHUNK, 4, 2 * FF_CHUNK).astype(F32)
    out["w_down"] = w["ffn_w_down"].reshape(nd, FF_NCHUNK, FF_CHUNK, D_MODEL).astype(BF)
    out["norm1_w"] = f32(w["norm1_w"])[:, None, :]
    out["norm2_w"] = f32(w["norm2_w"])[:, None, :]
    return out


def _rope_tables(n_tokens):
    q = MLA_ROPE // 4
    t = np.arange(n_tokens)
    row = (t // GRID_W).astype(np.float32)
    col = (t % GRID_W).astype(np.float32)
    axis_dims = MLA_ROPE // 2
    inv_freq = np.power(np.float32(ROPE_BASE), -np.arange(0, axis_dims, 2, dtype=np.float32) / axis_dims)
    ang_r = row[:, None] * inv_freq
    ang_c = col[:, None] * inv_freq
    cos = np.ones((n_tokens, HEAD_SLOT), np.float32)
    sin = np.zeros((n_tokens, HEAD_SLOT), np.float32)
    for k, ang in enumerate((ang_r, ang_r, ang_c, ang_c)):
        lo = MLA_NOPE + k * q
        cos[:, lo:lo + q] = np.cos(ang)
        sin[:, lo:lo + q] = np.sin(ang)
    return jnp.asarray(cos), jnp.asarray(sin)


def _fft_tables(n1, n2):
    k = np.arange(n1)
    ang1 = 2.0 * np.pi * np.outer(k, k) / n1
    fr, fi = np.cos(ang1), -np.sin(ang1)
    w1 = np.block([[fr, fi], [fi, -fr]])
    length = n1 * n2
    kk = (np.arange(n1)[:, None] + n1 * np.arange(n2)[None, :]).astype(np.float64)
    t2 = np.arange(n2, dtype=np.float64)
    ang2 = 2.0 * np.pi * kk[:, :, None] * t2[None, None, :] / length
    norm = 1.0 / math.sqrt(length * F_DIM)
    tab = np.concatenate([np.cos(ang2), np.sin(ang2)], axis=-1) * norm
    return jnp.asarray(w1, F32).astype(BF), jnp.asarray(tab, F32).astype(BF)


def _ctx_dft_table(n):
    k = np.arange(n)
    ang = 2.0 * np.pi * np.outer(k, k) / n
    norm = 1.0 / math.sqrt(n * F_DIM)
    return jnp.asarray(np.concatenate([np.cos(ang), -np.sin(ang)], axis=1) * norm, F32).astype(BF)


def _mod_kernel(c_ref, w_ref, b_ref, o_ref):
    c = c_ref[...]
    s = c * (1.0 / (1.0 + jnp.exp(-c)))
    o_ref[...] = _dot(s.astype(BF), w_ref[...].astype(BF)) + b_ref[...]


def _adaln(cond8, mod_w, mod_b):
    nd, d, n = mod_w.shape
    tn = 1536
    return pl.pallas_call(
        _mod_kernel,
        out_shape=jax.ShapeDtypeStruct((nd, 8, n), F32),
        grid=(nd, n // tn),
        in_specs=[pl.BlockSpec((8, d), lambda l, j: (0, 0)),
                  pl.BlockSpec((None, d, tn), lambda l, j: (l, 0, j)),
                  pl.BlockSpec((None, 1, tn), lambda l, j: (l, 0, j))],
        out_specs=pl.BlockSpec((None, 8, tn), lambda l, j: (l, 0, j)),
        compiler_params=_cparams("arbitrary", "arbitrary"),
        name="adaln",
    )(cond8, mod_w, mod_b.reshape(nd, 1, n))


def _rms(x, n):
    return lax.rsqrt(jnp.sum(x * x, axis=-1, keepdims=True) * (1.0 / n) + EPS)


def _in_kernel(x_ref, sh_ref, sc_ref, n1_ref, win_ref, dft_ref, qlw_ref, wq_ref, kvlw_ref, wkv_ref, place_ref,
               qg_ref, qgp_ref, kg_ref, kgp_ref, cos_ref, sin_ref, wgk_ref, bgk_ref,
               fa_ref, fb_ref, q_ref, k_ref, v_ref, gq_ref, gk_ref, gv_ref, gg_ref, df_ref, db_ref,
               *, row, rope):
    x = x_ref[...]
    xn = x * _rms(x, D_MODEL) * n1_ref[...]
    h = xn * (1.0 + sc_ref[row:row + 1, :]) + sh_ref[row:row + 1, :]
    p = _dot(h.astype(BF), win_ref[...])

    ab = _dot(p[:, COL_F:COL_F + F_WIDTH].astype(BF), dft_ref[...])
    fa_ref[...] = ab[:, :F_WIDTH].astype(BF)
    fb_ref[...] = ab[:, F_WIDTH:].astype(BF)

    cq = p[:, COL_CQ:COL_CQ + 256]
    cqn = cq * _rms(cq, MLA_Q_LORA) * qlw_ref[...]
    qq = _dot(cqn.astype(BF), wq_ref[...])
    ckv = p[:, COL_CKV:COL_CKV + MLA_KV_LORA]
    ckvn = ckv * _rms(ckv, MLA_KV_LORA) * kvlw_ref[...]
    kk = _dot(ckvn.astype(BF), wkv_ref[...])
    kp = _dot(p[:, COL_KPE:COL_KPE + LANE].astype(BF), place_ref[...])
    v_ref[...] = kk[:, MLA_SLOTS:].astype(BF)

    q_scale = (MLA_QK ** -0.5) * LOG2E
    for hd in range(MLA_HEADS):
        sl = slice(hd * HEAD_SLOT, (hd + 1) * HEAD_SLOT)
        sp = slice(MLA_SLOTS + hd * HEAD_SLOT, MLA_SLOTS + (hd + 1) * HEAD_SLOT)
        qh = qq[:, sl]
        kh = kk[:, sl] + kp[:, sl]
        rq = _rms(qh, MLA_QK) * q_scale
        rk = _rms(kh, MLA_QK)
        if rope:
            cos = cos_ref[...]
            sin = sin_ref[...]
            qo = rq * (qh * (qg_ref[:, sl] * cos) + qq[:, sp] * (qgp_ref[:, sl] * sin))
            ko = rk * (kh * (kg_ref[:, sl] * cos) + kp[:, sp] * (kgp_ref[:, sl] * sin))
        else:
            qo = rq * (qh * qg_ref[:, sl])
            ko = rk * (kh * kg_ref[:, sl])
        q_ref[:, sl] = qo.astype(BF)
        k_ref[:, sl] = ko.astype(BF)

    gq_ref[...] = (p[:, COL_GQ:COL_GQ + GLA_SLOTS] * (GLA_DK ** -0.5)).astype(BF)
    gk_ref[...] = p[:, COL_GK:COL_GK + GLA_SLOTS].astype(BF)
    gv_ref[...] = p[:, COL_GV:COL_GV + GLA_SLOTS].astype(BF)
    gg_ref[...] = p[:, COL_GG:COL_GG + GLA_SLOTS].astype(BF)
    z = _dot(p[:, COL_LR:COL_LR + LANE].astype(BF), wgk_ref[...]) + bgk_ref[...]
    ls = (jnp.minimum(z, 0.0) - jnp.log1p(jnp.exp(-jnp.abs(z)))) * (1.0 / GLA_GATE_NORM)
    df_ref[...] = ls[:, :GLA_SLOTS]
    db_ref[...] = ls[:, GLA_SLOTS:]


def _in_proj(x2, mods, pw, cos_t, sin_t, l, row, rope):
    rows = x2.shape[0]
    tm = min(ROW_TILE, rows)
    rt = lambda n: pl.BlockSpec((tm, n), lambda i: (i, 0))
    modspec = lambda k: pl.BlockSpec((None, 8, D_MODEL), lambda i: (l, 0, k))
    outs = [(F_WIDTH, BF), (F_WIDTH, BF), (MLA_SLOTS, BF), (MLA_SLOTS, BF), (MLA_WIDTH, BF),
            (GLA_SLOTS, BF), (GLA_SLOTS, BF), (GLA_SLOTS, BF), (GLA_SLOTS, BF), (GLA_SLOTS, F32), (GLA_SLOTS, F32)]
    return pl.pallas_call(
        functools.partial(_in_kernel, row=row, rope=rope),
        out_shape=[jax.ShapeDtypeStruct((rows, n), dt) for n, dt in outs],
        grid=(rows // tm,),
        in_specs=[rt(D_MODEL), modspec(0), modspec(1), _layer((1, D_MODEL), l),
                  _layer((D_MODEL, IN_PAD), l), _full((F_WIDTH, 2 * F_WIDTH)),
                  _layer((1, 256), l), _layer((256, 2 * MLA_SLOTS), l),
                  _layer((1, MLA_KV_LORA), l), _layer((MLA_KV_LORA, MLA_SLOTS + MLA_WIDTH), l),
                  _full((LANE, 2 * MLA_SLOTS)),
                  _layer((1, MLA_SLOTS), l), _layer((1, MLA_SLOTS), l), _layer((1, MLA_SLOTS), l),
                  _layer((1, MLA_SLOTS), l),
                  rt(HEAD_SLOT), rt(HEAD_SLOT),
                  _layer((LANE, 2 * GLA_SLOTS), l), _layer((1, 2 * GLA_SLOTS), l)],
        out_specs=[rt(n) for n, _ in outs],
        compiler_params=_cparams("arbitrary"),
        name="in_proj",
    )(x2, mods, mods, pw["norm1_w"], pw["w_in"], pw["dft64"], pw["q_lora_w"], pw["w_uq"], pw["kv_lora_w"],
      pw["w_ukv"], pw["kpe_place"], pw["q_gain"], pw["q_gain_p"], pw["k_gain"], pw["k_gain_p"],
      cos_t, sin_t, pw["w_gk"], pw["b_gk"])


def _fft1_kernel(a_ref, b_ref, w_ref, g_ref):
    n1 = a_ref.shape[0]
    ab = jnp.concatenate([a_ref[...], b_ref[...]], axis=0)
    g = _dot(w_ref[...], ab)
    g_ref[0] = g[:n1].astype(BF)
    g_ref[1] = g[n1:].astype(BF)


def _fft2_kernel(g_ref, t_ref, o_ref, *, batch):
    for j in range(batch):
        g = jnp.concatenate([g_ref[0, j], g_ref[1, j]], axis=0)
        o_ref[:, j * F_WIDTH:(j + 1) * F_WIDTH] = _dot(t_ref[j], g).astype(BF)


def _fourier_latent(fa, fb, w1, tab):
    length = fa.shape[0]
    n1, n2 = tab.shape[0], tab.shape[1]
    cols = n2 * F_WIDTH
    tn = min(4096, cols)
    g = pl.pallas_call(
        _fft1_kernel,
        out_shape=jax.ShapeDtypeStruct((2, n1, cols), BF),
        grid=(cols // tn,),
        in_specs=[pl.BlockSpec((n1, tn), lambda j: (0, j)), pl.BlockSpec((n1, tn), lambda j: (0, j)),
                  _full((2 * n1, 2 * n1))],
        out_specs=pl.BlockSpec((2, n1, tn), lambda j: (0, 0, j)),
        compiler_params=_cparams("arbitrary"),
        name="fft_stage1",
    )(fa.reshape(n1, cols), fb.reshape(n1, cols), w1)
    batch = 8
    y = pl.pallas_call(
        functools.partial(_fft2_kernel, batch=batch),
        out_shape=jax.ShapeDtypeStruct((n2, n1 * F_WIDTH), BF),
        grid=(n1 // batch,),
        in_specs=[pl.BlockSpec((2, batch, n2, F_WIDTH), lambda i: (0, i, 0, 0)),
                  pl.BlockSpec((batch, n2, 2 * n2), lambda i: (i, 0, 0))],
        out_specs=pl.BlockSpec((n2, batch * F_WIDTH), lambda i: (0, i)),
        compiler_params=_cparams("arbitrary"),
        name="fft_stage2",
    )(g.reshape(2, n1, n2, F_WIDTH), tab)
    return y.reshape(length, F_WIDTH)


def _fctx_kernel(a_ref, b_ref, t_ref, o_ref):
    ab = jnp.concatenate([a_ref[...], b_ref[...]], axis=0)
    o_ref[...] = _dot(t_ref[...], ab).astype(BF)


def _fourier_ctx(fa, fb, tab):
    n = fa.shape[0]
    return pl.pallas_call(
        _fctx_kernel,
        out_shape=jax.ShapeDtypeStruct((n, F_WIDTH), BF),
        grid=(1,),
        in_specs=[_full((n, F_WIDTH)), _full((n, F_WIDTH)), _full((n, 2 * n))],
        out_specs=_full((n, F_WIDTH)),
        compiler_params=_cparams("arbitrary"),
        name="fft_ctx",
    )(fa, fb, tab)


def _attn_kernel(*refs, tk, has_x):
    if has_x:
        q_ref, kc_ref, vc_ref, kx_ref, vx_ref, o_ref = refs
    else:
        q_ref, kc_ref, vc_ref, o_ref = refs
    tq = q_ref.shape[0]
    outs = []
    for hh in range(2):
        sl = slice(hh * HEAD_SLOT, (hh + 1) * HEAD_SLOT)
        q = q_ref[:, sl]

        def step(k, v, carry):
            m, l, acc = carry
            s = _dot_nt(q, k)
            m_new = jnp.maximum(m, jnp.max(s, axis=-1, keepdims=True))
            alpha = jnp.exp2(m - m_new)
            p = jnp.exp2(s - m_new)
            l = alpha * l + jnp.sum(p, axis=-1, keepdims=True)
            acc = alpha * acc + _dot(p.astype(BF), v)
            return m_new, l, acc

        carry = (jnp.full((tq, 1), -1e30, F32), jnp.zeros((tq, 1), F32), jnp.zeros((tq, HEAD_SLOT), F32))
        carry = step(kc_ref[:, sl], vc_ref[...], carry)
        if has_x:
            def body(j, carry):
                off = pl.multiple_of(j * tk, tk)
                return step(kx_ref[pl.ds(off, tk), sl], vx_ref[pl.ds(off, tk), :], carry)

            carry = lax.fori_loop(0, kx_ref.shape[0] // tk, body, carry)
        _, l, acc = carry
        outs.append(acc / l)
    lane = lax.broadcasted_iota(jnp.int32, (tq, HEAD_SLOT), 1)
    o_ref[...] = jnp.where(lane < MLA_V, outs[0], outs[1]).astype(BF)


def _attention(q, kc, vc, kx=None, vx=None):
    rows = q.shape[0]
    lc = kc.shape[0]
    tq = min(256, rows)
    has_x = kx is not None
    pair = 2 * HEAD_SLOT
    in_specs = [pl.BlockSpec((tq, pair), lambda p, i: (i, p)),
                pl.BlockSpec((lc, pair), lambda p, i: (0, p)),
                pl.BlockSpec((lc, HEAD_SLOT), lambda p, i: (0, p))]
    args = [q, kc, vc]
    tk = 512
    if has_x:
        lx = kx.shape[0]
        tk = min(tk, lx)
        in_specs += [pl.BlockSpec((lx, pair), lambda p, i: (0, p)),
                     pl.BlockSpec((lx, HEAD_SLOT), lambda p, i: (0, p))]
        args += [kx, vx]
    return pl.pallas_call(
        functools.partial(_attn_kernel, tk=tk, has_x=has_x),
        out_shape=jax.ShapeDtypeStruct((rows, MLA_WIDTH), BF),
        grid=(MLA_HEADS // 2, rows // tq),
        in_specs=in_specs,
        out_specs=pl.BlockSpec((tq, HEAD_SLOT), lambda p, i: (i, p)),
        compiler_params=_cparams("arbitrary", "arbitrary"),
        name="attention",
    )(*args)


def _gla_kernel(qf_ref, kf_ref, vf_ref, df_ref, qb_ref, kb_ref, vb_ref, db_ref, s0_ref,
                of_ref, ob_ref, sfin_ref, st_ref):
    i = pl.program_id(0)
    c = qf_ref.shape[0]

    @pl.when(i == 0)
    def _():
        st_ref[...] = s0_ref[...]

    r = lax.broadcasted_iota(jnp.int32, (c, c), 0)
    s = lax.broadcasted_iota(jnp.int32, (c, c), 1)
    dirs = ((qf_ref, kf_ref, vf_ref, df_ref, of_ref, s <= r, c - 1),
            (qb_ref, kb_ref, vb_ref, db_ref, ob_ref, s >= r, 0))
    for d, (q_ref, k_ref, v_ref, g_ref, o_ref, mask, last) in enumerate(dirs):
        tri = mask.astype(F32)
        b = jnp.dot(tri, g_ref[...], preferred_element_type=F32, precision=lax.Precision.HIGHEST)
        b_tot = b[last:last + 1, :]
        e_in = jnp.exp(b)
        q_in = (q_ref[...].astype(F32) * e_in).astype(BF)
        kf = k_ref[...].astype(F32)
        k_in = (kf * jnp.exp(-b)).astype(BF)
        k_out = (kf * jnp.exp(b_tot - b)).astype(BF)
        dec = jnp.exp(b_tot)
        v = v_ref[...]
        for h in range(GLA_HEADS):
            sl = slice(h * HEAD_SLOT, (h + 1) * HEAD_SLOT)
            st = st_ref[d, h]
            a = jnp.where(mask, _dot_nt(q_in[:, sl], k_in[:, sl]), 0.0)
            o = _dot(a.astype(BF), v[:, sl]) + _dot_nt(q_in[:, sl], st.astype(BF))
            o_ref[:, sl] = o
            st_ref[d, h] = st * dec[:, sl] + _dot_tn(v[:, sl], k_out[:, sl])

    @pl.when(i == pl.num_programs(0) - 1)
    def _():
        sfin_ref[...] = st_ref[...]


def _gla(gq, gk, gv, df, db, s0):
    rows = gq.shape[0]
    c = GLA_CHUNK
    n = rows // c
    fwd = pl.BlockSpec((c, GLA_SLOTS), lambda i: (i, 0))
    bwd = pl.BlockSpec((c, GLA_SLOTS), lambda i: (n - 1 - i, 0))
    st_shape = (2, GLA_HEADS, HEAD_SLOT, HEAD_SLOT)
    return pl.pallas_call(
        _gla_kernel,
        out_shape=[jax.ShapeDtypeStruct((rows, GLA_SLOTS), F32), jax.ShapeDtypeStruct((rows, GLA_SLOTS), F32),
                   jax.ShapeDtypeStruct(st_shape, F32)],
        grid=(n,),
        in_specs=[fwd, fwd, fwd, fwd, bwd, bwd, bwd, bwd, _full(st_shape)],
        out_specs=[fwd, bwd, _full(st_shape)],
        scratch_shapes=[pltpu.VMEM(st_shape, F32)],
        compiler_params=_cparams("arbitrary"),
        name="gla_scan",
    )(gq, gk, gv, df, gq, gk, gv, db, s0)


def _out_kernel(x_ref, four_ref, att_ref, of_ref, ob_ref, gg_ref, gain_ref, wout_ref, g1_ref, n2_ref,
                sh_ref, sc_ref, xo_ref, h_ref, *, row):
    o = of_ref[...] + ob_ref[...]
    g = gg_ref[...].astype(F32)
    gate = g * (1.0 / (1.0 + jnp.exp(-g)))
    y = _dot(four_ref[...], wout_ref[0:F_WIDTH, :])
    y += _dot(att_ref[...], wout_ref[F_WIDTH:F_WIDTH + MLA_WIDTH, :])
    for h in range(GLA_HEADS):
        sl = slice(h * HEAD_SLOT, (h + 1) * HEAD_SLOT)
        oh = o[:, sl]
        lin = oh * _rms(oh, GLA_DV) * gain_ref[:, sl] * gate[:, sl]
        lo = F_WIDTH + MLA_WIDTH + h * HEAD_SLOT
        y += _dot(lin.astype(BF), wout_ref[lo:lo + HEAD_SLOT, :])
    x = x_ref[...] + g1_ref[row:row + 1, :] * y
    xo_ref[...] = x
    hn = x * _rms(x, D_MODEL) * n2_ref[...]
    h_ref[...] = (hn * (1.0 + sc_ref[row:row + 1, :]) + sh_ref[row:row + 1, :]).astype(BF)


def _out_proj(x2, four, att, o_f, o_b, gg, mods, pw, l, row):
    rows = x2.shape[0]
    tm = min(ROW_TILE, rows)
    rt = lambda n: pl.BlockSpec((tm, n), lambda i: (i, 0))
    modspec = lambda k: pl.BlockSpec((None, 8, D_MODEL), lambda i: (l, 0, k))
    return pl.pallas_call(
        functools.partial(_out_kernel, row=row),
        out_shape=[jax.ShapeDtypeStruct((rows, D_MODEL), F32), jax.ShapeDtypeStruct((rows, D_MODEL), BF)],
        grid=(rows // tm,),
        in_specs=[rt(D_MODEL), rt(F_WIDTH), rt(MLA_WIDTH), rt(GLA_SLOTS), rt(GLA_SLOTS), rt(GLA_SLOTS),
                  _layer((1, GLA_SLOTS), l), _layer((MIX_PAD, D_MODEL), l), modspec(2),
                  _layer((1, D_MODEL), l), modspec(3), modspec(4)],
        out_specs=[rt(D_MODEL), rt(D_MODEL)],
        compiler_params=_cparams("arbitrary"),
        name="out_proj",
    )(x2, four, att, o_f, o_b, gg, pw["gla_gain"], pw["w_out"], mods, pw["norm2_w"], mods, mods)


def _ffn_kernel(x_ref, h_ref, hp_ref, hn_ref, w12_ref, conv_ref, wd_ref, g2_ref, o_ref, acc_ref, *, row):
    i = pl.program_id(0)
    tm = h_ref.shape[0]
    m = tm + 2 * FF_HALO
    keep_prev = jnp.where(i > 0, 1.0, 0.0)
    keep_next = jnp.where(i < pl.num_programs(0) - 1, 1.0, 0.0)
    h_prev = (hp_ref[...].astype(F32) * keep_prev).astype(BF)
    h_next = (hn_ref[...].astype(F32) * keep_next).astype(BF)
    h_ext = jnp.concatenate([h_prev, h_ref[...], h_next], axis=0)
    acc_ref[...] = jnp.zeros_like(acc_ref)

    def body(c, carry):
        u = _dot(h_ext, w12_ref[c])
        cw = conv_ref[c]
        u_prev = pltpu.roll(u, 1, axis=0)
        u_next = pltpu.roll(u, m - 1, axis=0)
        uc = u_prev * cw[0:1, :] + u * cw[1:2, :] + u_next * cw[2:3, :] + cw[3:4, :]
        uc = uc[FF_HALO:FF_HALO + tm, :]
        a = uc[:, :FF_CHUNK]
        act = a * (1.0 / (1.0 + jnp.exp(-a))) * uc[:, FF_CHUNK:]
        acc_ref[...] += _dot(act.astype(BF), wd_ref[c])
        return carry

    lax.fori_loop(0, FF_NCHUNK, body, 0)
    o_ref[...] = x_ref[...] + g2_ref[row:row + 1, :] * acc_ref[...]


def _ffn(x2, h2, mods, pw, l, row):
    rows = x2.shape[0]
    tm = min(ROW_TILE, rows)
    nt = rows // tm
    per = tm // FF_HALO
    last_blk = rows // FF_HALO - 1
    rt = lambda n: pl.BlockSpec((tm, n), lambda i: (i, 0))
    return pl.pallas_call(
        functools.partial(_ffn_kernel, row=row),
        out_shape=jax.ShapeDtypeStruct((rows, D_MODEL), F32),
        grid=(nt,),
        in_specs=[rt(D_MODEL), rt(D_MODEL),
                  pl.BlockSpec((FF_HALO, D_MODEL), lambda i: (jnp.maximum(i * per - 1, 0), 0)),
                  pl.BlockSpec((FF_HALO, D_MODEL), lambda i: (jnp.minimum((i + 1) * per, last_blk), 0)),
                  _layer((FF_NCHUNK, D_MODEL, 2 * FF_CHUNK), l), _layer((FF_NCHUNK, 4, 2 * FF_CHUNK), l),
                  _layer((FF_NCHUNK, FF_CHUNK, D_MODEL), l),
                  pl.BlockSpec((None, 8, D_MODEL), lambda i: (l, 0, 5))],
        out_specs=rt(D_MODEL),
        scratch_shapes=[pltpu.VMEM((tm, D_MODEL), F32)],
        compiler_params=_cparams("arbitrary"),
        name="conv_ffn",
    )(x2, h2, h2, h2, pw["w12"], pw["conv"], pw["w_down"], mods)


def kernel(x, c, ctx, c_ctx, mod_w, mod_b, norm1_w, norm2_w, w_in, mla_q_lora_norm_w, mla_w_uq, mla_kv_lora_norm_w, mla_w_ukv, mla_q_norm_w, mla_k_norm_w, gla_w_gk_fwd, gla_b_gk_fwd, gla_w_gk_bwd, gla_b_gk_bwd, gla_norm_w, w_out, ffn_w_12, ffn_conv_w, ffn_conv_b, ffn_w_down):
    batch, seq, d = x.shape
    assert batch == 1 and d == D_MODEL
    lc = ctx.shape[1]
    depth = mod_w.shape[0]
    n1 = int(round(math.sqrt(seq)))
    assert n1 * n1 == seq and seq % ROW_TILE == 0 and lc % GLA_CHUNK == 0

    pw = _prep_weights(dict(
        w_in=w_in, mla_w_uq=mla_w_uq, mla_q_lora_norm_w=mla_q_lora_norm_w, mla_kv_lora_norm_w=mla_kv_lora_norm_w,
        mla_w_ukv=mla_w_ukv, mla_q_norm_w=mla_q_norm_w, mla_k_norm_w=mla_k_norm_w, gla_w_gk_fwd=gla_w_gk_fwd,
        gla_b_gk_fwd=gla_b_gk_fwd, gla_w_gk_bwd=gla_w_gk_bwd, gla_b_gk_bwd=gla_b_gk_bwd, gla_norm_w=gla_norm_w,
        w_out=w_out, ffn_w_12=ffn_w_12, ffn_conv_w=ffn_conv_w, ffn_conv_b=ffn_conv_b, ffn_w_down=ffn_w_down,
        norm1_w=norm1_w, norm2_w=norm2_w))
    cos_t, sin_t = _rope_tables(seq)
    ones_c = jnp.ones((lc, HEAD_SLOT), F32)
    w1, tab = _fft_tables(n1, n1)
    tab_c = _ctx_dft_table(lc)

    cond8 = jnp.zeros((8, d), F32).at[0].set(c[0].astype(F32)).at[1].set(c_ctx.astype(F32))
    mods = _adaln(cond8, mod_w, mod_b)

    xs = x[0].astype(F32)
    xc = ctx[0].astype(F32)
    s_zero = jnp.zeros((2, GLA_HEADS, HEAD_SLOT, HEAD_SLOT), F32)
    for l in range(depth):
        last = l == depth - 1
        fa_c, fb_c, q_c, k_c, v_c, gq_c, gk_c, gv_c, gg_c, df_c, db_c = _in_proj(xc, mods, pw, ones_c, ones_c, l, 1, False)
        fa_x, fb_x, q_x, k_x, v_x, gq_x, gk_x, gv_x, gg_x, df_x, db_x = _in_proj(xs, mods, pw, cos_t, sin_t, l, 0, True)

        of_c, ob_c, s_c = _gla(gq_c, gk_c, gv_c, df_c, db_c, s_zero)
        of_x, ob_x, _ = _gla(gq_x, gk_x, gv_x, df_x, db_x, s_c)
        four_x = _fourier_latent(fa_x, fb_x, w1, tab)
        att_x = _attention(q_x, k_c, v_c, k_x, v_x)
        x_mid, h2 = _out_proj(xs, four_x, att_x, of_x, ob_x, gg_x, mods, pw, l, 0)
        xs = _ffn(x_mid, h2, mods, pw, l, 0)
        if not last:
            four_c = _fourier_ctx(fa_c, fb_c, tab_c)
            att_c = _attention(q_c, k_c, v_c)
            c_mid, hc2 = _out_proj(xc, four_c, att_c, of_c, ob_c, gg_c, mods, pw, l, 1)
            xc = _ffn(c_mid, hc2, mods, pw, l, 1)
    return xs[None].astype(x.dtype)
```

```python
import functools
import math

import numpy as np
import jax
import jax.numpy as jnp
from jax import lax
from jax.experimental import pallas as pl
from jax.experimental.pallas import tpu as pltpu

D_MODEL = 1024
DEPTH = 2
GRID_W = 64
EPS = 1e-6

F_GROUPS = 4
F_DIM = 64
F_WIDTH = F_GROUPS * F_DIM

MLA_HEADS = 6
MLA_Q_LORA = 192
MLA_KV_LORA = 128
MLA_NOPE = 64
MLA_ROPE = 32
MLA_V = 64
MLA_QK = MLA_NOPE + MLA_ROPE
MLA_WIDTH = MLA_HEADS * MLA_V
MLA_IN = MLA_Q_LORA + MLA_KV_LORA + MLA_ROPE
ROPE_BASE = 10000.0

GLA_HEADS = 4
GLA_DK = 48
GLA_DV = 96
GLA_GATE_RANK = 16
GLA_GATE_NORM = 16.0
GLA_CHUNK = 64
GLA_WIDTH = GLA_HEADS * GLA_DV
GLA_QK_W = GLA_HEADS * GLA_DK

MIX_WIDTH = F_WIDTH + MLA_WIDTH + GLA_WIDTH
D_FF = 2816
N_MOD = 6

LANE = 128
SUBLANE_BF16 = 16
VMEM_LIMIT = 48 * 1024 * 1024

HEAD_SLOT = LANE
MLA_SLOTS = MLA_HEADS * HEAD_SLOT
GLA_SLOTS = GLA_HEADS * HEAD_SLOT

COL_F = 0
COL_CQ = COL_F + F_WIDTH
COL_CKV = COL_CQ + 256
COL_KPE = COL_CKV + MLA_KV_LORA
COL_GQ = COL_KPE + LANE
COL_GK = COL_GQ + GLA_SLOTS
COL_GV = COL_GK + GLA_SLOTS
COL_GG = COL_GV + GLA_SLOTS
COL_LR = COL_GG + GLA_SLOTS
IN_PAD = COL_LR + LANE

MIX_PAD = F_WIDTH + MLA_SLOTS + GLA_SLOTS

FF_CHUNK = 256
FF_NCHUNK = D_FF // FF_CHUNK
FF_HALO = SUBLANE_BF16

LOG2E = 1.4426950408889634

ROW_TILE = 256
BF = jnp.bfloat16
F32 = jnp.float32


def _cparams(*sem):
    return pltpu.CompilerParams(dimension_semantics=sem, vmem_limit_bytes=VMEM_LIMIT)


def _dot(a, b):
    return jnp.dot(a, b, preferred_element_type=F32)


def _dot_nt(a, b):
    return lax.dot_general(a, b, (((1,), (1,)), ((), ())), preferred_element_type=F32)


def _dot_tn(a, b):
    return lax.dot_general(a, b, (((0,), (0,)), ((), ())), preferred_element_type=F32)


def _full(shape):
    n = len(shape)
    return pl.BlockSpec(shape, lambda *_: (0,) * n)


def _layer(shape, l):
    n = len(shape)
    return pl.BlockSpec((None,) + tuple(shape), lambda *_: (l,) + (0,) * n)


def _rot_partner(n_rope):
    q = n_rope // 4
    src = np.zeros(n_rope, np.int64)
    sgn = np.zeros(n_rope, np.float32)
    for base in (0, 2 * q):
        for j in range(q):
            src[base + j] = base + q + j
            sgn[base + j] = -1.0
            src[base + q + j] = base + j
            sgn[base + q + j] = 1.0
    return src, sgn


def _take_cols(w, src, sgn=None, axis=-1):
    src = np.asarray(src)
    sgn = np.ones(len(src), np.float32) if sgn is None else np.asarray(sgn, np.float32)
    axis = axis % w.ndim
    pieces = []
    lo = 0
    while lo < len(src):
        hi = lo + 1
        if src[lo] < 0:
            while hi < len(src) and src[hi] < 0:
                hi += 1
            shape = list(w.shape)
            shape[axis] = hi - lo
            pieces.append(jnp.zeros(shape, w.dtype))
        else:
            while hi < len(src) and src[hi] == src[hi - 1] + 1 and sgn[hi] == sgn[lo]:
                hi += 1
            piece = lax.slice_in_dim(w, int(src[lo]), int(src[lo]) + hi - lo, axis=axis)
            pieces.append(-piece if sgn[lo] < 0 else piece)
        lo = hi
    return jnp.concatenate(pieces, axis=axis)


def _pad_rows(w, n):
    pad = [(0, 0)] * w.ndim
    pad[-2] = (0, n - w.shape[-2])
    return jnp.pad(w, pad)


def _in_proj_layout():
    src = -np.ones(IN_PAD, np.int64)
    sgn = np.ones(IN_PAD, np.float32)
    src[COL_F:COL_F + F_WIDTH] = np.arange(F_WIDTH)
    o = F_WIDTH
    src[COL_CQ:COL_CQ + MLA_Q_LORA] = o + np.arange(MLA_Q_LORA)
    o += MLA_Q_LORA
    src[COL_CKV:COL_CKV + MLA_KV_LORA] = o + np.arange(MLA_KV_LORA)
    o += MLA_KV_LORA
    src[COL_KPE:COL_KPE + MLA_ROPE] = o + np.arange(MLA_ROPE)
    psrc, psgn = _rot_partner(MLA_ROPE)
    src[COL_KPE + MLA_ROPE:COL_KPE + 2 * MLA_ROPE] = o + psrc
    sgn[COL_KPE + MLA_ROPE:COL_KPE + 2 * MLA_ROPE] = psgn
    o += MLA_ROPE
    for h in range(GLA_HEADS):
        src[COL_GQ + h * HEAD_SLOT:COL_GQ + h * HEAD_SLOT + GLA_DK] = o + h * GLA_DK + np.arange(GLA_DK)
    o += GLA_QK_W
    for h in range(GLA_HEADS):
        src[COL_GK + h * HEAD_SLOT:COL_GK + h * HEAD_SLOT + GLA_DK] = o + h * GLA_DK + np.arange(GLA_DK)
    o += GLA_QK_W
    for h in range(GLA_HEADS):
        src[COL_GV + h * HEAD_SLOT:COL_GV + h * HEAD_SLOT + GLA_DV] = o + h * GLA_DV + np.arange(GLA_DV)
    o += GLA_WIDTH
    src[COL_LR:COL_LR + GLA_GATE_RANK] = o + np.arange(GLA_GATE_RANK)
    o += GLA_GATE_RANK
    for h in range(GLA_HEADS):
        src[COL_GG + h * HEAD_SLOT:COL_GG + h * HEAD_SLOT + GLA_DV] = o + h * GLA_DV + np.arange(GLA_DV)
    return src, sgn


def _head_slots(n_heads, d_src, d_take, src_off=0):
    src = -np.ones(n_heads * HEAD_SLOT, np.int64)
    for h in range(n_heads):
        src[h * HEAD_SLOT:h * HEAD_SLOT + d_take] = h * d_src + src_off + np.arange(d_take)
    return src


def _prep_weights(w):
    f32 = lambda a: a.astype(F32)
    out = {}
    src, sgn = _in_proj_layout()
    out["w_in"] = _take_cols(w["w_in"], src, sgn).astype(BF)

    m = np.arange(F_DIM)
    ang = 2.0 * np.pi * np.outer(m, m) / F_DIM
    c64, s64 = np.cos(ang), np.sin(ang)
    dft = np.zeros((F_WIDTH, 2 * F_WIDTH), np.float32)
    for g in range(F_GROUPS):
        dft[g * F_DIM:(g + 1) * F_DIM, g * F_DIM:(g + 1) * F_DIM] = c64
        dft[g * F_DIM:(g + 1) * F_DIM, F_WIDTH + g * F_DIM:F_WIDTH + (g + 1) * F_DIM] = s64
    out["dft64"] = jnp.asarray(dft, F32).astype(BF)

    psrc, psgn = _rot_partner(MLA_ROPE)
    q_src = _head_slots(MLA_HEADS, MLA_QK, MLA_QK)
    q_part = -np.ones(MLA_SLOTS, np.int64)
    q_psg = np.ones(MLA_SLOTS, np.float32)
    g_part = -np.ones(MLA_SLOTS, np.int64)
    for h in range(MLA_HEADS):
        lo = h * HEAD_SLOT + MLA_NOPE
        q_part[lo:lo + MLA_ROPE] = h * MLA_QK + MLA_NOPE + psrc
        q_psg[lo:lo + MLA_ROPE] = psgn
        g_part[lo:lo + MLA_ROPE] = MLA_NOPE + psrc
    g_src = np.where(q_src >= 0, q_src % MLA_QK, -1)
    wq = jnp.concatenate([_take_cols(w["mla_w_uq"], q_src), _take_cols(w["mla_w_uq"], q_part, q_psg)], axis=-1)
    out["w_uq"] = _pad_rows(wq, 256).astype(BF)
    out["q_lora_w"] = jnp.pad(f32(w["mla_q_lora_norm_w"]), ((0, 0), (0, 256 - MLA_Q_LORA)))[:, None, :]
    out["kv_lora_w"] = f32(w["mla_kv_lora_norm_w"])[:, None, :]
    out["q_gain"] = _take_cols(w["mla_q_norm_w"], g_src)[:, None, :]
    out["q_gain_p"] = _take_cols(w["mla_q_norm_w"], g_part)[:, None, :]
    out["k_gain"] = _take_cols(w["mla_k_norm_w"], g_src)[:, None, :]
    out["k_gain_p"] = _take_cols(w["mla_k_norm_w"], g_part)[:, None, :]

    kn_src = _head_slots(MLA_HEADS, MLA_NOPE + MLA_V, MLA_NOPE)
    v_src = _head_slots(MLA_HEADS, MLA_NOPE + MLA_V, MLA_V, MLA_NOPE)
    out["w_ukv"] = jnp.concatenate([_take_cols(w["mla_w_ukv"], kn_src), _take_cols(w["mla_w_ukv"], v_src)],
                                   axis=-1).astype(BF)

    e2 = np.zeros((LANE, 2 * MLA_SLOTS), np.float32)
    for h in range(MLA_HEADS):
        for j in range(MLA_ROPE):
            e2[j, h * HEAD_SLOT + MLA_NOPE + j] = 1.0
            e2[MLA_ROPE + j, MLA_SLOTS + h * HEAD_SLOT + MLA_NOPE + j] = 1.0
    out["kpe_place"] = jnp.asarray(e2, BF)

    gk_src = _head_slots(GLA_HEADS, GLA_DK, GLA_DK)
    wgk = jnp.concatenate([_take_cols(w["gla_w_gk_fwd"], gk_src), _take_cols(w["gla_w_gk_bwd"], gk_src)], axis=-1)
    out["w_gk"] = _pad_rows(wgk, LANE).astype(BF)
    out["b_gk"] = jnp.concatenate([_take_cols(w["gla_b_gk_fwd"], gk_src), _take_cols(w["gla_b_gk_bwd"], gk_src)],
                                  axis=-1)[:, None, :]
    gv_src = _head_slots(GLA_HEADS, GLA_DV, GLA_DV)
    out["gla_gain"] = _take_cols(w["gla_norm_w"], np.where(gv_src >= 0, gv_src % GLA_DV, -1))[:, None, :]

    av_src = _head_slots(MLA_HEADS, MLA_V, MLA_V)
    row_src = np.concatenate([np.arange(F_WIDTH), np.where(av_src >= 0, F_WIDTH + av_src, -1),
                              np.where(gv_src >= 0, F_WIDTH + MLA_WIDTH + gv_src, -1)])
    out["w_out"] = _take_cols(w["w_out"], row_src, axis=-2).astype(BF)

    nd = w["ffn_w_12"].shape[0]
    w12 = w["ffn_w_12"].astype(BF).reshape(nd, D_MODEL, 2, FF_NCHUNK, FF_CHUNK)
    out["w12"] = jnp.transpose(w12, (0, 3, 1, 2, 4)).reshape(nd, FF_NCHUNK, D_MODEL, 2 * FF_CHUNK)
    cw = w["ffn_conv_w"].reshape(nd, 3, 2, FF_NCHUNK, FF_CHUNK)
    cb = w["ffn_conv_b"].reshape(nd, 1, 2, FF_NCHUNK, FF_CHUNK)
    cwb = jnp.concatenate([cw, cb], axis=1)
    out["conv"] = jnp.transpose(cwb, (0, 3, 1, 2, 4)).reshape(nd, FF_NCHUNK, 4, 2 * FF_CHUNK).astype(F32)
    out["w_down"] = w["ffn_w_down"].reshape(nd, FF_NCHUNK, FF_CHUNK, D_MODEL).astype(BF)
    out["norm1_w"] = f32(w["norm1_w"])[:, None, :]
    out["norm2_w"] = f32(w["norm2_w"])[:, None, :]
    return out


def _rope_tables(n_tokens):
    q = MLA_ROPE // 4
    t = np.arange(n_tokens)
    row = (t // GRID_W).astype(np.float32)
    col = (t % GRID_W).astype(np.float32)
    axis_dims = MLA_ROPE // 2
    inv_freq = np.power(np.float32(ROPE_BASE), -np.arange(0, axis_dims, 2, dtype=np.float32) / axis_dims)
    ang_r = row[:, None] * inv_freq
    ang_c = col[:, None] * inv_freq
    cos = np.ones((n_tokens, HEAD_SLOT), np.float32)
    sin = np.zeros((n_tokens, HEAD_SLOT), np.float32)
    for k, ang in enumerate((ang_r, ang_r, ang_c, ang_c)):
        lo = MLA_NOPE + k * q
        cos[:, lo:lo + q] = np.cos(ang)
        sin[:, lo:lo + q] = np.sin(ang)
    return jnp.asarray(cos), jnp.asarray(sin)


def _fft_tables(n1, n2):
    k = np.arange(n1)
    ang1 = 2.0 * np.pi * np.outer(k, k) / n1
    fr, fi = np.cos(ang1), -np.sin(ang1)
    w1 = np.block([[fr, fi], [fi, -fr]])
    length = n1 * n2
    kk = (np.arange(n1)[:, None] + n1 * np.arange(n2)[None, :]).astype(np.float64)
    t2 = np.arange(n2, dtype=np.float64)
    ang2 = 2.0 * np.pi * kk[:, :, None] * t2[None, None, :] / length
    norm = 1.0 / math.sqrt(length * F_DIM)
    tab = np.concatenate([np.cos(ang2), np.sin(ang2)], axis=-1) * norm
    return jnp.asarray(w1, F32).astype(BF), jnp.asarray(tab, F32).astype(BF)


def _ctx_dft_table(n):
    k = np.arange(n)
    ang = 2.0 * np.pi * np.outer(k, k) / n
    norm = 1.0 / math.sqrt(n * F_DIM)
    return jnp.asarray(np.concatenate([np.cos(ang), -np.sin(ang)], axis=1) * norm, F32).astype(BF)


def _mod_kernel(c_ref, w_ref, b_ref, o_ref):
    c = c_ref[...]
    s = c * (1.0 / (1.0 + jnp.exp(-c)))
    o_ref[...] = _dot(s.astype(BF), w_ref[...].astype(BF)) + b_ref[...]


def _adaln(cond8, mod_w, mod_b):
    nd, d, n = mod_w.shape
    tn = 1536
    return pl.pallas_call(
        _mod_kernel,
        out_shape=jax.ShapeDtypeStruct((nd, 8, n), F32),
        grid=(nd, n // tn),
        in_specs=[pl.BlockSpec((8, d), lambda l, j: (0, 0)),
                  pl.BlockSpec((None, d, tn), lambda l, j: (l, 0, j)),
                  pl.BlockSpec((None, 1, tn), lambda l, j: (l, 0, j))],
        out_specs=pl.BlockSpec((None, 8, tn), lambda l, j: (l, 0, j)),
        compiler_params=_cparams("arbitrary", "arbitrary"),
        name="adaln",
    )(cond8, mod_w, mod_b.reshape(nd, 1, n))


def _rms(x, n):
    return lax.rsqrt(jnp.sum(x * x, axis=-1, keepdims=True) * (1.0 / n) + EPS)


def _in_kernel(x_ref, sh_ref, sc_ref, n1_ref, win_ref, dft_ref, qlw_ref, wq_ref, kvlw_ref, wkv_ref, place_ref,
               qg_ref, qgp_ref, kg_ref, kgp_ref, cos_ref, sin_ref, wgk_ref, bgk_ref,
               fa_ref, fb_ref, q_ref, k_ref, v_ref, gq_ref, gk_ref, gv_ref, gg_ref, df_ref, db_ref,
               *, row, rope):
    x = x_ref[...]
    xn = x * _rms(x, D_MODEL) * n1_ref[...]
    h = xn * (1.0 + sc_ref[row:row + 1, :]) + sh_ref[row:row + 1, :]
    p = _dot(h.astype(BF), win_ref[...])

    ab = _dot(p[:, COL_F:COL_F + F_WIDTH].astype(BF), dft_ref[...])
    fa_ref[...] = ab[:, :F_WIDTH].astype(BF)
    fb_ref[...] = ab[:, F_WIDTH:].astype(BF)

    cq = p[:, COL_CQ:COL_CQ + 256]
    cqn = cq * _rms(cq, MLA_Q_LORA) * qlw_ref[...]
    qq = _dot(cqn.astype(BF), wq_ref[...])
    ckv = p[:, COL_CKV:COL_CKV + MLA_KV_LORA]
    ckvn = ckv * _rms(ckv, MLA_KV_LORA) * kvlw_ref[...]
    kk = _dot(ckvn.astype(BF), wkv_ref[...])
    kp = _dot(p[:, COL_KPE:COL_KPE + LANE].astype(BF), place_ref[...])
    v_lane = lax.broadcasted_iota(jnp.int32, (x.shape[0], MLA_SLOTS), 1) % HEAD_SLOT
    v_ref[...] = jnp.where(v_lane == MLA_V, 1.0, kk[:, MLA_SLOTS:]).astype(BF)

    q_scale = (MLA_QK ** -0.5) * LOG2E
    for hd in range(MLA_HEADS):
        sl = slice(hd * HEAD_SLOT, (hd + 1) * HEAD_SLOT)
        sp = slice(MLA_SLOTS + hd * HEAD_SLOT, MLA_SLOTS + (hd + 1) * HEAD_SLOT)
        qh = qq[:, sl]
        kh = kk[:, sl] + kp[:, sl]
        rq = _rms(qh, MLA_QK) * q_scale
        rk = _rms(kh, MLA_QK)
        if rope:
            cos = cos_ref[...]
            sin = sin_ref[...]
            qo = rq * (qh * (qg_ref[:, sl] * cos) + qq[:, sp] * (qgp_ref[:, sl] * sin))
            ko = rk * (kh * (kg_ref[:, sl] * cos) + kp[:, sp] * (kgp_ref[:, sl] * sin))
        else:
            qo = rq * (qh * qg_ref[:, sl])
            ko = rk * (kh * kg_ref[:, sl])
        q_ref[:, sl] = qo.astype(BF)
        k_ref[:, sl] = ko.astype(BF)

    gq_ref[...] = (p[:, COL_GQ:COL_GQ + GLA_SLOTS] * (GLA_DK ** -0.5)).astype(BF)
    gk_ref[...] = p[:, COL_GK:COL_GK + GLA_SLOTS].astype(BF)
    gv_ref[...] = p[:, COL_GV:COL_GV + GLA_SLOTS].astype(BF)
    gg_ref[...] = p[:, COL_GG:COL_GG + GLA_SLOTS].astype(BF)
    z = _dot(p[:, COL_LR:COL_LR + LANE].astype(BF), wgk_ref[...]) + bgk_ref[...]
    ls = (jnp.minimum(z, 0.0) - jnp.log1p(jnp.exp(-jnp.abs(z)))) * (1.0 / GLA_GATE_NORM)
    df_ref[...] = ls[:, :GLA_SLOTS]
    db_ref[...] = ls[:, GLA_SLOTS:]


def _in_proj(x2, mods, pw, cos_t, sin_t, l, row, rope):
    rows = x2.shape[0]
    tm = min(ROW_TILE, rows)
    rt = lambda n: pl.BlockSpec((tm, n), lambda i: (i, 0))
    modspec = lambda k: pl.BlockSpec((None, 8, D_MODEL), lambda i: (l, 0, k))
    outs = [(F_WIDTH, BF), (F_WIDTH, BF), (MLA_SLOTS, BF), (MLA_SLOTS, BF), (MLA_SLOTS, BF),
            (GLA_SLOTS, BF), (GLA_SLOTS, BF), (GLA_SLOTS, BF), (GLA_SLOTS, BF), (GLA_SLOTS, F32), (GLA_SLOTS, F32)]
    return pl.pallas_call(
        functools.partial(_in_kernel, row=row, rope=rope),
        out_shape=[jax.ShapeDtypeStruct((rows, n), dt) for n, dt in outs],
        grid=(rows // tm,),
        in_specs=[rt(D_MODEL), modspec(0), modspec(1), _layer((1, D_MODEL), l),
                  _layer((D_MODEL, IN_PAD), l), _full((F_WIDTH, 2 * F_WIDTH)),
                  _layer((1, 256), l), _layer((256, 2 * MLA_SLOTS), l),
                  _layer((1, MLA_KV_LORA), l), _layer((MLA_KV_LORA, 2 * MLA_SLOTS), l),
                  _full((LANE, 2 * MLA_SLOTS)),
                  _layer((1, MLA_SLOTS), l), _layer((1, MLA_SLOTS), l), _layer((1, MLA_SLOTS), l),
                  _layer((1, MLA_SLOTS), l),
                  rt(HEAD_SLOT), rt(HEAD_SLOT),
                  _layer((LANE, 2 * GLA_SLOTS), l), _layer((1, 2 * GLA_SLOTS), l)],
        out_specs=[rt(n) for n, _ in outs],
        compiler_params=_cparams("arbitrary"),
        name="in_proj",
    )(x2, mods, mods, pw["norm1_w"], pw["w_in"], pw["dft64"], pw["q_lora_w"], pw["w_uq"], pw["kv_lora_w"],
      pw["w_ukv"], pw["kpe_place"], pw["q_gain"], pw["q_gain_p"], pw["k_gain"], pw["k_gain_p"],
      cos_t, sin_t, pw["w_gk"], pw["b_gk"])


def _fft1_kernel(a_ref, b_ref, w_ref, g_ref):
    n1 = a_ref.shape[0]
    ab = jnp.concatenate([a_ref[...], b_ref[...]], axis=0)
    g = _dot(w_ref[...], ab)
    g_ref[0] = g[:n1].astype(BF)
    g_ref[1] = g[n1:].astype(BF)


def _fft2_kernel(g_ref, t_ref, o_ref, *, batch):
    for j in range(batch):
        g = jnp.concatenate([g_ref[0, j], g_ref[1, j]], axis=0)
        o_ref[:, j * F_WIDTH:(j + 1) * F_WIDTH] = _dot(t_ref[j], g).astype(BF)


def _fourier_latent(fa, fb, w1, tab):
    length = fa.shape[0]
    n1, n2 = tab.shape[0], tab.shape[1]
    cols = n2 * F_WIDTH
    tn = min(4096, cols)
    g = pl.pallas_call(
        _fft1_kernel,
        out_shape=jax.ShapeDtypeStruct((2, n1, cols), BF),
        grid=(cols // tn,),
        in_specs=[pl.BlockSpec((n1, tn), lambda j: (0, j)), pl.BlockSpec((n1, tn), lambda j: (0, j)),
                  _full((2 * n1, 2 * n1))],
        out_specs=pl.BlockSpec((2, n1, tn), lambda j: (0, 0, j)),
        compiler_params=_cparams("arbitrary"),
        name="fft_stage1",
    )(fa.reshape(n1, cols), fb.reshape(n1, cols), w1)
    batch = 8
    y = pl.pallas_call(
        functools.partial(_fft2_kernel, batch=batch),
        out_shape=jax.ShapeDtypeStruct((n2, n1 * F_WIDTH), BF),
        grid=(n1 // batch,),
        in_specs=[pl.BlockSpec((2, batch, n2, F_WIDTH), lambda i: (0, i, 0, 0)),
                  pl.BlockSpec((batch, n2, 2 * n2), lambda i: (i, 0, 0))],
        out_specs=pl.BlockSpec((n2, batch * F_WIDTH), lambda i: (0, i)),
        compiler_params=_cparams("arbitrary"),
        name="fft_stage2",
    )(g.reshape(2, n1, n2, F_WIDTH), tab)
    return y.reshape(length, F_WIDTH)


def _fctx_kernel(a_ref, b_ref, t_ref, o_ref):
    ab = jnp.concatenate([a_ref[...], b_ref[...]], axis=0)
    o_ref[...] = _dot(t_ref[...], ab).astype(BF)


def _fourier_ctx(fa, fb, tab):
    n = fa.shape[0]
    return pl.pallas_call(
        _fctx_kernel,
        out_shape=jax.ShapeDtypeStruct((n, F_WIDTH), BF),
        grid=(1,),
        in_specs=[_full((n, F_WIDTH)), _full((n, F_WIDTH)), _full((n, 2 * n))],
        out_specs=_full((n, F_WIDTH)),
        compiler_params=_cparams("arbitrary"),
        name="fft_ctx",
    )(fa, fb, tab)


def _attn_kernel(*refs, tk, has_x):
    if has_x:
        q_ref, kc_ref, vc_ref, kx_ref, vx_ref, o_ref, acc_ref, s_ref = refs
    else:
        q_ref, kc_ref, vc_ref, o_ref, acc_ref = refs
    tq = q_ref.shape[0]
    heads = range(2)
    slot = lambda h: slice(h * HEAD_SLOT, (h + 1) * HEAD_SLOT)
    q = [q_ref[:, slot(h)] for h in heads]

    def scores(k2):
        return tuple(_dot_nt(q[h], k2[:, slot(h)]) for h in heads)

    def absorb(s, v2, m):
        m_new = [jnp.maximum(m[h], jnp.max(s[h], axis=-1, keepdims=True)) for h in heads]
        p = [jnp.exp2(s[h] - m_new[h]).astype(BF) for h in heads]
        for h in heads:
            alpha = jnp.exp2(m[h] - m_new[h])
            acc_ref[h] = alpha * acc_ref[h] + _dot(p[h], v2[:, slot(h)])
        return tuple(m_new)

    acc_ref[...] = jnp.zeros_like(acc_ref)
    m = tuple(jnp.full((tq, 1), -1e30, F32) for _ in heads)
    m = absorb(scores(kc_ref[...]), vc_ref[...], m)
    if has_x:
        n = kx_ref.shape[0] // tk
        assert n % 2 == 0

        def put_scores(buf, j):
            off = pl.multiple_of(j * tk, tk)
            s = scores(kx_ref[pl.ds(off, tk), :])
            for h in heads:
                s_ref[buf, h] = s[h]

        def take(buf, j, m):
            off = pl.multiple_of(j * tk, tk)
            return absorb(tuple(s_ref[buf, h] for h in heads), vx_ref[pl.ds(off, tk), :], m)

        def body(i, m):
            put_scores(1, 2 * i + 1)
            m = take(0, 2 * i, m)
            put_scores(0, jnp.minimum(2 * i + 2, n - 1))
            return take(1, 2 * i + 1, m)

        put_scores(0, 0)
        m = lax.fori_loop(0, n // 2, body, m)
    for h in heads:
        acc = acc_ref[h]
        o_ref[:, slot(h)] = (acc / acc[:, MLA_V:MLA_V + 1]).astype(BF)


def _attention(q, kc, vc, kx=None, vx=None):
    rows = q.shape[0]
    lc = kc.shape[0]
    tq = min(512, rows)
    has_x = kx is not None
    pair = 2 * HEAD_SLOT
    in_specs = [pl.BlockSpec((tq, pair), lambda p, i: (i, p)),
                pl.BlockSpec((lc, pair), lambda p, i: (0, p)),
                pl.BlockSpec((lc, pair), lambda p, i: (0, p))]
    args = [q, kc, vc]
    tk = 512
    if has_x:
        lx = kx.shape[0]
        tk = min(tk, lx)
        in_specs += [pl.BlockSpec((lx, pair), lambda p, i: (0, p)),
                     pl.BlockSpec((lx, pair), lambda p, i: (0, p))]
        args += [kx, vx]
    return pl.pallas_call(
        functools.partial(_attn_kernel, tk=tk, has_x=has_x),
        out_shape=jax.ShapeDtypeStruct((rows, MLA_SLOTS), BF),
        grid=(MLA_HEADS // 2, rows // tq),
        in_specs=in_specs,
        out_specs=pl.BlockSpec((tq, pair), lambda p, i: (i, p)),
        scratch_shapes=[pltpu.VMEM((2, tq, HEAD_SLOT), F32)] + ([pltpu.VMEM((2, 2, tq, tk), F32)] if has_x else []),
        compiler_params=_cparams("arbitrary", "arbitrary"),
        name="attention",
    )(*args)


def _gla_kernel(qf_ref, kf_ref, vf_ref, df_ref, qb_ref, kb_ref, vb_ref, db_ref, s0_ref,
                of_ref, ob_ref, sfin_ref, st_ref):
    i = pl.program_id(0)
    c = qf_ref.shape[0]

    @pl.when(i == 0)
    def _():
        st_ref[...] = s0_ref[...]

    r = lax.broadcasted_iota(jnp.int32, (c, c), 0)
    s = lax.broadcasted_iota(jnp.int32, (c, c), 1)
    dirs = ((qf_ref, kf_ref, vf_ref, df_ref, of_ref, s <= r, c - 1),
            (qb_ref, kb_ref, vb_ref, db_ref, ob_ref, s >= r, 0))
    for d, (q_ref, k_ref, v_ref, g_ref, o_ref, mask, last) in enumerate(dirs):
        tri = mask.astype(F32)
        b = jnp.dot(tri, g_ref[...], preferred_element_type=F32, precision=lax.Precision.HIGHEST)
        b_tot = b[last:last + 1, :]
        e_in = jnp.exp(b)
        q_in = (q_ref[...].astype(F32) * e_in).astype(BF)
        kf = k_ref[...].astype(F32)
        k_in = (kf * jnp.exp(-b)).astype(BF)
        k_out = (kf * jnp.exp(b_tot - b)).astype(BF)
        dec = jnp.exp(b_tot)
        v = v_ref[...]
        for h in range(GLA_HEADS):
            sl = slice(h * HEAD_SLOT, (h + 1) * HEAD_SLOT)
            st = st_ref[d, h]
            a = jnp.where(mask, _dot_nt(q_in[:, sl], k_in[:, sl]), 0.0)
            o = _dot(a.astype(BF), v[:, sl]) + _dot_nt(q_in[:, sl], st.astype(BF))
            o_ref[:, sl] = o
            st_ref[d, h] = st * dec[:, sl] + _dot_tn(v[:, sl], k_out[:, sl])

    @pl.when(i == pl.num_programs(0) - 1)
    def _():
        sfin_ref[...] = st_ref[...]


def _gla(gq, gk, gv, df, db, s0):
    rows = gq.shape[0]
    c = GLA_CHUNK
    n = rows // c
    fwd = pl.BlockSpec((c, GLA_SLOTS), lambda i: (i, 0))
    bwd = pl.BlockSpec((c, GLA_SLOTS), lambda i: (n - 1 - i, 0))
    st_shape = (2, GLA_HEADS, HEAD_SLOT, HEAD_SLOT)
    return pl.pallas_call(
        _gla_kernel,
        out_shape=[jax.ShapeDtypeStruct((rows, GLA_SLOTS), F32), jax.ShapeDtypeStruct((rows, GLA_SLOTS), F32),
                   jax.ShapeDtypeStruct(st_shape, F32)],
        grid=(n,),
        in_specs=[fwd, fwd, fwd, fwd, bwd, bwd, bwd, bwd, _full(st_shape)],
        out_specs=[fwd, bwd, _full(st_shape)],
        scratch_shapes=[pltpu.VMEM(st_shape, F32)],
        compiler_params=_cparams("arbitrary"),
        name="gla_scan",
    )(gq, gk, gv, df, gq, gk, gv, db, s0)


def _out_kernel(x_ref, four_ref, att_ref, of_ref, ob_ref, gg_ref, gain_ref, wout_ref, g1_ref, n2_ref,
                sh_ref, sc_ref, xo_ref, h_ref, *, row):
    o = of_ref[...] + ob_ref[...]
    g = gg_ref[...].astype(F32)
    gate = g * (1.0 / (1.0 + jnp.exp(-g)))
    y = _dot(four_ref[...], wout_ref[0:F_WIDTH, :])
    y += _dot(att_ref[...], wout_ref[F_WIDTH:F_WIDTH + MLA_SLOTS, :])
    for h in range(GLA_HEADS):
        sl = slice(h * HEAD_SLOT, (h + 1) * HEAD_SLOT)
        oh = o[:, sl]
        lin = oh * _rms(oh, GLA_DV) * gain_ref[:, sl] * gate[:, sl]
        lo = F_WIDTH + MLA_SLOTS + h * HEAD_SLOT
        y += _dot(lin.astype(BF), wout_ref[lo:lo + HEAD_SLOT, :])
    x = x_ref[...] + g1_ref[row:row + 1, :] * y
    xo_ref[...] = x
    hn = x * _rms(x, D_MODEL) * n2_ref[...]
    h_ref[...] = (hn * (1.0 + sc_ref[row:row + 1, :]) + sh_ref[row:row + 1, :]).astype(BF)


def _out_proj(x2, four, att, o_f, o_b, gg, mods, pw, l, row):
    rows = x2.shape[0]
    tm = min(ROW_TILE, rows)
    rt = lambda n: pl.BlockSpec((tm, n), lambda i: (i, 0))
    modspec = lambda k: pl.BlockSpec((None, 8, D_MODEL), lambda i: (l, 0, k))
    return pl.pallas_call(
        functools.partial(_out_kernel, row=row),
        out_shape=[jax.ShapeDtypeStruct((rows, D_MODEL), F32), jax.ShapeDtypeStruct((rows, D_MODEL), BF)],
        grid=(rows // tm,),
        in_specs=[rt(D_MODEL), rt(F_WIDTH), rt(MLA_SLOTS), rt(GLA_SLOTS), rt(GLA_SLOTS), rt(GLA_SLOTS),
                  _layer((1, GLA_SLOTS), l), _layer((MIX_PAD, D_MODEL), l), modspec(2),
                  _layer((1, D_MODEL), l), modspec(3), modspec(4)],
        out_specs=[rt(D_MODEL), rt(D_MODEL)],
        compiler_params=_cparams("arbitrary"),
        name="out_proj",
    )(x2, four, att, o_f, o_b, gg, pw["gla_gain"], pw["w_out"], mods, pw["norm2_w"], mods, mods)


def _ffn_kernel(x_ref, h_ref, hp_ref, hn_ref, w12_ref, conv_ref, wd_ref, g2_ref, o_ref, acc_ref, hx_ref, u_ref,
                *, row):
    i = pl.program_id(0)
    tm = h_ref.shape[0]
    m = tm + 2 * FF_HALO
    keep_prev = jnp.where(i > 0, 1.0, 0.0)
    keep_next = jnp.where(i < pl.num_programs(0) - 1, 1.0, 0.0)
    hx_ref[0:FF_HALO, :] = (hp_ref[...].astype(F32) * keep_prev).astype(BF)
    hx_ref[FF_HALO:FF_HALO + tm, :] = h_ref[...]
    hx_ref[FF_HALO + tm:m, :] = (hn_ref[...].astype(F32) * keep_next).astype(BF)
    acc_ref[...] = jnp.zeros_like(acc_ref)

    def up(buf, c):
        u_ref[buf] = _dot(hx_ref[...], w12_ref[c])

    def down(buf, c):
        u = u_ref[buf]
        cw = conv_ref[c]
        u_prev = pltpu.roll(u, 1, axis=0)
        u_next = pltpu.roll(u, m - 1, axis=0)
        uc = u_prev * cw[0:1, :] + u * cw[1:2, :] + u_next * cw[2:3, :] + cw[3:4, :]
        uc = uc[FF_HALO:FF_HALO + tm, :]
        a = uc[:, :FF_CHUNK]
        act = a * (1.0 / (1.0 + jnp.exp(-a))) * uc[:, FF_CHUNK:]
        acc_ref[...] += _dot(act.astype(BF), wd_ref[c])

    def body(j, carry):
        up(1, 2 * j + 1)
        down(0, 2 * j)
        up(0, 2 * j + 2)
        down(1, 2 * j + 1)
        return carry

    assert FF_NCHUNK % 2 == 1
    up(0, 0)
    lax.fori_loop(0, FF_NCHUNK // 2, body, 0)
    down(0, FF_NCHUNK - 1)
    o_ref[...] = x_ref[...] + g2_ref[row:row + 1, :] * acc_ref[...]


def _ffn(x2, h2, mods, pw, l, row):
    rows = x2.shape[0]
    tm = min(512, rows)
    nt = rows // tm
    per = tm // FF_HALO
    last_blk = rows // FF_HALO - 1
    m = tm + 2 * FF_HALO
    rt = lambda n: pl.BlockSpec((tm, n), lambda i: (i, 0))
    once = lambda shape: pl.BlockSpec((None,) + shape, lambda i: (l,) + (0,) * len(shape),
                                      pipeline_mode=pl.Buffered(1))
    return pl.pallas_call(
        functools.partial(_ffn_kernel, row=row),
        out_shape=jax.ShapeDtypeStruct((rows, D_MODEL), F32),
        grid=(nt,),
        in_specs=[rt(D_MODEL), rt(D_MODEL),
                  pl.BlockSpec((FF_HALO, D_MODEL), lambda i: (jnp.maximum(i * per - 1, 0), 0)),
                  pl.BlockSpec((FF_HALO, D_MODEL), lambda i: (jnp.minimum((i + 1) * per, last_blk), 0)),
                  once((FF_NCHUNK, D_MODEL, 2 * FF_CHUNK)), once((FF_NCHUNK, 4, 2 * FF_CHUNK)),
                  once((FF_NCHUNK, FF_CHUNK, D_MODEL)),
                  pl.BlockSpec((None, 8, D_MODEL), lambda i: (l, 0, 5))],
        out_specs=rt(D_MODEL),
        scratch_shapes=[pltpu.VMEM((tm, D_MODEL), F32), pltpu.VMEM((m, D_MODEL), BF),
                        pltpu.VMEM((2, m, 2 * FF_CHUNK), F32)],
        compiler_params=_cparams("arbitrary"),
        name="conv_ffn",
    )(x2, h2, h2, h2, pw["w12"], pw["conv"], pw["w_down"], mods)


def kernel(x, c, ctx, c_ctx, mod_w, mod_b, norm1_w, norm2_w, w_in, mla_q_lora_norm_w, mla_w_uq, mla_kv_lora_norm_w, mla_w_ukv, mla_q_norm_w, mla_k_norm_w, gla_w_gk_fwd, gla_b_gk_fwd, gla_w_gk_bwd, gla_b_gk_bwd, gla_norm_w, w_out, ffn_w_12, ffn_conv_w, ffn_conv_b, ffn_w_down):
    batch, seq, d = x.shape
    assert batch == 1 and d == D_MODEL
    lc = ctx.shape[1]
    depth = mod_w.shape[0]
    n1 = int(round(math.sqrt(seq)))
    assert n1 * n1 == seq and seq % ROW_TILE == 0 and lc % GLA_CHUNK == 0

    pw = _prep_weights(dict(
        w_in=w_in, mla_w_uq=mla_w_uq, mla_q_lora_norm_w=mla_q_lora_norm_w, mla_kv_lora_norm_w=mla_kv_lora_norm_w,
        mla_w_ukv=mla_w_ukv, mla_q_norm_w=mla_q_norm_w, mla_k_norm_w=mla_k_norm_w, gla_w_gk_fwd=gla_w_gk_fwd,
        gla_b_gk_fwd=gla_b_gk_fwd, gla_w_gk_bwd=gla_w_gk_bwd, gla_b_gk_bwd=gla_b_gk_bwd, gla_norm_w=gla_norm_w,
        w_out=w_out, ffn_w_12=ffn_w_12, ffn_conv_w=ffn_conv_w, ffn_conv_b=ffn_conv_b, ffn_w_down=ffn_w_down,
        norm1_w=norm1_w, norm2_w=norm2_w))
    cos_t, sin_t = _rope_tables(seq)
    ones_c = jnp.ones((lc, HEAD_SLOT), F32)
    w1, tab = _fft_tables(n1, n1)
    tab_c = _ctx_dft_table(lc)

    cond8 = jnp.zeros((8, d), F32).at[0].set(c[0].astype(F32)).at[1].set(c_ctx.astype(F32))
    mods = _adaln(cond8, mod_w, mod_b)

    xs = x[0].astype(F32)
    xc = ctx[0].astype(F32)
    s_zero = jnp.zeros((2, GLA_HEADS, HEAD_SLOT, HEAD_SLOT), F32)
    for l in range(depth):
        last = l == depth - 1
        fa_c, fb_c, q_c, k_c, v_c, gq_c, gk_c, gv_c, gg_c, df_c, db_c = _in_proj(xc, mods, pw, ones_c, ones_c, l, 1, False)
        fa_x, fb_x, q_x, k_x, v_x, gq_x, gk_x, gv_x, gg_x, df_x, db_x = _in_proj(xs, mods, pw, cos_t, sin_t, l, 0, True)

        of_c, ob_c, s_c = _gla(gq_c, gk_c, gv_c, df_c, db_c, s_zero)
        of_x, ob_x, _ = _gla(gq_x, gk_x, gv_x, df_x, db_x, s_c)
        four_x = _fourier_latent(fa_x, fb_x, w1, tab)
        att_x = _attention(q_x, k_c, v_c, k_x, v_x)
        x_mid, h2 = _out_proj(xs, four_x, att_x, of_x, ob_x, gg_x, mods, pw, l, 0)
        xs = _ffn(x_mid, h2, mods, pw, l, 0)
        if not last:
            four_c = _fourier_ctx(fa_c, fb_c, tab_c)
            att_c = _attention(q_c, k_c, v_c)
            c_mid, hc2 = _out_proj(xc, four_c, att_c, of_c, ob_c, gg_c, mods, pw, l, 1)
            xc = _ffn(c_mid, hc2, mods, pw, l, 1)
    return xs[None].astype(x.dtype)
```

```python
import functools
import math

import numpy as np
import jax
import jax.numpy as jnp
from jax import lax
from jax.experimental import pallas as pl
from jax.experimental.pallas import tpu as pltpu

D_MODEL = 1024
DEPTH = 2
GRID_W = 64
EPS = 1e-6

F_GROUPS = 4
F_DIM = 64
F_WIDTH = F_GROUPS * F_DIM

MLA_HEADS = 6
MLA_Q_LORA = 192
MLA_KV_LORA = 128
MLA_NOPE = 64
MLA_ROPE = 32
MLA_V = 64
MLA_QK = MLA_NOPE + MLA_ROPE
MLA_WIDTH = MLA_HEADS * MLA_V
MLA_IN = MLA_Q_LORA + MLA_KV_LORA + MLA_ROPE
ROPE_BASE = 10000.0

GLA_HEADS = 4
GLA_DK = 48
GLA_DV = 96
GLA_GATE_RANK = 16
GLA_GATE_NORM = 16.0
GLA_CHUNK = 64
GLA_STEP = 256
GLA_WIDTH = GLA_HEADS * GLA_DV
GLA_QK_W = GLA_HEADS * GLA_DK

MIX_WIDTH = F_WIDTH + MLA_WIDTH + GLA_WIDTH
D_FF = 2816
N_MOD = 6

LANE = 128
SUBLANE_BF16 = 16
VMEM_LIMIT = 48 * 1024 * 1024

HEAD_SLOT = LANE
V_ROWS = 80
MLA_SLOTS = MLA_HEADS * HEAD_SLOT
GLA_SLOTS = GLA_HEADS * HEAD_SLOT

COL_F = 0
COL_CQ = COL_F + F_WIDTH
COL_CKV = COL_CQ + 256
COL_KPE = COL_CKV + MLA_KV_LORA
COL_GQ = COL_KPE + LANE
COL_GK = COL_GQ + GLA_SLOTS
COL_GV = COL_GK + GLA_SLOTS
COL_GG = COL_GV + GLA_SLOTS
COL_LR = COL_GG + GLA_SLOTS
IN_PAD = COL_LR + LANE

MIX_PAD = F_WIDTH + MLA_WIDTH + GLA_SLOTS

FF_CHUNK = 256
FF_NCHUNK = D_FF // FF_CHUNK
FF_HALO = SUBLANE_BF16

LOG2E = 1.4426950408889634

ROW_TILE = 256
IN_SUB = 256
BF = jnp.bfloat16
F32 = jnp.float32


def _cparams(*sem):
    return pltpu.CompilerParams(dimension_semantics=sem, vmem_limit_bytes=VMEM_LIMIT)


def _dot(a, b):
    return jnp.dot(a, b, preferred_element_type=F32)


def _dot_nt(a, b):
    return lax.dot_general(a, b, (((1,), (1,)), ((), ())), preferred_element_type=F32)


def _dot_tn(a, b):
    return lax.dot_general(a, b, (((0,), (0,)), ((), ())), preferred_element_type=F32)


def _full(shape):
    n = len(shape)
    return pl.BlockSpec(shape, lambda *_: (0,) * n)


def _layer(shape, l):
    n = len(shape)
    return pl.BlockSpec((None,) + tuple(shape), lambda *_: (l,) + (0,) * n)


def _rot_partner(n_rope):
    q = n_rope // 4
    src = np.zeros(n_rope, np.int64)
    sgn = np.zeros(n_rope, np.float32)
    for base in (0, 2 * q):
        for j in range(q):
            src[base + j] = base + q + j
            sgn[base + j] = -1.0
            src[base + q + j] = base + j
            sgn[base + q + j] = 1.0
    return src, sgn


def _take_cols(w, src, sgn=None, axis=-1):
    src = np.asarray(src)
    sgn = np.ones(len(src), np.float32) if sgn is None else np.asarray(sgn, np.float32)
    axis = axis % w.ndim
    pieces = []
    lo = 0
    while lo < len(src):
        hi = lo + 1
        if src[lo] < 0:
            while hi < len(src) and src[hi] < 0:
                hi += 1
            shape = list(w.shape)
            shape[axis] = hi - lo
            pieces.append(jnp.zeros(shape, w.dtype))
        else:
            while hi < len(src) and src[hi] == src[hi - 1] + 1 and sgn[hi] == sgn[lo]:
                hi += 1
            piece = lax.slice_in_dim(w, int(src[lo]), int(src[lo]) + hi - lo, axis=axis)
            pieces.append(-piece if sgn[lo] < 0 else piece)
        lo = hi
    return jnp.concatenate(pieces, axis=axis)


def _pad_rows(w, n):
    pad = [(0, 0)] * w.ndim
    pad[-2] = (0, n - w.shape[-2])
    return jnp.pad(w, pad)


def _in_proj_layout():
    src = -np.ones(IN_PAD, np.int64)
    sgn = np.ones(IN_PAD, np.float32)
    src[COL_F:COL_F + F_WIDTH] = np.arange(F_WIDTH)
    o = F_WIDTH
    src[COL_CQ:COL_CQ + MLA_Q_LORA] = o + np.arange(MLA_Q_LORA)
    o += MLA_Q_LORA
    src[COL_CKV:COL_CKV + MLA_KV_LORA] = o + np.arange(MLA_KV_LORA)
    o += MLA_KV_LORA
    src[COL_KPE:COL_KPE + MLA_ROPE] = o + np.arange(MLA_ROPE)
    psrc, psgn = _rot_partner(MLA_ROPE)
    src[COL_KPE + MLA_ROPE:COL_KPE + 2 * MLA_ROPE] = o + psrc
    sgn[COL_KPE + MLA_ROPE:COL_KPE + 2 * MLA_ROPE] = psgn
    o += MLA_ROPE
    for h in range(GLA_HEADS):
        src[COL_GQ + h * HEAD_SLOT:COL_GQ + h * HEAD_SLOT + GLA_DK] = o + h * GLA_DK + np.arange(GLA_DK)
    o += GLA_QK_W
    for h in range(GLA_HEADS):
        src[COL_GK + h * HEAD_SLOT:COL_GK + h * HEAD_SLOT + GLA_DK] = o + h * GLA_DK + np.arange(GLA_DK)
    o += GLA_QK_W
    for h in range(GLA_HEADS):
        src[COL_GV + h * HEAD_SLOT:COL_GV + h * HEAD_SLOT + GLA_DV] = o + h * GLA_DV + np.arange(GLA_DV)
    o += GLA_WIDTH
    src[COL_LR:COL_LR + GLA_GATE_RANK] = o + np.arange(GLA_GATE_RANK)
    o += GLA_GATE_RANK
    for h in range(GLA_HEADS):
        src[COL_GG + h * HEAD_SLOT:COL_GG + h * HEAD_SLOT + GLA_DV] = o + h * GLA_DV + np.arange(GLA_DV)
    return src, sgn


def _head_slots(n_heads, d_src, d_take, src_off=0):
    src = -np.ones(n_heads * HEAD_SLOT, np.int64)
    for h in range(n_heads):
        src[h * HEAD_SLOT:h * HEAD_SLOT + d_take] = h * d_src + src_off + np.arange(d_take)
    return src


def _prep_weights(w):
    f32 = lambda a: a.astype(F32)
    out = {}
    src, sgn = _in_proj_layout()
    out["w_in"] = _take_cols(w["w_in"], src, sgn).astype(BF)

    m = np.arange(F_DIM)
    ang = 2.0 * np.pi * np.outer(m, m) / F_DIM
    c64, s64 = np.cos(ang), np.sin(ang)
    dft = np.zeros((F_WIDTH, 2 * F_WIDTH), np.float32)
    for g in range(F_GROUPS):
        dft[g * F_DIM:(g + 1) * F_DIM, g * F_DIM:(g + 1) * F_DIM] = c64
        dft[g * F_DIM:(g + 1) * F_DIM, F_WIDTH + g * F_DIM:F_WIDTH + (g + 1) * F_DIM] = s64
    out["dft64"] = jnp.asarray(dft, F32).astype(BF)

    psrc, psgn = _rot_partner(MLA_ROPE)
    q_src = _head_slots(MLA_HEADS, MLA_QK, MLA_QK)
    q_part = -np.ones(MLA_SLOTS, np.int64)
    q_psg = np.ones(MLA_SLOTS, np.float32)
    g_part = -np.ones(MLA_SLOTS, np.int64)
    for h in range(MLA_HEADS):
        lo = h * HEAD_SLOT + MLA_NOPE
        q_part[lo:lo + MLA_ROPE] = h * MLA_QK + MLA_NOPE + psrc
        q_psg[lo:lo + MLA_ROPE] = psgn
        g_part[lo:lo + MLA_ROPE] = MLA_NOPE + psrc
    g_src = np.where(q_src >= 0, q_src % MLA_QK, -1)
    wq = jnp.concatenate([_take_cols(w["mla_w_uq"], q_src), _take_cols(w["mla_w_uq"], q_part, q_psg)], axis=-1)
    out["w_uq"] = _pad_rows(wq, 256).astype(BF)
    out["q_lora_w"] = jnp.pad(f32(w["mla_q_lora_norm_w"]), ((0, 0), (0, 256 - MLA_Q_LORA)))[:, None, :]
    out["kv_lora_w"] = f32(w["mla_kv_lora_norm_w"])[:, None, :]
    out["q_gain"] = _take_cols(w["mla_q_norm_w"], g_src)[:, None, :]
    out["q_gain_p"] = _take_cols(w["mla_q_norm_w"], g_part)[:, None, :]
    out["k_gain"] = _take_cols(w["mla_k_norm_w"], g_src)[:, None, :]
    out["k_gain_p"] = _take_cols(w["mla_k_norm_w"], g_part)[:, None, :]

    kn_src = _head_slots(MLA_HEADS, MLA_NOPE + MLA_V, MLA_NOPE)
    v_src = _head_slots(MLA_HEADS, MLA_NOPE + MLA_V, MLA_V, MLA_NOPE)
    out["w_ukv"] = jnp.concatenate([_take_cols(w["mla_w_ukv"], kn_src), _take_cols(w["mla_w_ukv"], v_src)],
                                   axis=-1).astype(BF)

    e2 = np.zeros((LANE, 2 * MLA_SLOTS), np.float32)
    for h in range(MLA_HEADS):
        for j in range(MLA_ROPE):
            e2[j, h * HEAD_SLOT + MLA_NOPE + j] = 1.0
            e2[MLA_ROPE + j, MLA_SLOTS + h * HEAD_SLOT + MLA_NOPE + j] = 1.0
    out["kpe_place"] = jnp.asarray(e2, BF)

    gk_src = _head_slots(GLA_HEADS, GLA_DK, GLA_DK)
    wgk = jnp.concatenate([_take_cols(w["gla_w_gk_fwd"], gk_src), _take_cols(w["gla_w_gk_bwd"], gk_src)], axis=-1)
    out["w_gk"] = _pad_rows(wgk, LANE).astype(BF)
    out["b_gk"] = jnp.concatenate([_take_cols(w["gla_b_gk_fwd"], gk_src), _take_cols(w["gla_b_gk_bwd"], gk_src)],
                                  axis=-1)[:, None, :]
    gv_src = _head_slots(GLA_HEADS, GLA_DV, GLA_DV)
    out["gla_gain"] = _take_cols(w["gla_norm_w"], np.where(gv_src >= 0, gv_src % GLA_DV, -1))[:, None, :]

    row_src = np.concatenate([np.arange(F_WIDTH + MLA_WIDTH), np.where(gv_src >= 0, F_WIDTH + MLA_WIDTH + gv_src, -1)])
    out["w_out"] = _take_cols(w["w_out"], row_src, axis=-2).astype(BF)

    nd = w["ffn_w_12"].shape[0]
    w12 = w["ffn_w_12"].astype(BF).reshape(nd, D_MODEL, 2, FF_NCHUNK, FF_CHUNK)
    out["w12"] = jnp.transpose(w12, (0, 3, 1, 2, 4)).reshape(nd, FF_NCHUNK, D_MODEL, 2 * FF_CHUNK)
    cw = w["ffn_conv_w"].reshape(nd, 3, 2, FF_NCHUNK, FF_CHUNK)
    cb = w["ffn_conv_b"].reshape(nd, 1, 2, FF_NCHUNK, FF_CHUNK)
    cwb = jnp.concatenate([cw, cb], axis=1)
    out["conv"] = jnp.transpose(cwb, (0, 3, 1, 2, 4)).reshape(nd, FF_NCHUNK, 4, 2 * FF_CHUNK).astype(F32)
    out["w_down"] = w["ffn_w_down"].reshape(nd, FF_NCHUNK, FF_CHUNK, D_MODEL).astype(BF)
    out["norm1_w"] = f32(w["norm1_w"])[:, None, :]
    out["norm2_w"] = f32(w["norm2_w"])[:, None, :]
    return out


def _rope_tables(n_tokens):
    q = MLA_ROPE // 4
    t = np.arange(n_tokens)
    row = (t // GRID_W).astype(np.float32)
    col = (t % GRID_W).astype(np.float32)
    axis_dims = MLA_ROPE // 2
    inv_freq = np.power(np.float32(ROPE_BASE), -np.arange(0, axis_dims, 2, dtype=np.float32) / axis_dims)
    ang_r = row[:, None] * inv_freq
    ang_c = col[:, None] * inv_freq
    cos = np.ones((n_tokens, HEAD_SLOT), np.float32)
    sin = np.zeros((n_tokens, HEAD_SLOT), np.float32)
    for k, ang in enumerate((ang_r, ang_r, ang_c, ang_c)):
        lo = MLA_NOPE + k * q
        cos[:, lo:lo + q] = np.cos(ang)
        sin[:, lo:lo + q] = np.sin(ang)
    return jnp.asarray(cos), jnp.asarray(sin)


def _fft_tables(n1, n2):
    k = np.arange(n1)
    ang1 = 2.0 * np.pi * np.outer(k, k) / n1
    fr, fi = np.cos(ang1), -np.sin(ang1)
    w1 = np.block([[fr, fi], [fi, -fr]])
    length = n1 * n2
    kk = (np.arange(n1)[:, None] + n1 * np.arange(n2)[None, :]).astype(np.float64)
    t2 = np.arange(n2, dtype=np.float64)
    ang2 = 2.0 * np.pi * kk[:, :, None] * t2[None, None, :] / length
    norm = 1.0 / math.sqrt(length * F_DIM)
    tab = np.concatenate([np.cos(ang2), np.sin(ang2)], axis=-1) * norm
    return jnp.asarray(w1, F32).astype(BF), jnp.asarray(tab, F32).astype(BF)


def _ctx_dft_table(n):
    k = np.arange(n)
    ang = 2.0 * np.pi * np.outer(k, k) / n
    norm = 1.0 / math.sqrt(n * F_DIM)
    return jnp.asarray(np.concatenate([np.cos(ang), -np.sin(ang)], axis=1) * norm, F32).astype(BF)


def _mod_kernel(c_ref, w_ref, b_ref, o_ref):
    c = c_ref[...]
    s = c * (1.0 / (1.0 + jnp.exp(-c)))
    o_ref[...] = _dot(s.astype(BF), w_ref[...].astype(BF)) + b_ref[...]


def _adaln(cond8, mod_w, mod_b):
    nd, d, n = mod_w.shape
    tn = 1536
    return pl.pallas_call(
        _mod_kernel,
        out_shape=jax.ShapeDtypeStruct((nd, 8, n), F32),
        grid=(nd, n // tn),
        in_specs=[pl.BlockSpec((8, d), lambda l, j: (0, 0)),
                  pl.BlockSpec((None, d, tn), lambda l, j: (l, 0, j)),
                  pl.BlockSpec((None, 1, tn), lambda l, j: (l, 0, j))],
        out_specs=pl.BlockSpec((None, 8, tn), lambda l, j: (l, 0, j)),
        compiler_params=_cparams("arbitrary", "arbitrary"),
        name="adaln",
    )(cond8, mod_w, mod_b.reshape(nd, 1, n))


def _rms(x, n):
    return lax.rsqrt(jnp.sum(x * x, axis=-1, keepdims=True) * (1.0 / n) + EPS)


def _in_kernel(x_ref, sh_ref, sc_ref, n1_ref, win_ref, dft_ref, qlw_ref, wq_ref, kvlw_ref, wkv_ref, place_ref,
               qg_ref, qgp_ref, kg_ref, kgp_ref, cos_ref, sin_ref, wgk_ref, bgk_ref,
               fa_ref, fb_ref, q_ref, k_ref, v_ref, gq_ref, gk_ref, gv_ref, gg_ref, df_ref, db_ref,
               *, row, rope):
    for r0 in range(0, x_ref.shape[0], IN_SUB):
        rs = slice(r0, min(r0 + IN_SUB, x_ref.shape[0]))
        x = x_ref[rs, :]
        xn = x * _rms(x, D_MODEL) * n1_ref[...]
        h = xn * (1.0 + sc_ref[row:row + 1, :]) + sh_ref[row:row + 1, :]
        hb = h.astype(BF)
        p = _dot(hb, win_ref[:, :COL_GQ])

        ab = _dot(p[:, COL_F:COL_F + F_WIDTH].astype(BF), dft_ref[...])
        fa_ref[rs, :] = ab[:, :F_WIDTH].astype(BF)
        fb_ref[rs, :] = ab[:, F_WIDTH:].astype(BF)

        cq = p[:, COL_CQ:COL_CQ + 256]
        cqn = cq * _rms(cq, MLA_Q_LORA) * qlw_ref[...]
        qq = _dot(cqn.astype(BF), wq_ref[...])
        ckv = p[:, COL_CKV:COL_CKV + MLA_KV_LORA]
        ckvn = ckv * _rms(ckv, MLA_KV_LORA) * kvlw_ref[...]
        kk = _dot(ckvn.astype(BF), wkv_ref[...])
        kp = _dot(p[:, COL_KPE:COL_KPE + LANE].astype(BF), place_ref[...])
        pg = _dot(hb, win_ref[:, COL_GQ:])
        v_lane = lax.broadcasted_iota(jnp.int32, (x.shape[0], HEAD_SLOT), 1)

        q_scale = (MLA_QK ** -0.5) * LOG2E
        for hd in range(MLA_HEADS):
            sl = slice(hd * HEAD_SLOT, (hd + 1) * HEAD_SLOT)
            sp = slice(MLA_SLOTS + hd * HEAD_SLOT, MLA_SLOTS + (hd + 1) * HEAD_SLOT)
            qh = qq[:, sl]
            kh = kk[:, sl] + kp[:, sl]
            rq = _rms(qh, MLA_QK) * q_scale
            rk = _rms(kh, MLA_QK)
            if rope:
                cos = cos_ref[rs, :]
                sin = sin_ref[rs, :]
                qo = rq * (qh * (qg_ref[:, sl] * cos) + qq[:, sp] * (qgp_ref[:, sl] * sin))
                ko = rk * (kh * (kg_ref[:, sl] * cos) + kp[:, sp] * (kgp_ref[:, sl] * sin))
            else:
                qo = rq * (qh * qg_ref[:, sl])
                ko = rk * (kh * kg_ref[:, sl])
            q_ref[sl, rs] = qo.T.astype(BF)
            k_ref[rs, sl] = ko.astype(BF)
            vh = jnp.where(v_lane == MLA_V, 1.0, kk[:, MLA_SLOTS + hd * HEAD_SLOT:MLA_SLOTS + (hd + 1) * HEAD_SLOT])
            v_ref[hd * V_ROWS:(hd + 1) * V_ROWS, rs] = vh.T[:V_ROWS, :].astype(BF)

        gq_ref[rs, :] = (pg[:, 0:GLA_SLOTS] * (GLA_DK ** -0.5)).astype(BF)
        gk_ref[rs, :] = pg[:, COL_GK - COL_GQ:COL_GK - COL_GQ + GLA_SLOTS].astype(BF)
        gv_ref[rs, :] = pg[:, COL_GV - COL_GQ:COL_GV - COL_GQ + GLA_SLOTS].astype(BF)
        gg_ref[rs, :] = pg[:, COL_GG - COL_GQ:COL_GG - COL_GQ + GLA_SLOTS].astype(BF)
        z = _dot(pg[:, COL_LR - COL_GQ:COL_LR - COL_GQ + LANE].astype(BF), wgk_ref[...]) + bgk_ref[...]
        ls = (jnp.minimum(z, 0.0) - jnp.log1p(jnp.exp(-jnp.abs(z)))) * (1.0 / GLA_GATE_NORM)
        df_ref[rs, :] = ls[:, :GLA_SLOTS]
        db_ref[rs, :] = ls[:, GLA_SLOTS:]


def _in_proj(x2, mods, pw, cos_t, sin_t, l, row, rope):
    rows = x2.shape[0]
    tm = min(2 * IN_SUB, rows)
    rt = lambda n: pl.BlockSpec((tm, n), lambda i: (i, 0))
    modspec = lambda k: pl.BlockSpec((None, 8, D_MODEL), lambda i: (l, 0, k))
    outs = [(F_WIDTH, BF), (F_WIDTH, BF), None, (MLA_SLOTS, BF), None,
            (GLA_SLOTS, BF), (GLA_SLOTS, BF), (GLA_SLOTS, BF), (GLA_SLOTS, BF), (GLA_SLOTS, F32), (GLA_SLOTS, F32)]
    shapes = [jax.ShapeDtypeStruct((rows, o[0]), o[1]) if o else None for o in outs]
    specs = [rt(o[0]) if o else None for o in outs]
    shapes[2] = jax.ShapeDtypeStruct((MLA_SLOTS, rows), BF)
    specs[2] = pl.BlockSpec((MLA_SLOTS, tm), lambda i: (0, i))
    shapes[4] = jax.ShapeDtypeStruct((rows // tm, MLA_HEADS * V_ROWS, tm), BF)
    specs[4] = pl.BlockSpec((None, MLA_HEADS * V_ROWS, tm), lambda i: (i, 0, 0))
    return pl.pallas_call(
        functools.partial(_in_kernel, row=row, rope=rope),
        out_shape=shapes,
        grid=(rows // tm,),
        in_specs=[rt(D_MODEL), modspec(0), modspec(1), _layer((1, D_MODEL), l),
                  _layer((D_MODEL, IN_PAD), l), _full((F_WIDTH, 2 * F_WIDTH)),
                  _layer((1, 256), l), _layer((256, 2 * MLA_SLOTS), l),
                  _layer((1, MLA_KV_LORA), l), _layer((MLA_KV_LORA, 2 * MLA_SLOTS), l),
                  _full((LANE, 2 * MLA_SLOTS)),
                  _layer((1, MLA_SLOTS), l), _layer((1, MLA_SLOTS), l), _layer((1, MLA_SLOTS), l),
                  _layer((1, MLA_SLOTS), l),
                  rt(HEAD_SLOT), rt(HEAD_SLOT),
                  _layer((LANE, 2 * GLA_SLOTS), l), _layer((1, 2 * GLA_SLOTS), l)],
        out_specs=specs,
        compiler_params=_cparams("arbitrary"),
        name="in_proj",
    )(x2, mods, mods, pw["norm1_w"], pw["w_in"], pw["dft64"], pw["q_lora_w"], pw["w_uq"], pw["kv_lora_w"],
      pw["w_ukv"], pw["kpe_place"], pw["q_gain"], pw["q_gain_p"], pw["k_gain"], pw["k_gain_p"],
      cos_t, sin_t, pw["w_gk"], pw["b_gk"])


def _fft1_kernel(a_ref, b_ref, w_ref, g_ref):
    n1 = a_ref.shape[0]
    ab = jnp.concatenate([a_ref[...], b_ref[...]], axis=0)
    g = _dot(w_ref[...], ab)
    g_ref[0] = g[:n1].astype(BF)
    g_ref[1] = g[n1:].astype(BF)


def _fft2_kernel(g_ref, t_ref, o_ref, *, batch):
    for j in range(batch):
        g = jnp.concatenate([g_ref[0, j], g_ref[1, j]], axis=0)
        o_ref[:, j * F_WIDTH:(j + 1) * F_WIDTH] = _dot(t_ref[j], g).astype(BF)


def _fourier_latent(fa, fb, w1, tab):
    length = fa.shape[0]
    n1, n2 = tab.shape[0], tab.shape[1]
    cols = n2 * F_WIDTH
    tn = min(4096, cols)
    g = pl.pallas_call(
        _fft1_kernel,
        out_shape=jax.ShapeDtypeStruct((2, n1, cols), BF),
        grid=(cols // tn,),
        in_specs=[pl.BlockSpec((n1, tn), lambda j: (0, j)), pl.BlockSpec((n1, tn), lambda j: (0, j)),
                  _full((2 * n1, 2 * n1))],
        out_specs=pl.BlockSpec((2, n1, tn), lambda j: (0, 0, j)),
        compiler_params=_cparams("arbitrary"),
        name="fft_stage1",
    )(fa.reshape(n1, cols), fb.reshape(n1, cols), w1)
    batch = 8
    y = pl.pallas_call(
        functools.partial(_fft2_kernel, batch=batch),
        out_shape=jax.ShapeDtypeStruct((n2, n1 * F_WIDTH), BF),
        grid=(n1 // batch,),
        in_specs=[pl.BlockSpec((2, batch, n2, F_WIDTH), lambda i: (0, i, 0, 0)),
                  pl.BlockSpec((batch, n2, 2 * n2), lambda i: (i, 0, 0))],
        out_specs=pl.BlockSpec((n2, batch * F_WIDTH), lambda i: (0, i)),
        compiler_params=_cparams("arbitrary"),
        name="fft_stage2",
    )(g.reshape(2, n1, n2, F_WIDTH), tab)
    return y.reshape(length, F_WIDTH)


def _fctx_kernel(a_ref, b_ref, t_ref, o_ref):
    ab = jnp.concatenate([a_ref[...], b_ref[...]], axis=0)
    o_ref[...] = _dot(t_ref[...], ab).astype(BF)


def _fourier_ctx(fa, fb, tab):
    n = fa.shape[0]
    return pl.pallas_call(
        _fctx_kernel,
        out_shape=jax.ShapeDtypeStruct((n, F_WIDTH), BF),
        grid=(1,),
        in_specs=[_full((n, F_WIDTH)), _full((n, F_WIDTH)), _full((n, 2 * n))],
        out_specs=_full((n, F_WIDTH)),
        compiler_params=_cparams("arbitrary"),
        name="fft_ctx",
    )(fa, fb, tab)


def _attn_kernel(*refs, tk, has_x):
    if has_x:
        q_ref, kc_ref, vc_ref, kx_ref, vx_ref, o_ref, acc_ref, s_ref = refs
    else:
        q_ref, kc_ref, vc_ref, o_ref, acc_ref = refs
    tq = q_ref.shape[1]
    heads = range(2)
    slot = lambda h: slice(h * HEAD_SLOT, (h + 1) * HEAD_SLOT)
    vrow = lambda h: slice(h * V_ROWS, (h + 1) * V_ROWS)
    qt = [q_ref[slot(h), :] for h in heads]

    def scores(k2):
        return tuple(_dot(k2[:, slot(h)], qt[h]) for h in heads)

    def absorb(s, vts, m):
        sub = vts[0].shape[1]
        m_new = [jnp.maximum(m[h], jnp.max(s[h], axis=0, keepdims=True)) for h in heads]
        p = [jnp.exp2(s[h] - m_new[h]).astype(BF) for h in heads]
        for h in heads:
            alpha = jnp.exp2(m[h] - m_new[h])
            pv = _dot(vts[0][vrow(h), :], p[h][0:sub, :])
            for t in range(1, len(vts)):
                pv += _dot(vts[t][vrow(h), :], p[h][t * sub:(t + 1) * sub, :])
            acc_ref[h] = alpha * acc_ref[h] + pv
        return tuple(m_new)

    acc_ref[...] = jnp.zeros_like(acc_ref)
    m = tuple(jnp.full((1, tq), -1e30, F32) for _ in heads)
    m = absorb(scores(kc_ref[...]), [vc_ref[t] for t in range(vc_ref.shape[0])], m)
    if has_x:
        n = kx_ref.shape[0] // tk
        per = tk // vx_ref.shape[2]
        assert n % 2 == 0

        def put_scores(buf, j):
            off = pl.multiple_of(j * tk, tk)
            s = scores(kx_ref[pl.ds(off, tk), :])
            for h in heads:
                s_ref[buf, h] = s[h]

        def take(buf, j, m):
            return absorb(tuple(s_ref[buf, h] for h in heads), [vx_ref[j * per + t] for t in range(per)], m)

        def body(i, m):
            put_scores(1, 2 * i + 1)
            m = take(0, 2 * i, m)
            put_scores(0, jnp.minimum(2 * i + 2, n - 1))
            return take(1, 2 * i + 1, m)

        put_scores(0, 0)
        m = lax.fori_loop(0, n // 2, body, m)
    for h in heads:
        acc = acc_ref[h]
        o_ref[h * MLA_V:(h + 1) * MLA_V, :] = (acc[:MLA_V, :] / acc[MLA_V:MLA_V + 1, :]).astype(BF)


def _attention(qt, kc, vct, kx=None, vxt=None):
    rows = qt.shape[1]
    lc = kc.shape[0]
    tq = min(512, rows)
    has_x = kx is not None
    pair = 2 * HEAD_SLOT
    vpair = 2 * V_ROWS
    in_specs = [pl.BlockSpec((pair, tq), lambda p, i: (p, i)),
                pl.BlockSpec((lc, pair), lambda p, i: (0, p)),
                pl.BlockSpec((vct.shape[0], vpair, vct.shape[2]), lambda p, i: (0, p, 0))]
    args = [qt, kc, vct]
    tk = 512
    scratch = [pltpu.VMEM((2, V_ROWS, tq), F32)]
    if has_x:
        lx = kx.shape[0]
        tk = min(tk, lx)
        in_specs += [pl.BlockSpec((lx, pair), lambda p, i: (0, p)),
                     pl.BlockSpec((vxt.shape[0], vpair, vxt.shape[2]), lambda p, i: (0, p, 0))]
        args += [kx, vxt]
        scratch.append(pltpu.VMEM((2, 2, tk, tq), F32))
    return pl.pallas_call(
        functools.partial(_attn_kernel, tk=tk, has_x=has_x),
        out_shape=jax.ShapeDtypeStruct((MLA_WIDTH, rows), BF),
        grid=(MLA_HEADS // 2, rows // tq),
        in_specs=in_specs,
        out_specs=pl.BlockSpec((2 * MLA_V, tq), lambda p, i: (p, i)),
        scratch_shapes=scratch,
        compiler_params=_cparams("arbitrary", "arbitrary"),
        name="attention",
    )(*args)


def _gla_kernel(qf_ref, kf_ref, vf_ref, df_ref, qb_ref, kb_ref, vb_ref, db_ref, s0_ref,
                of_ref, ob_ref, sfin_ref, st_ref):
    i = pl.program_id(0)
    t = qf_ref.shape[0]
    c = GLA_CHUNK
    nc = t // c
    assert c & (c - 1) == 0

    @pl.when(i == 0)
    def _():
        st_ref[...] = s0_ref[...]

    r = lax.broadcasted_iota(jnp.int32, (t, t), 0)
    s = lax.broadcasted_iota(jnp.int32, (t, t), 1)
    same_chunk = jnp.bitwise_xor(r, s) < c
    dirs = ((qf_ref, kf_ref, vf_ref, df_ref, of_ref, same_chunk & (s <= r), range(nc), c - 1),
            (qb_ref, kb_ref, vb_ref, db_ref, ob_ref, same_chunk & (s >= r), range(nc - 1, -1, -1), 0))
    heads = range(GLA_HEADS)
    slot = lambda h: slice(h * HEAD_SLOT, (h + 1) * HEAD_SLOT)
    chunk = lambda ch: slice(ch * c, (ch + 1) * c)

    b = []
    for q_ref, k_ref, v_ref, g_ref, o_ref, mask, order, last in dirs:
        tri = mask.astype(BF)
        g = g_ref[...]
        g_hi = g.astype(BF)
        g_r = g - g_hi.astype(F32)
        g_mid = g_r.astype(BF)
        g_lo = (g_r - g_mid.astype(F32)).astype(BF)
        b.append(_dot(tri, g_hi) + _dot(tri, g_mid) + _dot(tri, g_lo))

    work = []
    for d, (q_ref, k_ref, v_ref, g_ref, o_ref, mask, order, last) in enumerate(dirs):
        tot = [b[d][ch * c + last:ch * c + last + 1, :] for ch in range(nc)]
        b_tot = jnp.concatenate([jnp.broadcast_to(tot[ch], (c, GLA_SLOTS)) for ch in range(nc)], axis=0)
        q_in = (q_ref[...].astype(F32) * jnp.exp(b[d])).astype(BF)
        kf = k_ref[...].astype(F32)
        k_in = (kf * jnp.exp(-b[d])).astype(BF)
        k_out = (kf * jnp.exp(b_tot - b[d])).astype(BF)
        v = v_ref[...]
        a = [_dot_nt(q_in[:, slot(h)], k_in[:, slot(h)]) for h in heads]
        inc = [{ch: _dot_tn(v[chunk(ch), slot(h)], k_out[chunk(ch), slot(h)]) for ch in order} for h in heads]
        work.append((tot, q_in, v, a, inc))

    o_intra = []
    for d, (q_ref, k_ref, v_ref, g_ref, o_ref, mask, order, last) in enumerate(dirs):
        tot, q_in, v, a, inc = work[d]
        o_intra.append([_dot(jnp.where(mask, a[h], 0.0).astype(BF), v[:, slot(h)]) for h in heads])

    for d, (q_ref, k_ref, v_ref, g_ref, o_ref, mask, order, last) in enumerate(dirs):
        tot, q_in, v, a, inc = work[d]
        entering = []
        for h in heads:
            st = st_ref[d, h]
            ent = {}
            for ch in order:
                ent[ch] = st.astype(BF)
                st = st * jnp.exp(tot[ch][:, slot(h)]) + inc[h][ch]
            st_ref[d, h] = st
            entering.append(ent)
        for h in heads:
            for ch in order:
                o_ref[chunk(ch), slot(h)] = (o_intra[d][h][chunk(ch), :]
                                             + _dot_nt(q_in[chunk(ch), slot(h)], entering[h][ch]))

    @pl.when(i == pl.num_programs(0) - 1)
    def _():
        sfin_ref[...] = st_ref[...]


def _gla(gq, gk, gv, df, db, s0):
    rows = gq.shape[0]
    c = min(GLA_STEP, rows)
    n = rows // c
    fwd = pl.BlockSpec((c, GLA_SLOTS), lambda i: (i, 0))
    bwd = pl.BlockSpec((c, GLA_SLOTS), lambda i: (n - 1 - i, 0))
    st_shape = (2, GLA_HEADS, HEAD_SLOT, HEAD_SLOT)
    return pl.pallas_call(
        _gla_kernel,
        out_shape=[jax.ShapeDtypeStruct((rows, GLA_SLOTS), F32), jax.ShapeDtypeStruct((rows, GLA_SLOTS), F32),
                   jax.ShapeDtypeStruct(st_shape, F32)],
        grid=(n,),
        in_specs=[fwd, fwd, fwd, fwd, bwd, bwd, bwd, bwd, _full(st_shape)],
        out_specs=[fwd, bwd, _full(st_shape)],
        scratch_shapes=[pltpu.VMEM(st_shape, F32)],
        compiler_params=_cparams("arbitrary"),
        name="gla_scan",
    )(gq, gk, gv, df, gq, gk, gv, db, s0)


def _out_kernel(x_ref, four_ref, att_ref, of_ref, ob_ref, gg_ref, gain_ref, wout_ref, g1_ref, n2_ref,
                sh_ref, sc_ref, xo_ref, h_ref, *, row):
    o = of_ref[...] + ob_ref[...]
    g = gg_ref[...].astype(F32)
    gate = g * (1.0 / (1.0 + jnp.exp(-g)))
    y = _dot(four_ref[...], wout_ref[0:F_WIDTH, :])
    y += _dot_tn(att_ref[...], wout_ref[F_WIDTH:F_WIDTH + MLA_WIDTH, :])
    for h in range(GLA_HEADS):
        sl = slice(h * HEAD_SLOT, (h + 1) * HEAD_SLOT)
        oh = o[:, sl]
        lin = oh * _rms(oh, GLA_DV) * gain_ref[:, sl] * gate[:, sl]
        lo = F_WIDTH + MLA_WIDTH + h * HEAD_SLOT
        y += _dot(lin.astype(BF), wout_ref[lo:lo + HEAD_SLOT, :])
    x = x_ref[...] + g1_ref[row:row + 1, :] * y
    xo_ref[...] = x
    hn = x * _rms(x, D_MODEL) * n2_ref[...]
    h_ref[...] = (hn * (1.0 + sc_ref[row:row + 1, :]) + sh_ref[row:row + 1, :]).astype(BF)


def _out_proj(x2, four, att, o_f, o_b, gg, mods, pw, l, row):
    rows = x2.shape[0]
    tm = min(ROW_TILE, rows)
    rt = lambda n: pl.BlockSpec((tm, n), lambda i: (i, 0))
    modspec = lambda k: pl.BlockSpec((None, 8, D_MODEL), lambda i: (l, 0, k))
    return pl.pallas_call(
        functools.partial(_out_kernel, row=row),
        out_shape=[jax.ShapeDtypeStruct((rows, D_MODEL), F32), jax.ShapeDtypeStruct((rows, D_MODEL), BF)],
        grid=(rows // tm,),
        in_specs=[rt(D_MODEL), rt(F_WIDTH), pl.BlockSpec((MLA_WIDTH, tm), lambda i: (0, i)),
                  rt(GLA_SLOTS), rt(GLA_SLOTS), rt(GLA_SLOTS),
                  _layer((1, GLA_SLOTS), l), _layer((MIX_PAD, D_MODEL), l), modspec(2),
                  _layer((1, D_MODEL), l), modspec(3), modspec(4)],
        out_specs=[rt(D_MODEL), rt(D_MODEL)],
        compiler_params=_cparams("arbitrary"),
        name="out_proj",
    )(x2, four, att, o_f, o_b, gg, pw["gla_gain"], pw["w_out"], mods, pw["norm2_w"], mods, mods)


def _ffn_kernel(x_ref, h_ref, hp_ref, hn_ref, w12_ref, conv_ref, wd_ref, g2_ref, o_ref, acc_ref, hx_ref, u_ref,
                *, row):
    i = pl.program_id(0)
    tm = h_ref.shape[0]
    m = tm + 2 * FF_HALO
    keep_prev = jnp.where(i > 0, 1.0, 0.0)
    keep_next = jnp.where(i < pl.num_programs(0) - 1, 1.0, 0.0)
    hx_ref[0:FF_HALO, :] = (hp_ref[...].astype(F32) * keep_prev).astype(BF)
    hx_ref[FF_HALO:FF_HALO + tm, :] = h_ref[...]
    hx_ref[FF_HALO + tm:m, :] = (hn_ref[...].astype(F32) * keep_next).astype(BF)
    acc_ref[...] = jnp.zeros_like(acc_ref)

    def up(buf, c):
        u_ref[buf] = _dot(hx_ref[...], w12_ref[c])

    def down(buf, c):
        u = u_ref[buf]
        cw = conv_ref[c]
        u_prev = pltpu.roll(u, 1, axis=0)
        u_next = pltpu.roll(u, m - 1, axis=0)
        uc = u_prev * cw[0:1, :] + u * cw[1:2, :] + u_next * cw[2:3, :] + cw[3:4, :]
        uc = uc[FF_HALO:FF_HALO + tm, :]
        a = uc[:, :FF_CHUNK]
        act = a * (1.0 / (1.0 + jnp.exp(-a))) * uc[:, FF_CHUNK:]
        acc_ref[...] += _dot(act.astype(BF), wd_ref[c])

    def body(j, carry):
        up(1, 2 * j + 1)
        down(0, 2 * j)
        up(0, 2 * j + 2)
        down(1, 2 * j + 1)
        return carry

    assert FF_NCHUNK % 2 == 1
    up(0, 0)
    lax.fori_loop(0, FF_NCHUNK // 2, body, 0)
    down(0, FF_NCHUNK - 1)
    o_ref[...] = x_ref[...] + g2_ref[row:row + 1, :] * acc_ref[...]


def _ffn(x2, h2, mods, pw, l, row):
    rows = x2.shape[0]
    tm = min(512, rows)
    nt = rows // tm
    per = tm // FF_HALO
    last_blk = rows // FF_HALO - 1
    m = tm + 2 * FF_HALO
    rt = lambda n: pl.BlockSpec((tm, n), lambda i: (i, 0))
    once = lambda shape: pl.BlockSpec((None,) + shape, lambda i: (l,) + (0,) * len(shape),
                                      pipeline_mode=pl.Buffered(1))
    return pl.pallas_call(
        functools.partial(_ffn_kernel, row=row),
        out_shape=jax.ShapeDtypeStruct((rows, D_MODEL), F32),
        grid=(nt,),
        in_specs=[rt(D_MODEL), rt(D_MODEL),
                  pl.BlockSpec((FF_HALO, D_MODEL), lambda i: (jnp.maximum(i * per - 1, 0), 0)),
                  pl.BlockSpec((FF_HALO, D_MODEL), lambda i: (jnp.minimum((i + 1) * per, last_blk), 0)),
                  once((FF_NCHUNK, D_MODEL, 2 * FF_CHUNK)), once((FF_NCHUNK, 4, 2 * FF_CHUNK)),
                  once((FF_NCHUNK, FF_CHUNK, D_MODEL)),
                  pl.BlockSpec((None, 8, D_MODEL), lambda i: (l, 0, 5))],
        out_specs=rt(D_MODEL),
        scratch_shapes=[pltpu.VMEM((tm, D_MODEL), F32), pltpu.VMEM((m, D_MODEL), BF),
                        pltpu.VMEM((2, m, 2 * FF_CHUNK), F32)],
        compiler_params=_cparams("arbitrary"),
        name="conv_ffn",
    )(x2, h2, h2, h2, pw["w12"], pw["conv"], pw["w_down"], mods)


def kernel(x, c, ctx, c_ctx, mod_w, mod_b, norm1_w, norm2_w, w_in, mla_q_lora_norm_w, mla_w_uq, mla_kv_lora_norm_w, mla_w_ukv, mla_q_norm_w, mla_k_norm_w, gla_w_gk_fwd, gla_b_gk_fwd, gla_w_gk_bwd, gla_b_gk_bwd, gla_norm_w, w_out, ffn_w_12, ffn_conv_w, ffn_conv_b, ffn_w_down):
    batch, seq, d = x.shape
    assert batch == 1 and d == D_MODEL
    lc = ctx.shape[1]
    depth = mod_w.shape[0]
    n1 = int(round(math.sqrt(seq)))
    assert n1 * n1 == seq and seq % ROW_TILE == 0 and lc % GLA_CHUNK == 0

    pw = _prep_weights(dict(
        w_in=w_in, mla_w_uq=mla_w_uq, mla_q_lora_norm_w=mla_q_lora_norm_w, mla_kv_lora_norm_w=mla_kv_lora_norm_w,
        mla_w_ukv=mla_w_ukv, mla_q_norm_w=mla_q_norm_w, mla_k_norm_w=mla_k_norm_w, gla_w_gk_fwd=gla_w_gk_fwd,
        gla_b_gk_fwd=gla_b_gk_fwd, gla_w_gk_bwd=gla_w_gk_bwd, gla_b_gk_bwd=gla_b_gk_bwd, gla_norm_w=gla_norm_w,
        w_out=w_out, ffn_w_12=ffn_w_12, ffn_conv_w=ffn_conv_w, ffn_conv_b=ffn_conv_b, ffn_w_down=ffn_w_down,
        norm1_w=norm1_w, norm2_w=norm2_w))
    cos_t, sin_t = _rope_tables(seq)
    ones_c = jnp.ones((lc, HEAD_SLOT), F32)
    w1, tab = _fft_tables(n1, n1)
    tab_c = _ctx_dft_table(lc)

    cond8 = jnp.zeros((8, d), F32).at[0].set(c[0].astype(F32)).at[1].set(c_ctx.astype(F32))
    mods = _adaln(cond8, mod_w, mod_b)

    xs = x[0].astype(F32)
    xc = ctx[0].astype(F32)
    s_zero = jnp.zeros((2, GLA_HEADS, HEAD_SLOT, HEAD_SLOT), F32)
    for l in range(depth):
        last = l == depth - 1
        fa_c, fb_c, q_c, k_c, v_c, gq_c, gk_c, gv_c, gg_c, df_c, db_c = _in_proj(xc, mods, pw, ones_c, ones_c, l, 1, False)
        fa_x, fb_x, q_x, k_x, v_x, gq_x, gk_x, gv_x, gg_x, df_x, db_x = _in_proj(xs, mods, pw, cos_t, sin_t, l, 0, True)

        of_c, ob_c, s_c = _gla(gq_c, gk_c, gv_c, df_c, db_c, s_zero)
        of_x, ob_x, _ = _gla(gq_x, gk_x, gv_x, df_x, db_x, s_c)
        four_x = _fourier_latent(fa_x, fb_x, w1, tab)
        att_x = _attention(q_x, k_c, v_c, k_x, v_x)
        x_mid, h2 = _out_proj(xs, four_x, att_x, of_x, ob_x, gg_x, mods, pw, l, 0)
        xs = _ffn(x_mid, h2, mods, pw, l, 0)
        if not last:
            four_c = _fourier_ctx(fa_c, fb_c, tab_c)
            att_c = _attention(q_c, k_c, v_c)
            c_mid, hc2 = _out_proj(xc, four_c, att_c, of_c, ob_c, gg_c, mods, pw, l, 1)
            xc = _ffn(c_mid, hc2, mods, pw, l, 1)
    return xs[None].astype(x.dtype)
```

```python
import functools
import math

import numpy as np
import jax
import jax.numpy as jnp
from jax import lax
from jax.experimental import pallas as pl
from jax.experimental.pallas import tpu as pltpu

D_MODEL = 1024
DEPTH = 2
GRID_W = 64
EPS = 1e-6

F_GROUPS = 4
F_DIM = 64
F_WIDTH = F_GROUPS * F_DIM

MLA_HEADS = 6
MLA_Q_LORA = 192
MLA_KV_LORA = 128
MLA_NOPE = 64
MLA_ROPE = 32
MLA_V = 64
MLA_QK = MLA_NOPE + MLA_ROPE
MLA_WIDTH = MLA_HEADS * MLA_V
MLA_IN = MLA_Q_LORA + MLA_KV_LORA + MLA_ROPE
ROPE_BASE = 10000.0

GLA_HEADS = 4
GLA_DK = 48
GLA_DV = 96
GLA_GATE_RANK = 16
GLA_GATE_NORM = 16.0
GLA_CHUNK = 64
GLA_STEP = 256
GLA_WIDTH = GLA_HEADS * GLA_DV
GLA_QK_W = GLA_HEADS * GLA_DK

MIX_WIDTH = F_WIDTH + MLA_WIDTH + GLA_WIDTH
D_FF = 2816
N_MOD = 6

LANE = 128
SUBLANE_BF16 = 16
VMEM_LIMIT = 48 * 1024 * 1024

HEAD_SLOT = LANE
V_ROWS = 80
MLA_SLOTS = MLA_HEADS * HEAD_SLOT
GLA_SLOTS = GLA_HEADS * HEAD_SLOT

COL_F = 0
COL_CQ = COL_F + F_WIDTH
COL_CKV = COL_CQ + 256
COL_KPE = COL_CKV + MLA_KV_LORA
COL_GQ = COL_KPE + LANE
COL_GK = COL_GQ + GLA_SLOTS
COL_GV = COL_GK + GLA_SLOTS
COL_GG = COL_GV + GLA_SLOTS
COL_LR = COL_GG + GLA_SLOTS
IN_PAD = COL_LR + LANE

MIX_PAD = F_WIDTH + MLA_WIDTH + GLA_SLOTS

FF_CHUNK = 256
FF_NCHUNK = D_FF // FF_CHUNK
FF_HALO = SUBLANE_BF16

LOG2E = 1.4426950408889634

ROW_TILE = 256
IN_SUB = 256
BF = jnp.bfloat16
F32 = jnp.float32


def _cparams(*sem):
    return pltpu.CompilerParams(dimension_semantics=sem, vmem_limit_bytes=VMEM_LIMIT)


def _dot(a, b):
    return jnp.dot(a, b, preferred_element_type=F32)


def _dot_nt(a, b):
    return lax.dot_general(a, b, (((1,), (1,)), ((), ())), preferred_element_type=F32)


def _dot_tn(a, b):
    return lax.dot_general(a, b, (((0,), (0,)), ((), ())), preferred_element_type=F32)


def _full(shape):
    n = len(shape)
    return pl.BlockSpec(shape, lambda *_: (0,) * n)


def _layer(shape, l):
    n = len(shape)
    return pl.BlockSpec((None,) + tuple(shape), lambda *_: (l,) + (0,) * n)


def _rot_partner(n_rope):
    q = n_rope // 4
    src = np.zeros(n_rope, np.int64)
    sgn = np.zeros(n_rope, np.float32)
    for base in (0, 2 * q):
        for j in range(q):
            src[base + j] = base + q + j
            sgn[base + j] = -1.0
            src[base + q + j] = base + j
            sgn[base + q + j] = 1.0
    return src, sgn


def _take_cols(w, src, sgn=None, axis=-1):
    src = np.asarray(src)
    sgn = np.ones(len(src), np.float32) if sgn is None else np.asarray(sgn, np.float32)
    axis = axis % w.ndim
    pieces = []
    lo = 0
    while lo < len(src):
        hi = lo + 1
        if src[lo] < 0:
            while hi < len(src) and src[hi] < 0:
                hi += 1
            shape = list(w.shape)
            shape[axis] = hi - lo
            pieces.append(jnp.zeros(shape, w.dtype))
        else:
            while hi < len(src) and src[hi] == src[hi - 1] + 1 and sgn[hi] == sgn[lo]:
                hi += 1
            piece = lax.slice_in_dim(w, int(src[lo]), int(src[lo]) + hi - lo, axis=axis)
            pieces.append(-piece if sgn[lo] < 0 else piece)
        lo = hi
    return jnp.concatenate(pieces, axis=axis)


def _pad_rows(w, n):
    pad = [(0, 0)] * w.ndim
    pad[-2] = (0, n - w.shape[-2])
    return jnp.pad(w, pad)


def _in_proj_layout():
    src = -np.ones(IN_PAD, np.int64)
    sgn = np.ones(IN_PAD, np.float32)
    src[COL_F:COL_F + F_WIDTH] = np.arange(F_WIDTH)
    o = F_WIDTH
    src[COL_CQ:COL_CQ + MLA_Q_LORA] = o + np.arange(MLA_Q_LORA)
    o += MLA_Q_LORA
    src[COL_CKV:COL_CKV + MLA_KV_LORA] = o + np.arange(MLA_KV_LORA)
    o += MLA_KV_LORA
    src[COL_KPE:COL_KPE + MLA_ROPE] = o + np.arange(MLA_ROPE)
    psrc, psgn = _rot_partner(MLA_ROPE)
    src[COL_KPE + MLA_ROPE:COL_KPE + 2 * MLA_ROPE] = o + psrc
    sgn[COL_KPE + MLA_ROPE:COL_KPE + 2 * MLA_ROPE] = psgn
    o += MLA_ROPE
    for h in range(GLA_HEADS):
        src[COL_GQ + h * HEAD_SLOT:COL_GQ + h * HEAD_SLOT + GLA_DK] = o + h * GLA_DK + np.arange(GLA_DK)
    o += GLA_QK_W
    for h in range(GLA_HEADS):
        src[COL_GK + h * HEAD_SLOT:COL_GK + h * HEAD_SLOT + GLA_DK] = o + h * GLA_DK + np.arange(GLA_DK)
    o += GLA_QK_W
    for h in range(GLA_HEADS):
        src[COL_GV + h * HEAD_SLOT:COL_GV + h * HEAD_SLOT + GLA_DV] = o + h * GLA_DV + np.arange(GLA_DV)
    o += GLA_WIDTH
    src[COL_LR:COL_LR + GLA_GATE_RANK] = o + np.arange(GLA_GATE_RANK)
    o += GLA_GATE_RANK
    for h in range(GLA_HEADS):
        src[COL_GG + h * HEAD_SLOT:COL_GG + h * HEAD_SLOT + GLA_DV] = o + h * GLA_DV + np.arange(GLA_DV)
    return src, sgn


def _head_slots(n_heads, d_src, d_take, src_off=0):
    src = -np.ones(n_heads * HEAD_SLOT, np.int64)
    for h in range(n_heads):
        src[h * HEAD_SLOT:h * HEAD_SLOT + d_take] = h * d_src + src_off + np.arange(d_take)
    return src


def _prep_weights(w):
    f32 = lambda a: a.astype(F32)
    out = {}
    src, sgn = _in_proj_layout()
    out["w_in"] = _take_cols(w["w_in"], src, sgn).astype(BF)

    m = np.arange(F_DIM)
    ang = 2.0 * np.pi * np.outer(m, m) / F_DIM
    c64, s64 = np.cos(ang), np.sin(ang)
    dft = np.zeros((F_WIDTH, 2 * F_WIDTH), np.float32)
    for g in range(F_GROUPS):
        dft[g * F_DIM:(g + 1) * F_DIM, g * F_DIM:(g + 1) * F_DIM] = c64
        dft[g * F_DIM:(g + 1) * F_DIM, F_WIDTH + g * F_DIM:F_WIDTH + (g + 1) * F_DIM] = s64
    out["dft64"] = jnp.asarray(dft, F32).astype(BF)

    psrc, psgn = _rot_partner(MLA_ROPE)
    q_src = _head_slots(MLA_HEADS, MLA_QK, MLA_QK)
    q_part = -np.ones(MLA_SLOTS, np.int64)
    q_psg = np.ones(MLA_SLOTS, np.float32)
    g_part = -np.ones(MLA_SLOTS, np.int64)
    for h in range(MLA_HEADS):
        lo = h * HEAD_SLOT + MLA_NOPE
        q_part[lo:lo + MLA_ROPE] = h * MLA_QK + MLA_NOPE + psrc
        q_psg[lo:lo + MLA_ROPE] = psgn
        g_part[lo:lo + MLA_ROPE] = MLA_NOPE + psrc
    g_src = np.where(q_src >= 0, q_src % MLA_QK, -1)
    wq = jnp.concatenate([_take_cols(w["mla_w_uq"], q_src), _take_cols(w["mla_w_uq"], q_part, q_psg)], axis=-1)
    out["w_uq"] = _pad_rows(wq, 256).astype(BF)
    out["q_lora_w"] = jnp.pad(f32(w["mla_q_lora_norm_w"]), ((0, 0), (0, 256 - MLA_Q_LORA)))[:, None, :]
    out["kv_lora_w"] = f32(w["mla_kv_lora_norm_w"])[:, None, :]
    out["q_gain"] = _take_cols(w["mla_q_norm_w"], g_src)[:, None, :]
    out["q_gain_p"] = _take_cols(w["mla_q_norm_w"], g_part)[:, None, :]
    out["k_gain"] = _take_cols(w["mla_k_norm_w"], g_src)[:, None, :]
    out["k_gain_p"] = _take_cols(w["mla_k_norm_w"], g_part)[:, None, :]

    kn_src = _head_slots(MLA_HEADS, MLA_NOPE + MLA_V, MLA_NOPE)
    v_src = _head_slots(MLA_HEADS, MLA_NOPE + MLA_V, MLA_V, MLA_NOPE)
    out["w_ukv"] = jnp.concatenate([_take_cols(w["mla_w_ukv"], kn_src), _take_cols(w["mla_w_ukv"], v_src)],
                                   axis=-1).astype(BF)

    e2 = np.zeros((LANE, 2 * MLA_SLOTS), np.float32)
    for h in range(MLA_HEADS):
        for j in range(MLA_ROPE):
            e2[j, h * HEAD_SLOT + MLA_NOPE + j] = 1.0
            e2[MLA_ROPE + j, MLA_SLOTS + h * HEAD_SLOT + MLA_NOPE + j] = 1.0
    out["kpe_place"] = jnp.asarray(e2, BF)

    gk_src = _head_slots(GLA_HEADS, GLA_DK, GLA_DK)
    wgk = jnp.concatenate([_take_cols(w["gla_w_gk_fwd"], gk_src), _take_cols(w["gla_w_gk_bwd"], gk_src)], axis=-1)
    out["w_gk"] = _pad_rows(wgk, LANE).astype(BF)
    out["b_gk"] = jnp.concatenate([_take_cols(w["gla_b_gk_fwd"], gk_src), _take_cols(w["gla_b_gk_bwd"], gk_src)],
                                  axis=-1)[:, None, :]
    gv_src = _head_slots(GLA_HEADS, GLA_DV, GLA_DV)
    out["gla_gain"] = _take_cols(w["gla_norm_w"], np.where(gv_src >= 0, gv_src % GLA_DV, -1))[:, None, :]

    row_src = np.concatenate([np.arange(F_WIDTH + MLA_WIDTH), np.where(gv_src >= 0, F_WIDTH + MLA_WIDTH + gv_src, -1)])
    out["w_out"] = _take_cols(w["w_out"], row_src, axis=-2).astype(BF)

    nd = w["ffn_w_12"].shape[0]
    out["w12"] = w["ffn_w_12"].astype(BF)
    out["conv"] = jnp.concatenate([w["ffn_conv_w"], w["ffn_conv_b"][:, None, :]], axis=1).astype(F32)
    out["w_down"] = w["ffn_w_down"].astype(BF)
    out["norm1_w"] = f32(w["norm1_w"])[:, None, :]
    out["norm2_w"] = f32(w["norm2_w"])[:, None, :]
    return out


def _rope_tables(n_tokens):
    q = MLA_ROPE // 4
    t = np.arange(n_tokens)
    row = (t // GRID_W).astype(np.float32)
    col = (t % GRID_W).astype(np.float32)
    axis_dims = MLA_ROPE // 2
    inv_freq = np.power(np.float32(ROPE_BASE), -np.arange(0, axis_dims, 2, dtype=np.float32) / axis_dims)
    ang_r = row[:, None] * inv_freq
    ang_c = col[:, None] * inv_freq
    cos = np.ones((n_tokens, HEAD_SLOT), np.float32)
    sin = np.zeros((n_tokens, HEAD_SLOT), np.float32)
    for k, ang in enumerate((ang_r, ang_r, ang_c, ang_c)):
        lo = MLA_NOPE + k * q
        cos[:, lo:lo + q] = np.cos(ang)
        sin[:, lo:lo + q] = np.sin(ang)
    return jnp.asarray(cos), jnp.asarray(sin)


def _fft_tables(n1, n2):
    k = np.arange(n1)
    ang1 = 2.0 * np.pi * np.outer(k, k) / n1
    fr, fi = np.cos(ang1), -np.sin(ang1)
    w1 = np.block([[fr, fi], [fi, -fr]])
    length = n1 * n2
    kk = (np.arange(n1)[:, None] + n1 * np.arange(n2)[None, :]).astype(np.float64)
    t2 = np.arange(n2, dtype=np.float64)
    ang2 = 2.0 * np.pi * kk[:, :, None] * t2[None, None, :] / length
    norm = 1.0 / math.sqrt(length * F_DIM)
    tab = np.concatenate([np.cos(ang2), np.sin(ang2)], axis=-1) * norm
    return jnp.asarray(w1, F32).astype(BF), jnp.asarray(tab, F32).astype(BF)


def _ctx_dft_table(n):
    k = np.arange(n)
    ang = 2.0 * np.pi * np.outer(k, k) / n
    norm = 1.0 / math.sqrt(n * F_DIM)
    return jnp.asarray(np.concatenate([np.cos(ang), -np.sin(ang)], axis=1) * norm, F32).astype(BF)


def _mod_kernel(c_ref, w_ref, b_ref, o_ref):
    c = c_ref[...]
    s = c * (1.0 / (1.0 + jnp.exp(-c)))
    o_ref[...] = _dot(s.astype(BF), w_ref[...].astype(BF)) + b_ref[...]


def _adaln(cond8, mod_w, mod_b):
    nd, d, n = mod_w.shape
    tn = 1536
    return pl.pallas_call(
        _mod_kernel,
        out_shape=jax.ShapeDtypeStruct((nd, 8, n), F32),
        grid=(nd, n // tn),
        in_specs=[pl.BlockSpec((8, d), lambda l, j: (0, 0)),
                  pl.BlockSpec((None, d, tn), lambda l, j: (l, 0, j)),
                  pl.BlockSpec((None, 1, tn), lambda l, j: (l, 0, j))],
        out_specs=pl.BlockSpec((None, 8, tn), lambda l, j: (l, 0, j)),
        compiler_params=_cparams("arbitrary", "arbitrary"),
        name="adaln",
    )(cond8, mod_w, mod_b.reshape(nd, 1, n))


def _rms(x, n):
    return lax.rsqrt(jnp.sum(x * x, axis=-1, keepdims=True) * (1.0 / n) + EPS)


def _in_kernel(x_ref, sh_ref, sc_ref, n1_ref, win_ref, dft_ref, qlw_ref, wq_ref, kvlw_ref, wkv_ref, place_ref,
               qg_ref, qgp_ref, kg_ref, kgp_ref, cos_ref, sin_ref, wgk_ref, bgk_ref,
               fa_ref, fb_ref, q_ref, k_ref, v_ref, gq_ref, gk_ref, gv_ref, gg_ref, df_ref, db_ref,
               *, row, rope):
    for r0 in range(0, x_ref.shape[0], IN_SUB):
        rs = slice(r0, min(r0 + IN_SUB, x_ref.shape[0]))
        x = x_ref[rs, :]
        xn = x * _rms(x, D_MODEL) * n1_ref[...]
        h = xn * (1.0 + sc_ref[row:row + 1, :]) + sh_ref[row:row + 1, :]
        hb = h.astype(BF)
        p = _dot(hb, win_ref[:, :COL_GQ])

        ab = _dot(p[:, COL_F:COL_F + F_WIDTH].astype(BF), dft_ref[...])
        fa_ref[rs, :] = ab[:, :F_WIDTH].astype(BF)
        fb_ref[rs, :] = ab[:, F_WIDTH:].astype(BF)

        cq = p[:, COL_CQ:COL_CQ + 256]
        cqn = cq * _rms(cq, MLA_Q_LORA) * qlw_ref[...]
        qq = _dot(cqn.astype(BF), wq_ref[...])
        ckv = p[:, COL_CKV:COL_CKV + MLA_KV_LORA]
        ckvn = ckv * _rms(ckv, MLA_KV_LORA) * kvlw_ref[...]
        kk = _dot(ckvn.astype(BF), wkv_ref[...])
        kp = _dot(p[:, COL_KPE:COL_KPE + LANE].astype(BF), place_ref[...])
        pg = _dot(hb, win_ref[:, COL_GQ:])
        v_lane = lax.broadcasted_iota(jnp.int32, (x.shape[0], HEAD_SLOT), 1)

        q_scale = (MLA_QK ** -0.5) * LOG2E
        for hd in range(MLA_HEADS):
            sl = slice(hd * HEAD_SLOT, (hd + 1) * HEAD_SLOT)
            sp = slice(MLA_SLOTS + hd * HEAD_SLOT, MLA_SLOTS + (hd + 1) * HEAD_SLOT)
            qh = qq[:, sl]
            kh = kk[:, sl] + kp[:, sl]
            rq = _rms(qh, MLA_QK) * q_scale
            rk = _rms(kh, MLA_QK)
            if rope:
                cos = cos_ref[rs, :]
                sin = sin_ref[rs, :]
                qo = rq * (qh * (qg_ref[:, sl] * cos) + qq[:, sp] * (qgp_ref[:, sl] * sin))
                ko = rk * (kh * (kg_ref[:, sl] * cos) + kp[:, sp] * (kgp_ref[:, sl] * sin))
            else:
                qo = rq * (qh * qg_ref[:, sl])
                ko = rk * (kh * kg_ref[:, sl])
            q_ref[sl, rs] = qo.T.astype(BF)
            k_ref[rs, sl] = ko.astype(BF)
            vh = jnp.where(v_lane == MLA_V, 1.0, kk[:, MLA_SLOTS + hd * HEAD_SLOT:MLA_SLOTS + (hd + 1) * HEAD_SLOT])
            v_ref[hd * V_ROWS:(hd + 1) * V_ROWS, rs] = vh.T[:V_ROWS, :].astype(BF)

        gq_ref[rs, :] = (pg[:, 0:GLA_SLOTS] * (GLA_DK ** -0.5)).astype(BF)
        gk_ref[rs, :] = pg[:, COL_GK - COL_GQ:COL_GK - COL_GQ + GLA_SLOTS].astype(BF)
        gv_ref[rs, :] = pg[:, COL_GV - COL_GQ:COL_GV - COL_GQ + GLA_SLOTS].astype(BF)
        gg_ref[rs, :] = pg[:, COL_GG - COL_GQ:COL_GG - COL_GQ + GLA_SLOTS].astype(BF)
        z = _dot(pg[:, COL_LR - COL_GQ:COL_LR - COL_GQ + LANE].astype(BF), wgk_ref[...]) + bgk_ref[...]
        ls = (jnp.minimum(z, 0.0) - jnp.log1p(jnp.exp(-jnp.abs(z)))) * (1.0 / GLA_GATE_NORM)
        df_ref[rs, :] = ls[:, :GLA_SLOTS]
        db_ref[rs, :] = ls[:, GLA_SLOTS:]


def _in_proj(x2, mods, pw, cos_t, sin_t, l, row, rope):
    rows = x2.shape[0]
    tm = min(2 * IN_SUB, rows)
    rt = lambda n: pl.BlockSpec((tm, n), lambda i: (i, 0))
    modspec = lambda k: pl.BlockSpec((None, 8, D_MODEL), lambda i: (l, 0, k))
    outs = [(F_WIDTH, BF), (F_WIDTH, BF), None, (MLA_SLOTS, BF), None,
            (GLA_SLOTS, BF), (GLA_SLOTS, BF), (GLA_SLOTS, BF), (GLA_SLOTS, BF), (GLA_SLOTS, F32), (GLA_SLOTS, F32)]
    shapes = [jax.ShapeDtypeStruct((rows, o[0]), o[1]) if o else None for o in outs]
    specs = [rt(o[0]) if o else None for o in outs]
    shapes[2] = jax.ShapeDtypeStruct((MLA_SLOTS, rows), BF)
    specs[2] = pl.BlockSpec((MLA_SLOTS, tm), lambda i: (0, i))
    shapes[4] = jax.ShapeDtypeStruct((rows // tm, MLA_HEADS * V_ROWS, tm), BF)
    specs[4] = pl.BlockSpec((None, MLA_HEADS * V_ROWS, tm), lambda i: (i, 0, 0))
    return pl.pallas_call(
        functools.partial(_in_kernel, row=row, rope=rope),
        out_shape=shapes,
        grid=(rows // tm,),
        in_specs=[rt(D_MODEL), modspec(0), modspec(1), _layer((1, D_MODEL), l),
                  _layer((D_MODEL, IN_PAD), l), _full((F_WIDTH, 2 * F_WIDTH)),
                  _layer((1, 256), l), _layer((256, 2 * MLA_SLOTS), l),
                  _layer((1, MLA_KV_LORA), l), _layer((MLA_KV_LORA, 2 * MLA_SLOTS), l),
                  _full((LANE, 2 * MLA_SLOTS)),
                  _layer((1, MLA_SLOTS), l), _layer((1, MLA_SLOTS), l), _layer((1, MLA_SLOTS), l),
                  _layer((1, MLA_SLOTS), l),
                  rt(HEAD_SLOT), rt(HEAD_SLOT),
                  _layer((LANE, 2 * GLA_SLOTS), l), _layer((1, 2 * GLA_SLOTS), l)],
        out_specs=specs,
        compiler_params=_cparams("arbitrary"),
        name="in_proj",
    )(x2, mods, mods, pw["norm1_w"], pw["w_in"], pw["dft64"], pw["q_lora_w"], pw["w_uq"], pw["kv_lora_w"],
      pw["w_ukv"], pw["kpe_place"], pw["q_gain"], pw["q_gain_p"], pw["k_gain"], pw["k_gain_p"],
      cos_t, sin_t, pw["w_gk"], pw["b_gk"])


def _fft1_kernel(a_ref, b_ref, w_ref, g_ref):
    n1 = a_ref.shape[0]
    ab = jnp.concatenate([a_ref[...], b_ref[...]], axis=0)
    g = _dot(w_ref[...], ab)
    g_ref[0] = g[:n1].astype(BF)
    g_ref[1] = g[n1:].astype(BF)


def _fft2_kernel(g_ref, t_ref, o_ref, *, batch):
    for j in range(batch):
        g = jnp.concatenate([g_ref[0, j], g_ref[1, j]], axis=0)
        o_ref[:, j * F_WIDTH:(j + 1) * F_WIDTH] = _dot(t_ref[j], g).astype(BF)


def _fourier_latent(fa, fb, w1, tab):
    length = fa.shape[0]
    n1, n2 = tab.shape[0], tab.shape[1]
    cols = n2 * F_WIDTH
    tn = min(4096, cols)
    g = pl.pallas_call(
        _fft1_kernel,
        out_shape=jax.ShapeDtypeStruct((2, n1, cols), BF),
        grid=(cols // tn,),
        in_specs=[pl.BlockSpec((n1, tn), lambda j: (0, j)), pl.BlockSpec((n1, tn), lambda j: (0, j)),
                  _full((2 * n1, 2 * n1))],
        out_specs=pl.BlockSpec((2, n1, tn), lambda j: (0, 0, j)),
        compiler_params=_cparams("arbitrary"),
        name="fft_stage1",
    )(fa.reshape(n1, cols), fb.reshape(n1, cols), w1)
    batch = 8
    y = pl.pallas_call(
        functools.partial(_fft2_kernel, batch=batch),
        out_shape=jax.ShapeDtypeStruct((n2, n1 * F_WIDTH), BF),
        grid=(n1 // batch,),
        in_specs=[pl.BlockSpec((2, batch, n2, F_WIDTH), lambda i: (0, i, 0, 0)),
                  pl.BlockSpec((batch, n2, 2 * n2), lambda i: (i, 0, 0))],
        out_specs=pl.BlockSpec((n2, batch * F_WIDTH), lambda i: (0, i)),
        compiler_params=_cparams("arbitrary"),
        name="fft_stage2",
    )(g.reshape(2, n1, n2, F_WIDTH), tab)
    return y.reshape(length, F_WIDTH)


def _fctx_kernel(a_ref, b_ref, t_ref, o_ref):
    ab = jnp.concatenate([a_ref[...], b_ref[...]], axis=0)
    o_ref[...] = _dot(t_ref[...], ab).astype(BF)


def _fourier_ctx(fa, fb, tab):
    n = fa.shape[0]
    return pl.pallas_call(
        _fctx_kernel,
        out_shape=jax.ShapeDtypeStruct((n, F_WIDTH), BF),
        grid=(1,),
        in_specs=[_full((n, F_WIDTH)), _full((n, F_WIDTH)), _full((n, 2 * n))],
        out_specs=_full((n, F_WIDTH)),
        compiler_params=_cparams("arbitrary"),
        name="fft_ctx",
    )(fa, fb, tab)


def _attn_kernel(*refs, tk, has_x):
    if has_x:
        q_ref, kc_ref, vc_ref, kx_ref, vx_ref, o_ref, acc_ref, s_ref = refs
    else:
        q_ref, kc_ref, vc_ref, o_ref, acc_ref = refs
    tq = q_ref.shape[1]
    heads = range(2)
    slot = lambda h: slice(h * HEAD_SLOT, (h + 1) * HEAD_SLOT)
    vrow = lambda h: slice(h * V_ROWS, (h + 1) * V_ROWS)
    qt = [q_ref[slot(h), :] for h in heads]

    def scores(k2):
        return tuple(_dot(k2[:, slot(h)], qt[h]) for h in heads)

    def absorb(s, vts, m):
        sub = vts[0].shape[1]
        m_new = [jnp.maximum(m[h], jnp.max(s[h], axis=0, keepdims=True)) for h in heads]
        p = [jnp.exp2(s[h] - m_new[h]).astype(BF) for h in heads]
        for h in heads:
            alpha = jnp.exp2(m[h] - m_new[h])
            pv = _dot(vts[0][vrow(h), :], p[h][0:sub, :])
            for t in range(1, len(vts)):
                pv += _dot(vts[t][vrow(h), :], p[h][t * sub:(t + 1) * sub, :])
            acc_ref[h] = alpha * acc_ref[h] + pv
        return tuple(m_new)

    acc_ref[...] = jnp.zeros_like(acc_ref)
    m = tuple(jnp.full((1, tq), -1e30, F32) for _ in heads)
    m = absorb(scores(kc_ref[...]), [vc_ref[t] for t in range(vc_ref.shape[0])], m)
    if has_x:
        n = kx_ref.shape[0] // tk
        per = tk // vx_ref.shape[2]
        assert n % 2 == 0

        def put_scores(buf, j):
            off = pl.multiple_of(j * tk, tk)
            s = scores(kx_ref[pl.ds(off, tk), :])
            for h in heads:
                s_ref[buf, h] = s[h]

        def take(buf, j, m):
            return absorb(tuple(s_ref[buf, h] for h in heads), [vx_ref[j * per + t] for t in range(per)], m)

        unroll = 4 if n % 4 == 0 else 2

        def body(i, m):
            for u in range(unroll):
                j = unroll * i + u
                nxt = j + 1 if u + 1 < unroll else jnp.minimum(j + 1, n - 1)
                put_scores((u + 1) % 2, nxt)
                m = take(u % 2, j, m)
            return m

        put_scores(0, 0)
        m = lax.fori_loop(0, n // unroll, body, m)
    for h in heads:
        acc = acc_ref[h]
        o_ref[h * MLA_V:(h + 1) * MLA_V, :] = (acc[:MLA_V, :] / acc[MLA_V:MLA_V + 1, :]).astype(BF)


def _attention(qt, kc, vct, kx=None, vxt=None):
    rows = qt.shape[1]
    lc = kc.shape[0]
    tq = min(512, rows)
    has_x = kx is not None
    pair = 2 * HEAD_SLOT
    vpair = 2 * V_ROWS
    in_specs = [pl.BlockSpec((pair, tq), lambda p, i: (p, i)),
                pl.BlockSpec((lc, pair), lambda p, i: (0, p)),
                pl.BlockSpec((vct.shape[0], vpair, vct.shape[2]), lambda p, i: (0, p, 0))]
    args = [qt, kc, vct]
    tk = 512
    scratch = [pltpu.VMEM((2, V_ROWS, tq), F32)]
    if has_x:
        lx = kx.shape[0]
        tk = min(tk, lx)
        in_specs += [pl.BlockSpec((lx, pair), lambda p, i: (0, p)),
                     pl.BlockSpec((vxt.shape[0], vpair, vxt.shape[2]), lambda p, i: (0, p, 0))]
        args += [kx, vxt]
        scratch.append(pltpu.VMEM((2, 2, tk, tq), F32))
    return pl.pallas_call(
        functools.partial(_attn_kernel, tk=tk, has_x=has_x),
        out_shape=jax.ShapeDtypeStruct((MLA_WIDTH, rows), BF),
        grid=(MLA_HEADS // 2, rows // tq),
        in_specs=in_specs,
        out_specs=pl.BlockSpec((2 * MLA_V, tq), lambda p, i: (p, i)),
        scratch_shapes=scratch,
        compiler_params=_cparams("arbitrary", "arbitrary"),
        name="attention",
    )(*args)


def _gla_kernel(qf_ref, kf_ref, vf_ref, df_ref, qb_ref, kb_ref, vb_ref, db_ref, s0_ref,
                of_ref, ob_ref, sfin_ref, st_ref):
    i = pl.program_id(0)
    t = qf_ref.shape[0]
    c = GLA_CHUNK
    nc = t // c
    assert c & (c - 1) == 0

    @pl.when(i == 0)
    def _():
        st_ref[...] = s0_ref[...]

    r = lax.broadcasted_iota(jnp.int32, (t, t), 0)
    s = lax.broadcasted_iota(jnp.int32, (t, t), 1)
    same_chunk = jnp.bitwise_xor(r, s) < c
    dirs = ((qf_ref, kf_ref, vf_ref, df_ref, of_ref, same_chunk & (s <= r), range(nc), c - 1),
            (qb_ref, kb_ref, vb_ref, db_ref, ob_ref, same_chunk & (s >= r), range(nc - 1, -1, -1), 0))
    heads = range(GLA_HEADS)
    slot = lambda h: slice(h * HEAD_SLOT, (h + 1) * HEAD_SLOT)
    chunk = lambda ch: slice(ch * c, (ch + 1) * c)

    b = []
    for q_ref, k_ref, v_ref, g_ref, o_ref, mask, order, last in dirs:
        tri = mask.astype(BF)
        g = g_ref[...]
        g_hi = g.astype(BF)
        g_r = g - g_hi.astype(F32)
        g_mid = g_r.astype(BF)
        g_lo = (g_r - g_mid.astype(F32)).astype(BF)
        b.append(_dot(tri, g_hi) + _dot(tri, g_mid) + _dot(tri, g_lo))

    work = []
    for d, (q_ref, k_ref, v_ref, g_ref, o_ref, mask, order, last) in enumerate(dirs):
        tot = [b[d][ch * c + last:ch * c + last + 1, :] for ch in range(nc)]
        b_tot = jnp.concatenate([jnp.broadcast_to(tot[ch], (c, GLA_SLOTS)) for ch in range(nc)], axis=0)
        q_in = (q_ref[...].astype(F32) * jnp.exp(b[d])).astype(BF)
        kf = k_ref[...].astype(F32)
        k_in = (kf * jnp.exp(-b[d])).astype(BF)
        k_out = (kf * jnp.exp(b_tot - b[d])).astype(BF)
        v = v_ref[...]
        a = [_dot_nt(q_in[:, slot(h)], k_in[:, slot(h)]) for h in heads]
        inc = [{ch: _dot_tn(v[chunk(ch), slot(h)], k_out[chunk(ch), slot(h)]) for ch in order} for h in heads]
        work.append((tot, q_in, v, a, inc))

    o_intra = []
    for d, (q_ref, k_ref, v_ref, g_ref, o_ref, mask, order, last) in enumerate(dirs):
        tot, q_in, v, a, inc = work[d]
        o_intra.append([_dot(jnp.where(mask, a[h], 0.0).astype(BF), v[:, slot(h)]) for h in heads])

    for d, (q_ref, k_ref, v_ref, g_ref, o_ref, mask, order, last) in enumerate(dirs):
        tot, q_in, v, a, inc = work[d]
        entering = []
        for h in heads:
            st = st_ref[d, h]
            ent = {}
            for ch in order:
                ent[ch] = st.astype(BF)
                st = st * jnp.exp(tot[ch][:, slot(h)]) + inc[h][ch]
            st_ref[d, h] = st
            entering.append(ent)
        for h in heads:
            for ch in order:
                o_ref[chunk(ch), slot(h)] = (o_intra[d][h][chunk(ch), :]
                                             + _dot_nt(q_in[chunk(ch), slot(h)], entering[h][ch]))

    @pl.when(i == pl.num_programs(0) - 1)
    def _():
        sfin_ref[...] = st_ref[...]


def _gla(gq, gk, gv, df, db, s0):
    rows = gq.shape[0]
    c = min(GLA_STEP, rows)
    n = rows // c
    fwd = pl.BlockSpec((c, GLA_SLOTS), lambda i: (i, 0))
    bwd = pl.BlockSpec((c, GLA_SLOTS), lambda i: (n - 1 - i, 0))
    st_shape = (2, GLA_HEADS, HEAD_SLOT, HEAD_SLOT)
    return pl.pallas_call(
        _gla_kernel,
        out_shape=[jax.ShapeDtypeStruct((rows, GLA_SLOTS), F32), jax.ShapeDtypeStruct((rows, GLA_SLOTS), F32),
                   jax.ShapeDtypeStruct(st_shape, F32)],
        grid=(n,),
        in_specs=[fwd, fwd, fwd, fwd, bwd, bwd, bwd, bwd, _full(st_shape)],
        out_specs=[fwd, bwd, _full(st_shape)],
        scratch_shapes=[pltpu.VMEM(st_shape, F32)],
        compiler_params=_cparams("arbitrary"),
        name="gla_scan",
    )(gq, gk, gv, df, gq, gk, gv, db, s0)


def _out_kernel(x_ref, four_ref, att_ref, of_ref, ob_ref, gg_ref, gain_ref, wout_ref, g1_ref, n2_ref,
                sh_ref, sc_ref, xo_ref, h_ref, *, row):
    o = of_ref[...] + ob_ref[...]
    g = gg_ref[...].astype(F32)
    gate = g * (1.0 / (1.0 + jnp.exp(-g)))
    y = _dot(four_ref[...], wout_ref[0:F_WIDTH, :])
    y += _dot_tn(att_ref[...], wout_ref[F_WIDTH:F_WIDTH + MLA_WIDTH, :])
    for h in range(GLA_HEADS):
        sl = slice(h * HEAD_SLOT, (h + 1) * HEAD_SLOT)
        oh = o[:, sl]
        lin = oh * _rms(oh, GLA_DV) * gain_ref[:, sl] * gate[:, sl]
        lo = F_WIDTH + MLA_WIDTH + h * HEAD_SLOT
        y += _dot(lin.astype(BF), wout_ref[lo:lo + HEAD_SLOT, :])
    x = x_ref[...] + g1_ref[row:row + 1, :] * y
    xo_ref[...] = x
    hn = x * _rms(x, D_MODEL) * n2_ref[...]
    h_ref[...] = (hn * (1.0 + sc_ref[row:row + 1, :]) + sh_ref[row:row + 1, :]).astype(BF)


def _out_proj(x2, four, att, o_f, o_b, gg, mods, pw, l, row):
    rows = x2.shape[0]
    tm = min(ROW_TILE, rows)
    rt = lambda n: pl.BlockSpec((tm, n), lambda i: (i, 0))
    modspec = lambda k: pl.BlockSpec((None, 8, D_MODEL), lambda i: (l, 0, k))
    return pl.pallas_call(
        functools.partial(_out_kernel, row=row),
        out_shape=[jax.ShapeDtypeStruct((rows, D_MODEL), F32), jax.ShapeDtypeStruct((rows, D_MODEL), BF)],
        grid=(rows // tm,),
        in_specs=[rt(D_MODEL), rt(F_WIDTH), pl.BlockSpec((MLA_WIDTH, tm), lambda i: (0, i)),
                  rt(GLA_SLOTS), rt(GLA_SLOTS), rt(GLA_SLOTS),
                  _layer((1, GLA_SLOTS), l), _layer((MIX_PAD, D_MODEL), l), modspec(2),
                  _layer((1, D_MODEL), l), modspec(3), modspec(4)],
        out_specs=[rt(D_MODEL), rt(D_MODEL)],
        compiler_params=_cparams("arbitrary"),
        name="out_proj",
    )(x2, four, att, o_f, o_b, gg, pw["gla_gain"], pw["w_out"], mods, pw["norm2_w"], mods, mods)


def _ffn_kernel(x_ref, h_ref, hp_ref, hn_ref, w12_ref, conv_ref, wd_ref, g2_ref, o_ref, acc_ref, hx_ref, u_ref,
                *, row):
    i = pl.program_id(0)
    tm = h_ref.shape[0]
    m = tm + 2 * FF_HALO
    keep_prev = jnp.where(i > 0, 1.0, 0.0)
    keep_next = jnp.where(i < pl.num_programs(0) - 1, 1.0, 0.0)
    hx_ref[0:FF_HALO, :] = (hp_ref[...].astype(F32) * keep_prev).astype(BF)
    hx_ref[FF_HALO:FF_HALO + tm, :] = h_ref[...]
    hx_ref[FF_HALO + tm:m, :] = (hn_ref[...].astype(F32) * keep_next).astype(BF)
    acc_ref[...] = jnp.zeros_like(acc_ref)

    def cols(c, half):
        lo = half * D_FF + c * FF_CHUNK
        return slice(lo, lo + FF_CHUNK)

    def up(buf, c):
        for half in range(2):
            u_ref[buf, half] = _dot(hx_ref[...], w12_ref[:, cols(c, half)])

    def conv(buf, c, half):
        u = u_ref[buf, half]
        cw = conv_ref[:, cols(c, half)]
        u_prev = pltpu.roll(u, 1, axis=0)
        u_next = pltpu.roll(u, m - 1, axis=0)
        uc = u_prev * cw[0:1, :] + u * cw[1:2, :] + u_next * cw[2:3, :] + cw[3:4, :]
        return uc[FF_HALO:FF_HALO + tm, :]

    def down(buf, c):
        a = conv(buf, c, 0)
        act = a * (1.0 / (1.0 + jnp.exp(-a))) * conv(buf, c, 1)
        acc_ref[...] += _dot(act.astype(BF), wd_ref[c * FF_CHUNK:(c + 1) * FF_CHUNK, :])

    up(0, 0)
    for c in range(FF_NCHUNK):
        if c + 1 < FF_NCHUNK:
            up((c + 1) % 2, c + 1)
        down(c % 2, c)
    o_ref[...] = x_ref[...] + g2_ref[row:row + 1, :] * acc_ref[...]


def _ffn(x2, h2, mods, pw, l, row):
    rows = x2.shape[0]
    tm = min(512, rows)
    nt = rows // tm
    per = tm // FF_HALO
    last_blk = rows // FF_HALO - 1
    m = tm + 2 * FF_HALO
    rt = lambda n: pl.BlockSpec((tm, n), lambda i: (i, 0))
    once = lambda shape: pl.BlockSpec((None,) + shape, lambda i: (l,) + (0,) * len(shape),
                                      pipeline_mode=pl.Buffered(1))
    return pl.pallas_call(
        functools.partial(_ffn_kernel, row=row),
        out_shape=jax.ShapeDtypeStruct((rows, D_MODEL), F32),
        grid=(nt,),
        in_specs=[rt(D_MODEL), rt(D_MODEL),
                  pl.BlockSpec((FF_HALO, D_MODEL), lambda i: (jnp.maximum(i * per - 1, 0), 0)),
                  pl.BlockSpec((FF_HALO, D_MODEL), lambda i: (jnp.minimum((i + 1) * per, last_blk), 0)),
                  once((D_MODEL, 2 * D_FF)), once((4, 2 * D_FF)), once((D_FF, D_MODEL)),
                  pl.BlockSpec((None, 8, D_MODEL), lambda i: (l, 0, 5))],
        out_specs=rt(D_MODEL),
        scratch_shapes=[pltpu.VMEM((tm, D_MODEL), F32), pltpu.VMEM((m, D_MODEL), BF),
                        pltpu.VMEM((2, 2, m, FF_CHUNK), F32)],
        compiler_params=_cparams("arbitrary"),
        name="conv_ffn",
    )(x2, h2, h2, h2, pw["w12"], pw["conv"], pw["w_down"], mods)


def kernel(x, c, ctx, c_ctx, mod_w, mod_b, norm1_w, norm2_w, w_in, mla_q_lora_norm_w, mla_w_uq, mla_kv_lora_norm_w, mla_w_ukv, mla_q_norm_w, mla_k_norm_w, gla_w_gk_fwd, gla_b_gk_fwd, gla_w_gk_bwd, gla_b_gk_bwd, gla_norm_w, w_out, ffn_w_12, ffn_conv_w, ffn_conv_b, ffn_w_down):
    batch, seq, d = x.shape
    assert batch == 1 and d == D_MODEL
    lc = ctx.shape[1]
    depth = mod_w.shape[0]
    n1 = int(round(math.sqrt(seq)))
    assert n1 * n1 == seq and seq % ROW_TILE == 0 and lc % GLA_CHUNK == 0

    pw = _prep_weights(dict(
        w_in=w_in, mla_w_uq=mla_w_uq, mla_q_lora_norm_w=mla_q_lora_norm_w, mla_kv_lora_norm_w=mla_kv_lora_norm_w,
        mla_w_ukv=mla_w_ukv, mla_q_norm_w=mla_q_norm_w, mla_k_norm_w=mla_k_norm_w, gla_w_gk_fwd=gla_w_gk_fwd,
        gla_b_gk_fwd=gla_b_gk_fwd, gla_w_gk_bwd=gla_w_gk_bwd, gla_b_gk_bwd=gla_b_gk_bwd, gla_norm_w=gla_norm_w,
        w_out=w_out, ffn_w_12=ffn_w_12, ffn_conv_w=ffn_conv_w, ffn_conv_b=ffn_conv_b, ffn_w_down=ffn_w_down,
        norm1_w=norm1_w, norm2_w=norm2_w))
    cos_t, sin_t = _rope_tables(seq)
    ones_c = jnp.ones((lc, HEAD_SLOT), F32)
    w1, tab = _fft_tables(n1, n1)
    tab_c = _ctx_dft_table(lc)

    cond8 = jnp.zeros((8, d), F32).at[0].set(c[0].astype(F32)).at[1].set(c_ctx.astype(F32))
    mods = _adaln(cond8, mod_w, mod_b)

    xs = x[0].astype(F32)
    xc = ctx[0].astype(F32)
    s_zero = jnp.zeros((2, GLA_HEADS, HEAD_SLOT, HEAD_SLOT), F32)
    for l in range(depth):
        last = l == depth - 1
        fa_c, fb_c, q_c, k_c, v_c, gq_c, gk_c, gv_c, gg_c, df_c, db_c = _in_proj(xc, mods, pw, ones_c, ones_c, l, 1, False)
        fa_x, fb_x, q_x, k_x, v_x, gq_x, gk_x, gv_x, gg_x, df_x, db_x = _in_proj(xs, mods, pw, cos_t, sin_t, l, 0, True)

        of_c, ob_c, s_c = _gla(gq_c, gk_c, gv_c, df_c, db_c, s_zero)
        of_x, ob_x, _ = _gla(gq_x, gk_x, gv_x, df_x, db_x, s_c)
        four_x = _fourier_latent(fa_x, fb_x, w1, tab)
        att_x = _attention(q_x, k_c, v_c, k_x, v_x)
        x_mid, h2 = _out_proj(xs, four_x, att_x, of_x, ob_x, gg_x, mods, pw, l, 0)
        xs = _ffn(x_mid, h2, mods, pw, l, 0)
        if not last:
            four_c = _fourier_ctx(fa_c, fb_c, tab_c)
            att_c = _attention(q_c, k_c, v_c)
            c_mid, hc2 = _out_proj(xc, four_c, att_c, of_c, ob_c, gg_c, mods, pw, l, 1)
            xc = _ffn(c_mid, hc2, mods, pw, l, 1)
    return xs[None].astype(x.dtype)
```

```python
import functools
import math

import numpy as np
import jax
import jax.numpy as jnp
from jax import lax
from jax.experimental import pallas as pl
from jax.experimental.pallas import tpu as pltpu

D_MODEL = 1024
DEPTH = 2
GRID_W = 64
EPS = 1e-6

F_GROUPS = 4
F_DIM = 64
F_WIDTH = F_GROUPS * F_DIM

MLA_HEADS = 6
MLA_Q_LORA = 192
MLA_KV_LORA = 128
MLA_NOPE = 64
MLA_ROPE = 32
MLA_V = 64
MLA_QK = MLA_NOPE + MLA_ROPE
MLA_WIDTH = MLA_HEADS * MLA_V
MLA_IN = MLA_Q_LORA + MLA_KV_LORA + MLA_ROPE
ROPE_BASE = 10000.0

GLA_HEADS = 4
GLA_DK = 48
GLA_DV = 96
GLA_GATE_RANK = 16
GLA_GATE_NORM = 16.0
GLA_CHUNK = 64
GLA_STEP = 256
GLA_WIDTH = GLA_HEADS * GLA_DV
GLA_QK_W = GLA_HEADS * GLA_DK

MIX_WIDTH = F_WIDTH + MLA_WIDTH + GLA_WIDTH
D_FF = 2816
N_MOD = 6

LANE = 128
SUBLANE_BF16 = 16
VMEM_LIMIT = 48 * 1024 * 1024

HEAD_SLOT = LANE
V_ROWS = 80
MLA_SLOTS = MLA_HEADS * HEAD_SLOT
GLA_SLOTS = GLA_HEADS * HEAD_SLOT
GLA_QK_SLOT = 64
GLA_QK_SLOTS = GLA_HEADS * GLA_QK_SLOT

COL_F = 0
COL_CQ = COL_F + F_WIDTH
COL_CKV = COL_CQ + 256
COL_KPE = COL_CKV + MLA_KV_LORA
COL_GQ = COL_KPE + LANE
COL_GK = COL_GQ + GLA_QK_SLOTS
COL_GV = COL_GK + GLA_QK_SLOTS
COL_GG = COL_GV + GLA_SLOTS
COL_LR = COL_GG + GLA_SLOTS
IN_PAD = COL_LR + LANE

MIX_PAD = F_WIDTH + MLA_WIDTH + GLA_SLOTS

FF_CHUNK = 256
FF_NCHUNK = D_FF // FF_CHUNK
FF_HALO = SUBLANE_BF16

LOG2E = 1.4426950408889634

ROW_TILE = 256
IN_SUB = 256
ATTN_UNROLL = 16
ATTN_HEADS = 2
BF = jnp.bfloat16
F32 = jnp.float32


def _cparams(*sem):
    return pltpu.CompilerParams(dimension_semantics=sem, vmem_limit_bytes=VMEM_LIMIT)


def _dot(a, b):
    return jnp.dot(a, b, preferred_element_type=F32)


def _dot_nt(a, b):
    return lax.dot_general(a, b, (((1,), (1,)), ((), ())), preferred_element_type=F32)


def _dot_tn(a, b):
    return lax.dot_general(a, b, (((0,), (0,)), ((), ())), preferred_element_type=F32)


def _full(shape):
    n = len(shape)
    return pl.BlockSpec(shape, lambda *_: (0,) * n)


def _layer(shape, l):
    n = len(shape)
    return pl.BlockSpec((None,) + tuple(shape), lambda *_: (l,) + (0,) * n)


def _rot_partner(n_rope):
    q = n_rope // 4
    src = np.zeros(n_rope, np.int64)
    sgn = np.zeros(n_rope, np.float32)
    for base in (0, 2 * q):
        for j in range(q):
            src[base + j] = base + q + j
            sgn[base + j] = -1.0
            src[base + q + j] = base + j
            sgn[base + q + j] = 1.0
    return src, sgn


def _take_cols(w, src, sgn=None, axis=-1):
    src = np.asarray(src)
    sgn = np.ones(len(src), np.float32) if sgn is None else np.asarray(sgn, np.float32)
    axis = axis % w.ndim
    pieces = []
    lo = 0
    while lo < len(src):
        hi = lo + 1
        if src[lo] < 0:
            while hi < len(src) and src[hi] < 0:
                hi += 1
            shape = list(w.shape)
            shape[axis] = hi - lo
            pieces.append(jnp.zeros(shape, w.dtype))
        else:
            while hi < len(src) and src[hi] == src[hi - 1] + 1 and sgn[hi] == sgn[lo]:
                hi += 1
            piece = lax.slice_in_dim(w, int(src[lo]), int(src[lo]) + hi - lo, axis=axis)
            pieces.append(-piece if sgn[lo] < 0 else piece)
        lo = hi
    return jnp.concatenate(pieces, axis=axis)


def _pad_rows(w, n):
    pad = [(0, 0)] * w.ndim
    pad[-2] = (0, n - w.shape[-2])
    return jnp.pad(w, pad)


def _in_proj_layout():
    src = -np.ones(IN_PAD, np.int64)
    sgn = np.ones(IN_PAD, np.float32)
    src[COL_F:COL_F + F_WIDTH] = np.arange(F_WIDTH)
    o = F_WIDTH
    src[COL_CQ:COL_CQ + MLA_Q_LORA] = o + np.arange(MLA_Q_LORA)
    o += MLA_Q_LORA
    src[COL_CKV:COL_CKV + MLA_KV_LORA] = o + np.arange(MLA_KV_LORA)
    o += MLA_KV_LORA
    src[COL_KPE:COL_KPE + MLA_ROPE] = o + np.arange(MLA_ROPE)
    psrc, psgn = _rot_partner(MLA_ROPE)
    src[COL_KPE + MLA_ROPE:COL_KPE + 2 * MLA_ROPE] = o + psrc
    sgn[COL_KPE + MLA_ROPE:COL_KPE + 2 * MLA_ROPE] = psgn
    o += MLA_ROPE
    for h in range(GLA_HEADS):
        src[COL_GQ + h * GLA_QK_SLOT:COL_GQ + h * GLA_QK_SLOT + GLA_DK] = o + h * GLA_DK + np.arange(GLA_DK)
    o += GLA_QK_W
    for h in range(GLA_HEADS):
        src[COL_GK + h * GLA_QK_SLOT:COL_GK + h * GLA_QK_SLOT + GLA_DK] = o + h * GLA_DK + np.arange(GLA_DK)
    o += GLA_QK_W
    for h in range(GLA_HEADS):
        src[COL_GV + h * HEAD_SLOT:COL_GV + h * HEAD_SLOT + GLA_DV] = o + h * GLA_DV + np.arange(GLA_DV)
    o += GLA_WIDTH
    src[COL_LR:COL_LR + GLA_GATE_RANK] = o + np.arange(GLA_GATE_RANK)
    o += GLA_GATE_RANK
    for h in range(GLA_HEADS):
        src[COL_GG + h * HEAD_SLOT:COL_GG + h * HEAD_SLOT + GLA_DV] = o + h * GLA_DV + np.arange(GLA_DV)
    return src, sgn


def _head_slots(n_heads, d_src, d_take, src_off=0):
    src = -np.ones(n_heads * HEAD_SLOT, np.int64)
    for h in range(n_heads):
        src[h * HEAD_SLOT:h * HEAD_SLOT + d_take] = h * d_src + src_off + np.arange(d_take)
    return src


def _prep_weights(w):
    f32 = lambda a: a.astype(F32)
    out = {}
    src, sgn = _in_proj_layout()
    out["w_in"] = _take_cols(w["w_in"], src, sgn).astype(BF)

    m = np.arange(F_DIM)
    ang = 2.0 * np.pi * np.outer(m, m) / F_DIM
    c64, s64 = np.cos(ang), np.sin(ang)
    dft = np.zeros((F_WIDTH, 2 * F_WIDTH), np.float32)
    for g in range(F_GROUPS):
        dft[g * F_DIM:(g + 1) * F_DIM, g * F_DIM:(g + 1) * F_DIM] = c64
        dft[g * F_DIM:(g + 1) * F_DIM, F_WIDTH + g * F_DIM:F_WIDTH + (g + 1) * F_DIM] = s64
    out["dft64"] = jnp.asarray(dft, F32).astype(BF)

    psrc, psgn = _rot_partner(MLA_ROPE)
    q_src = _head_slots(MLA_HEADS, MLA_QK, MLA_QK)
    q_part = -np.ones(MLA_SLOTS, np.int64)
    q_psg = np.ones(MLA_SLOTS, np.float32)
    g_part = -np.ones(MLA_SLOTS, np.int64)
    for h in range(MLA_HEADS):
        lo = h * HEAD_SLOT + MLA_NOPE
        q_part[lo:lo + MLA_ROPE] = h * MLA_QK + MLA_NOPE + psrc
        q_psg[lo:lo + MLA_ROPE] = psgn
        g_part[lo:lo + MLA_ROPE] = MLA_NOPE + psrc
    g_src = np.where(q_src >= 0, q_src % MLA_QK, -1)
    wq = jnp.concatenate([_take_cols(w["mla_w_uq"], q_src), _take_cols(w["mla_w_uq"], q_part, q_psg)], axis=-1)
    out["w_uq"] = _pad_rows(wq, 256).astype(BF)
    out["q_lora_w"] = jnp.pad(f32(w["mla_q_lora_norm_w"]), ((0, 0), (0, 256 - MLA_Q_LORA)))[:, None, :]
    out["kv_lora_w"] = f32(w["mla_kv_lora_norm_w"])[:, None, :]
    out["q_gain"] = _take_cols(w["mla_q_norm_w"], g_src)[:, None, :]
    out["q_gain_p"] = _take_cols(w["mla_q_norm_w"], g_part)[:, None, :]
    out["k_gain"] = _take_cols(w["mla_k_norm_w"], g_src)[:, None, :]
    out["k_gain_p"] = _take_cols(w["mla_k_norm_w"], g_part)[:, None, :]

    kn_src = _head_slots(MLA_HEADS, MLA_NOPE + MLA_V, MLA_NOPE)
    v_src = _head_slots(MLA_HEADS, MLA_NOPE + MLA_V, MLA_V, MLA_NOPE)
    out["w_ukv"] = jnp.concatenate([_take_cols(w["mla_w_ukv"], kn_src), _take_cols(w["mla_w_ukv"], v_src)],
                                   axis=-1).astype(BF)

    e2 = np.zeros((LANE, 2 * MLA_SLOTS), np.float32)
    for h in range(MLA_HEADS):
        for j in range(MLA_ROPE):
            e2[j, h * HEAD_SLOT + MLA_NOPE + j] = 1.0
            e2[MLA_ROPE + j, MLA_SLOTS + h * HEAD_SLOT + MLA_NOPE + j] = 1.0
    out["kpe_place"] = jnp.asarray(e2, BF)

    gk_src = -np.ones(GLA_QK_SLOTS, np.int64)
    for h in range(GLA_HEADS):
        gk_src[h * GLA_QK_SLOT:h * GLA_QK_SLOT + GLA_DK] = h * GLA_DK + np.arange(GLA_DK)
    wgk = jnp.concatenate([_take_cols(w["gla_w_gk_fwd"], gk_src), _take_cols(w["gla_w_gk_bwd"], gk_src)], axis=-1)
    out["w_gk"] = _pad_rows(wgk, LANE).astype(BF)
    out["b_gk"] = jnp.concatenate([_take_cols(w["gla_b_gk_fwd"], gk_src), _take_cols(w["gla_b_gk_bwd"], gk_src)],
                                  axis=-1)[:, None, :]
    gv_src = _head_slots(GLA_HEADS, GLA_DV, GLA_DV)
    out["gla_gain"] = _take_cols(w["gla_norm_w"], np.where(gv_src >= 0, gv_src % GLA_DV, -1))[:, None, :]

    row_src = np.concatenate([np.arange(F_WIDTH + MLA_WIDTH), np.where(gv_src >= 0, F_WIDTH + MLA_WIDTH + gv_src, -1)])
    out["w_out"] = _take_cols(w["w_out"], row_src, axis=-2).astype(BF)

    nd = w["ffn_w_12"].shape[0]
    out["w12"] = w["ffn_w_12"].astype(BF)
    out["conv"] = jnp.concatenate([w["ffn_conv_w"], w["ffn_conv_b"][:, None, :]], axis=1).astype(F32)
    out["w_down"] = w["ffn_w_down"].astype(BF)
    out["norm1_w"] = f32(w["norm1_w"])[:, None, :]
    out["norm2_w"] = f32(w["norm2_w"])[:, None, :]
    return out


def _rope_tables(n_tokens):
    q = MLA_ROPE // 4
    t = np.arange(n_tokens)
    row = (t // GRID_W).astype(np.float32)
    col = (t % GRID_W).astype(np.float32)
    axis_dims = MLA_ROPE // 2
    inv_freq = np.power(np.float32(ROPE_BASE), -np.arange(0, axis_dims, 2, dtype=np.float32) / axis_dims)
    ang_r = row[:, None] * inv_freq
    ang_c = col[:, None] * inv_freq
    cos = np.ones((n_tokens, HEAD_SLOT), np.float32)
    sin = np.zeros((n_tokens, HEAD_SLOT), np.float32)
    for k, ang in enumerate((ang_r, ang_r, ang_c, ang_c)):
        lo = MLA_NOPE + k * q
        cos[:, lo:lo + q] = np.cos(ang)
        sin[:, lo:lo + q] = np.sin(ang)
    return jnp.asarray(cos), jnp.asarray(sin)


def _fft_tables(n1, n2):
    k = np.arange(n1)
    ang1 = 2.0 * np.pi * np.outer(k, k) / n1
    fr, fi = np.cos(ang1), -np.sin(ang1)
    w1 = np.block([[fr, fi], [fi, -fr]])
    length = n1 * n2
    kk = (np.arange(n1)[:, None] + n1 * np.arange(n2)[None, :]).astype(np.float64)
    t2 = np.arange(n2, dtype=np.float64)
    ang2 = 2.0 * np.pi * kk[:, :, None] * t2[None, None, :] / length
    norm = 1.0 / math.sqrt(length * F_DIM)
    tab = np.concatenate([np.cos(ang2), np.sin(ang2)], axis=-1) * norm
    return jnp.asarray(w1, F32).astype(BF), jnp.asarray(tab, F32).astype(BF)


def _ctx_dft_table(n):
    k = np.arange(n)
    ang = 2.0 * np.pi * np.outer(k, k) / n
    norm = 1.0 / math.sqrt(n * F_DIM)
    return jnp.asarray(np.concatenate([np.cos(ang), -np.sin(ang)], axis=1) * norm, F32).astype(BF)


def _mod_kernel(c_ref, w_ref, b_ref, o_ref):
    c = c_ref[...]
    s = c * (1.0 / (1.0 + jnp.exp(-c)))
    o_ref[...] = _dot(s.astype(BF), w_ref[...].astype(BF)) + b_ref[...]


def _adaln(cond8, mod_w, mod_b):
    nd, d, n = mod_w.shape
    tn = 1536
    return pl.pallas_call(
        _mod_kernel,
        out_shape=jax.ShapeDtypeStruct((nd, 8, n), F32),
        grid=(nd, n // tn),
        in_specs=[pl.BlockSpec((8, d), lambda l, j: (0, 0)),
                  pl.BlockSpec((None, d, tn), lambda l, j: (l, 0, j)),
                  pl.BlockSpec((None, 1, tn), lambda l, j: (l, 0, j))],
        out_specs=pl.BlockSpec((None, 8, tn), lambda l, j: (l, 0, j)),
        compiler_params=_cparams("arbitrary", "arbitrary"),
        name="adaln",
    )(cond8, mod_w, mod_b.reshape(nd, 1, n))


def _rms(x, n):
    return lax.rsqrt(jnp.sum(x * x, axis=-1, keepdims=True) * (1.0 / n) + EPS)


def _in_kernel(x_ref, sh_ref, sc_ref, n1_ref, win_ref, dft_ref, qlw_ref, wq_ref, kvlw_ref, wkv_ref, place_ref,
               qg_ref, qgp_ref, kg_ref, kgp_ref, cos_ref, sin_ref, wgk_ref, bgk_ref,
               fa_ref, fb_ref, q_ref, k_ref, v_ref, gq_ref, gk_ref, gv_ref, gg_ref, df_ref, db_ref,
               *, row, rope):
    for r0 in range(0, x_ref.shape[0], IN_SUB):
        rs = slice(r0, min(r0 + IN_SUB, x_ref.shape[0]))
        x = x_ref[rs, :]
        xn = x * _rms(x, D_MODEL) * n1_ref[...]
        h = xn * (1.0 + sc_ref[row:row + 1, :]) + sh_ref[row:row + 1, :]
        hb = h.astype(BF)
        p = _dot(hb, win_ref[:, :COL_GQ])

        ab = _dot(p[:, COL_F:COL_F + F_WIDTH].astype(BF), dft_ref[...])
        fa_ref[rs, :] = ab[:, :F_WIDTH].astype(BF)
        fb_ref[rs, :] = ab[:, F_WIDTH:].astype(BF)

        cq = p[:, COL_CQ:COL_CQ + 256]
        cqn = cq * _rms(cq, MLA_Q_LORA) * qlw_ref[...]
        qq = _dot(cqn.astype(BF), wq_ref[...])
        ckv = p[:, COL_CKV:COL_CKV + MLA_KV_LORA]
        ckvn = ckv * _rms(ckv, MLA_KV_LORA) * kvlw_ref[...]
        kk = _dot(ckvn.astype(BF), wkv_ref[...])
        kp = _dot(p[:, COL_KPE:COL_KPE + LANE].astype(BF), place_ref[...])
        pg = _dot(hb, win_ref[:, COL_GQ:])
        v_lane = lax.broadcasted_iota(jnp.int32, (x.shape[0], HEAD_SLOT), 1)

        q_scale = (MLA_QK ** -0.5) * LOG2E
        for hd in range(MLA_HEADS):
            sl = slice(hd * HEAD_SLOT, (hd + 1) * HEAD_SLOT)
            sp = slice(MLA_SLOTS + hd * HEAD_SLOT, MLA_SLOTS + (hd + 1) * HEAD_SLOT)
            qh = qq[:, sl]
            kh = kk[:, sl] + kp[:, sl]
            rq = _rms(qh, MLA_QK) * q_scale
            rk = _rms(kh, MLA_QK)
            if rope:
                cos = cos_ref[rs, :]
                sin = sin_ref[rs, :]
                qo = rq * (qh * (qg_ref[:, sl] * cos) + qq[:, sp] * (qgp_ref[:, sl] * sin))
                ko = rk * (kh * (kg_ref[:, sl] * cos) + kp[:, sp] * (kgp_ref[:, sl] * sin))
            else:
                qo = rq * (qh * qg_ref[:, sl])
                ko = rk * (kh * kg_ref[:, sl])
            q_ref[sl, rs] = qo.T.astype(BF)
            k_ref[rs, sl] = ko.astype(BF)
            vh = jnp.where(v_lane == MLA_V, 1.0, kk[:, MLA_SLOTS + hd * HEAD_SLOT:MLA_SLOTS + (hd + 1) * HEAD_SLOT])
            v_ref[hd * V_ROWS:(hd + 1) * V_ROWS, rs] = vh.T[:V_ROWS, :].astype(BF)

        gq_ref[rs, :] = (pg[:, 0:GLA_QK_SLOTS] * (GLA_DK ** -0.5)).astype(BF)
        gk_ref[rs, :] = pg[:, COL_GK - COL_GQ:COL_GK - COL_GQ + GLA_QK_SLOTS].astype(BF)
        gv_ref[rs, :] = pg[:, COL_GV - COL_GQ:COL_GV - COL_GQ + GLA_SLOTS].astype(BF)
        gg_ref[rs, :] = pg[:, COL_GG - COL_GQ:COL_GG - COL_GQ + GLA_SLOTS].astype(BF)
        z = _dot(pg[:, COL_LR - COL_GQ:COL_LR - COL_GQ + LANE].astype(BF), wgk_ref[...]) + bgk_ref[...]
        ls = (jnp.minimum(z, 0.0) - jnp.log(1.0 + jnp.exp(-jnp.abs(z)))) * (1.0 / GLA_GATE_NORM)
        df_ref[rs, :] = ls[:, :GLA_QK_SLOTS]
        db_ref[rs, :] = ls[:, GLA_QK_SLOTS:]


def _in_proj(x2, mods, pw, cos_t, sin_t, l, row, rope):
    rows = x2.shape[0]
    tm = min(2 * IN_SUB, rows)
    rt = lambda n: pl.BlockSpec((tm, n), lambda i: (i, 0))
    modspec = lambda k: pl.BlockSpec((None, 8, D_MODEL), lambda i: (l, 0, k))
    outs = [(F_WIDTH, BF), (F_WIDTH, BF), None, (MLA_SLOTS, BF), None,
            (GLA_QK_SLOTS, BF), (GLA_QK_SLOTS, BF), (GLA_SLOTS, BF), (GLA_SLOTS, BF), (GLA_QK_SLOTS, F32),
            (GLA_QK_SLOTS, F32)]
    shapes = [jax.ShapeDtypeStruct((rows, o[0]), o[1]) if o else None for o in outs]
    specs = [rt(o[0]) if o else None for o in outs]
    shapes[2] = jax.ShapeDtypeStruct((MLA_SLOTS, rows), BF)
    specs[2] = pl.BlockSpec((MLA_SLOTS, tm), lambda i: (0, i))
    shapes[4] = jax.ShapeDtypeStruct((rows // tm, MLA_HEADS * V_ROWS, tm), BF)
    specs[4] = pl.BlockSpec((None, MLA_HEADS * V_ROWS, tm), lambda i: (i, 0, 0))
    return pl.pallas_call(
        functools.partial(_in_kernel, row=row, rope=rope),
        out_shape=shapes,
        grid=(rows // tm,),
        in_specs=[rt(D_MODEL), modspec(0), modspec(1), _layer((1, D_MODEL), l),
                  _layer((D_MODEL, IN_PAD), l), _full((F_WIDTH, 2 * F_WIDTH)),
                  _layer((1, 256), l), _layer((256, 2 * MLA_SLOTS), l),
                  _layer((1, MLA_KV_LORA), l), _layer((MLA_KV_LORA, 2 * MLA_SLOTS), l),
                  _full((LANE, 2 * MLA_SLOTS)),
                  _layer((1, MLA_SLOTS), l), _layer((1, MLA_SLOTS), l), _layer((1, MLA_SLOTS), l),
                  _layer((1, MLA_SLOTS), l),
                  rt(HEAD_SLOT), rt(HEAD_SLOT),
                  _layer((LANE, 2 * GLA_QK_SLOTS), l), _layer((1, 2 * GLA_QK_SLOTS), l)],
        out_specs=specs,
        compiler_params=_cparams("arbitrary"),
        name="in_proj",
    )(x2, mods, mods, pw["norm1_w"], pw["w_in"], pw["dft64"], pw["q_lora_w"], pw["w_uq"], pw["kv_lora_w"],
      pw["w_ukv"], pw["kpe_place"], pw["q_gain"], pw["q_gain_p"], pw["k_gain"], pw["k_gain_p"],
      cos_t, sin_t, pw["w_gk"], pw["b_gk"])


def _fft1_kernel(a_ref, b_ref, w_ref, g_ref):
    n1 = a_ref.shape[0]
    ab = jnp.concatenate([a_ref[...], b_ref[...]], axis=0)
    g = _dot(w_ref[...], ab)
    g_ref[0] = g[:n1].astype(BF)
    g_ref[1] = g[n1:].astype(BF)


def _fft2_kernel(g_ref, t_ref, o_ref, *, batch):
    for j in range(batch):
        g = jnp.concatenate([g_ref[0, j], g_ref[1, j]], axis=0)
        o_ref[:, j * F_WIDTH:(j + 1) * F_WIDTH] = _dot(t_ref[j], g).astype(BF)


def _fourier_latent(fa, fb, w1, tab):
    length = fa.shape[0]
    n1, n2 = tab.shape[0], tab.shape[1]
    cols = n2 * F_WIDTH
    tn = min(4096, cols)
    g = pl.pallas_call(
        _fft1_kernel,
        out_shape=jax.ShapeDtypeStruct((2, n1, cols), BF),
        grid=(cols // tn,),
        in_specs=[pl.BlockSpec((n1, tn), lambda j: (0, j)), pl.BlockSpec((n1, tn), lambda j: (0, j)),
                  _full((2 * n1, 2 * n1))],
        out_specs=pl.BlockSpec((2, n1, tn), lambda j: (0, 0, j)),
        compiler_params=_cparams("arbitrary"),
        name="fft_stage1",
    )(fa.reshape(n1, cols), fb.reshape(n1, cols), w1)
    batch = 8
    y = pl.pallas_call(
        functools.partial(_fft2_kernel, batch=batch),
        out_shape=jax.ShapeDtypeStruct((n2, n1 * F_WIDTH), BF),
        grid=(n1 // batch,),
        in_specs=[pl.BlockSpec((2, batch, n2, F_WIDTH), lambda i: (0, i, 0, 0)),
                  pl.BlockSpec((batch, n2, 2 * n2), lambda i: (i, 0, 0))],
        out_specs=pl.BlockSpec((n2, batch * F_WIDTH), lambda i: (0, i)),
        compiler_params=_cparams("arbitrary"),
        name="fft_stage2",
    )(g.reshape(2, n1, n2, F_WIDTH), tab)
    return y.reshape(length, F_WIDTH)


def _fctx_kernel(a_ref, b_ref, t_ref, o_ref):
    ab = jnp.concatenate([a_ref[...], b_ref[...]], axis=0)
    o_ref[...] = _dot(t_ref[...], ab).astype(BF)


def _fourier_ctx(fa, fb, tab):
    n = fa.shape[0]
    return pl.pallas_call(
        _fctx_kernel,
        out_shape=jax.ShapeDtypeStruct((n, F_WIDTH), BF),
        grid=(1,),
        in_specs=[_full((n, F_WIDTH)), _full((n, F_WIDTH)), _full((n, 2 * n))],
        out_specs=_full((n, F_WIDTH)),
        compiler_params=_cparams("arbitrary"),
        name="fft_ctx",
    )(fa, fb, tab)


def _attn_kernel(*refs, tk, has_x):
    if has_x:
        q_ref, kc_ref, vc_ref, kx_ref, vx_ref, o_ref, acc_ref, s_ref = refs
    else:
        q_ref, kc_ref, vc_ref, o_ref, acc_ref = refs
    tq = q_ref.shape[1]
    heads = range(q_ref.shape[0] // HEAD_SLOT)
    slot = lambda h: slice(h * HEAD_SLOT, (h + 1) * HEAD_SLOT)
    vrow = lambda h: slice(h * V_ROWS, (h + 1) * V_ROWS)
    qt = [q_ref[slot(h), :] for h in heads]

    def scores(k2):
        return tuple(_dot(k2[:, slot(h)], qt[h]) for h in heads)

    def absorb(s, vts, m):
        sub = vts[0].shape[1]
        m_new = [jnp.maximum(m[h], jnp.max(s[h], axis=0, keepdims=True)) for h in heads]
        p = [jnp.exp2(s[h] - m_new[h]).astype(BF) for h in heads]
        for h in heads:
            alpha = jnp.exp2(m[h] - m_new[h])
            pv = _dot(vts[0][vrow(h), :], p[h][0:sub, :])
            for t in range(1, len(vts)):
                pv += _dot(vts[t][vrow(h), :], p[h][t * sub:(t + 1) * sub, :])
            acc_ref[h] = alpha * acc_ref[h] + pv
        return tuple(m_new)

    acc_ref[...] = jnp.zeros_like(acc_ref)
    m = tuple(jnp.full((1, tq), -1e30, F32) for _ in heads)
    m = absorb(scores(kc_ref[...]), [vc_ref[t] for t in range(vc_ref.shape[0])], m)
    if has_x:
        n = kx_ref.shape[0] // tk
        sub = vx_ref.shape[2]
        assert n % 2 == 0
        unroll = ATTN_UNROLL if n % ATTN_UNROLL == 0 else 2

        def put_scores(buf, j):
            off = pl.multiple_of(j * tk, tk)
            s = scores(kx_ref[pl.ds(off, tk), :])
            for h in heads:
                s_ref[buf, h] = s[h]

        def take(buf, j, u, m):
            if tk >= sub:
                vts = [vx_ref[j * (tk // sub) + t] for t in range(tk // sub)]
            else:
                r = sub // tk
                assert unroll % r == 0
                vts = [vx_ref[j // r, :, (u % r) * tk:(u % r + 1) * tk]]
            return absorb(tuple(s_ref[buf, h] for h in heads), vts, m)

        def body(i, m):
            for u in range(unroll):
                j = unroll * i + u
                nxt = j + 1 if u + 1 < unroll else jnp.minimum(j + 1, n - 1)
                put_scores((u + 1) % 2, nxt)
                m = take(u % 2, j, u, m)
            return m

        put_scores(0, 0)
        m = lax.fori_loop(0, n // unroll, body, m)
    for h in heads:
        acc = acc_ref[h]
        o_ref[h * MLA_V:(h + 1) * MLA_V, :] = (acc[:MLA_V, :] / acc[MLA_V:MLA_V + 1, :]).astype(BF)


def _attention(qt, kc, vct, kx=None, vxt=None):
    rows = qt.shape[1]
    lc = kc.shape[0]
    tq = min(512, rows)
    has_x = kx is not None
    pair = ATTN_HEADS * HEAD_SLOT
    vpair = ATTN_HEADS * V_ROWS
    in_specs = [pl.BlockSpec((pair, tq), lambda p, i: (p, i)),
                pl.BlockSpec((lc, pair), lambda p, i: (0, p)),
                pl.BlockSpec((vct.shape[0], vpair, vct.shape[2]), lambda p, i: (0, p, 0))]
    args = [qt, kc, vct]
    tk = 256
    scratch = [pltpu.VMEM((ATTN_HEADS, V_ROWS, tq), F32)]
    if has_x:
        lx = kx.shape[0]
        tk = min(tk, lx)
        in_specs += [pl.BlockSpec((lx, pair), lambda p, i: (0, p)),
                     pl.BlockSpec((vxt.shape[0], vpair, vxt.shape[2]), lambda p, i: (0, p, 0))]
        args += [kx, vxt]
        scratch.append(pltpu.VMEM((2, ATTN_HEADS, tk, tq), F32))
    return pl.pallas_call(
        functools.partial(_attn_kernel, tk=tk, has_x=has_x),
        out_shape=jax.ShapeDtypeStruct((MLA_WIDTH, rows), BF),
        grid=(MLA_HEADS // ATTN_HEADS, rows // tq),
        in_specs=in_specs,
        out_specs=pl.BlockSpec((ATTN_HEADS * MLA_V, tq), lambda p, i: (p, i)),
        scratch_shapes=scratch,
        compiler_params=_cparams("arbitrary", "arbitrary"),
        name="attention",
    )(*args)


def _gla_kernel(qf_ref, kf_ref, vf_ref, df_ref, qb_ref, kb_ref, vb_ref, db_ref, s0_ref,
                of_ref, ob_ref, sfin_ref, st_ref):
    i = pl.program_id(0)
    t = qf_ref.shape[0]
    c = GLA_CHUNK
    nc = t // c
    assert c & (c - 1) == 0

    @pl.when(i == 0)
    def _():
        st_ref[...] = s0_ref[...]

    r = lax.broadcasted_iota(jnp.int32, (t, t), 0)
    s = lax.broadcasted_iota(jnp.int32, (t, t), 1)
    same_chunk = jnp.bitwise_xor(r, s) < c
    dirs = ((qf_ref, kf_ref, vf_ref, df_ref, of_ref, same_chunk & (s <= r), range(nc), c - 1),
            (qb_ref, kb_ref, vb_ref, db_ref, ob_ref, same_chunk & (s >= r), range(nc - 1, -1, -1), 0))
    heads = range(GLA_HEADS)
    slot = lambda h: slice(h * HEAD_SLOT, (h + 1) * HEAD_SLOT)
    chunk = lambda ch: slice(ch * c, (ch + 1) * c)
    pair = lambda h: slice((h // 2) * HEAD_SLOT, (h // 2 + 1) * HEAD_SLOT)
    lane = lax.broadcasted_iota(jnp.int32, (1, HEAD_SLOT), 1)
    own = [(lane >= (h % 2) * GLA_QK_SLOT) & (lane < (h % 2 + 1) * GLA_QK_SLOT) for h in heads]

    b = []
    for q_ref, k_ref, v_ref, g_ref, o_ref, mask, order, last in dirs:
        tri = mask.astype(BF)
        g = g_ref[...]
        g_hi = g.astype(BF)
        g_r = g - g_hi.astype(F32)
        g_mid = g_r.astype(BF)
        g_lo = (g_r - g_mid.astype(F32)).astype(BF)
        b.append(_dot(tri, g_hi) + _dot(tri, g_mid) + _dot(tri, g_lo))

    work = []
    for d, (q_ref, k_ref, v_ref, g_ref, o_ref, mask, order, last) in enumerate(dirs):
        tot = [b[d][ch * c + last:ch * c + last + 1, :] for ch in range(nc)]
        b_tot = jnp.concatenate([jnp.broadcast_to(tot[ch], (c, GLA_QK_SLOTS)) for ch in range(nc)], axis=0)
        q_in = (q_ref[...].astype(F32) * jnp.exp(b[d])).astype(BF)
        kf = k_ref[...].astype(F32)
        k_in = (kf * jnp.exp(-b[d])).astype(BF)
        k_out = (kf * jnp.exp(b_tot - b[d])).astype(BF)
        v = v_ref[...]
        q_h = [jnp.where(own[h], q_in[:, pair(h)], jnp.zeros_like(q_in[:, pair(h)])) for h in heads]
        ko_h = [jnp.where(own[h], k_out[:, pair(h)], jnp.zeros_like(k_out[:, pair(h)])) for h in heads]
        a = [_dot_nt(q_h[h], k_in[:, pair(h)]) for h in heads]
        inc = [{ch: _dot_tn(v[chunk(ch), slot(h)], ko_h[h][chunk(ch), :]) for ch in order} for h in heads]
        work.append((tot, q_h, v, a, inc))

    o_intra = []
    for d, (q_ref, k_ref, v_ref, g_ref, o_ref, mask, order, last) in enumerate(dirs):
        tot, q_h, v, a, inc = work[d]
        o_intra.append([_dot(jnp.where(mask, a[h], 0.0).astype(BF), v[:, slot(h)]) for h in heads])

    for d, (q_ref, k_ref, v_ref, g_ref, o_ref, mask, order, last) in enumerate(dirs):
        tot, q_h, v, a, inc = work[d]
        entering = []
        for h in heads:
            st = st_ref[d, h]
            ent = {}
            for ch in order:
                ent[ch] = st.astype(BF)
                st = st * jnp.exp(tot[ch][:, pair(h)]) + inc[h][ch]
            st_ref[d, h] = st
            entering.append(ent)
        for h in heads:
            for ch in order:
                o_ref[chunk(ch), slot(h)] = (o_intra[d][h][chunk(ch), :]
                                             + _dot_nt(q_h[h][chunk(ch), :], entering[h][ch])).astype(BF)

    @pl.when(i == pl.num_programs(0) - 1)
    def _():
        sfin_ref[...] = st_ref[...]


def _gla(gq, gk, gv, df, db, s0):
    rows = gq.shape[0]
    c = min(GLA_STEP, rows)
    n = rows // c
    fwd = pl.BlockSpec((c, GLA_SLOTS), lambda i: (i, 0))
    bwd = pl.BlockSpec((c, GLA_SLOTS), lambda i: (n - 1 - i, 0))
    fwd_qk = pl.BlockSpec((c, GLA_QK_SLOTS), lambda i: (i, 0))
    bwd_qk = pl.BlockSpec((c, GLA_QK_SLOTS), lambda i: (n - 1 - i, 0))
    st_shape = (2, GLA_HEADS, HEAD_SLOT, HEAD_SLOT)
    return pl.pallas_call(
        _gla_kernel,
        out_shape=[jax.ShapeDtypeStruct((rows, GLA_SLOTS), BF), jax.ShapeDtypeStruct((rows, GLA_SLOTS), BF),
                   jax.ShapeDtypeStruct(st_shape, F32)],
        grid=(n,),
        in_specs=[fwd_qk, fwd_qk, fwd, fwd_qk, bwd_qk, bwd_qk, bwd, bwd_qk, _full(st_shape)],
        out_specs=[fwd, bwd, _full(st_shape)],
        scratch_shapes=[pltpu.VMEM(st_shape, F32)],
        compiler_params=_cparams("arbitrary"),
        name="gla_scan",
    )(gq, gk, gv, df, gq, gk, gv, db, s0)


def _out_kernel(x_ref, four_ref, att_ref, of_ref, ob_ref, gg_ref, gain_ref, wout_ref, g1_ref, n2_ref,
                sh_ref, sc_ref, xo_ref, h_ref, *, row):
    o = of_ref[...].astype(F32) + ob_ref[...].astype(F32)
    g = gg_ref[...].astype(F32)
    gate = g * (1.0 / (1.0 + jnp.exp(-g)))
    y = _dot(four_ref[...], wout_ref[0:F_WIDTH, :])
    y += _dot_tn(att_ref[...], wout_ref[F_WIDTH:F_WIDTH + MLA_WIDTH, :])
    for h in range(GLA_HEADS):
        sl = slice(h * HEAD_SLOT, (h + 1) * HEAD_SLOT)
        oh = o[:, sl]
        lin = oh * _rms(oh, GLA_DV) * gain_ref[:, sl] * gate[:, sl]
        lo = F_WIDTH + MLA_WIDTH + h * HEAD_SLOT
        y += _dot(lin.astype(BF), wout_ref[lo:lo + HEAD_SLOT, :])
    x = x_ref[...] + g1_ref[row:row + 1, :] * y
    xo_ref[...] = x
    hn = x * _rms(x, D_MODEL) * n2_ref[...]
    h_ref[...] = (hn * (1.0 + sc_ref[row:row + 1, :]) + sh_ref[row:row + 1, :]).astype(BF)


def _out_proj(x2, four, att, o_f, o_b, gg, mods, pw, l, row):
    rows = x2.shape[0]
    tm = min(ROW_TILE, rows)
    rt = lambda n: pl.BlockSpec((tm, n), lambda i: (i, 0))
    modspec = lambda k: pl.BlockSpec((None, 8, D_MODEL), lambda i: (l, 0, k))
    return pl.pallas_call(
        functools.partial(_out_kernel, row=row),
        out_shape=[jax.ShapeDtypeStruct((rows, D_MODEL), F32), jax.ShapeDtypeStruct((rows, D_MODEL), BF)],
        grid=(rows // tm,),
        in_specs=[rt(D_MODEL), rt(F_WIDTH), pl.BlockSpec((MLA_WIDTH, tm), lambda i: (0, i)),
                  rt(GLA_SLOTS), rt(GLA_SLOTS), rt(GLA_SLOTS),
                  _layer((1, GLA_SLOTS), l), _layer((MIX_PAD, D_MODEL), l), modspec(2),
                  _layer((1, D_MODEL), l), modspec(3), modspec(4)],
        out_specs=[rt(D_MODEL), rt(D_MODEL)],
        compiler_params=_cparams("arbitrary"),
        name="out_proj",
    )(x2, four, att, o_f, o_b, gg, pw["gla_gain"], pw["w_out"], mods, pw["norm2_w"], mods, mods)


def _ffn_kernel(x_ref, h_ref, hp_ref, hn_ref, w12_ref, conv_ref, wd_ref, g2_ref, o_ref, acc_ref, hx_ref, u_ref,
                *, row):
    i = pl.program_id(0)
    tm = h_ref.shape[0]
    m = tm + 2 * FF_HALO
    keep_prev = jnp.where(i > 0, 1.0, 0.0)
    keep_next = jnp.where(i < pl.num_programs(0) - 1, 1.0, 0.0)
    hx_ref[0:FF_HALO, :] = (hp_ref[...].astype(F32) * keep_prev).astype(BF)
    hx_ref[FF_HALO:FF_HALO + tm, :] = h_ref[...]
    hx_ref[FF_HALO + tm:m, :] = (hn_ref[...].astype(F32) * keep_next).astype(BF)
    acc_ref[...] = jnp.zeros_like(acc_ref)

    def cols(c, half):
        lo = half * D_FF + c * FF_CHUNK
        return slice(lo, lo + FF_CHUNK)

    def up(buf, c):
        for half in range(2):
            u_ref[buf, half] = _dot(hx_ref[...], w12_ref[:, cols(c, half)])

    def conv(buf, c, half):
        u = u_ref[buf, half]
        cw = conv_ref[:, cols(c, half)]
        u_prev = pltpu.roll(u, 1, axis=0)
        u_next = pltpu.roll(u, m - 1, axis=0)
        uc = u_prev * cw[0:1, :] + u * cw[1:2, :] + u_next * cw[2:3, :] + cw[3:4, :]
        return uc[FF_HALO:FF_HALO + tm, :]

    def down(buf, c):
        a = conv(buf, c, 0)
        act = a * (1.0 / (1.0 + jnp.exp(-a))) * conv(buf, c, 1)
        acc_ref[...] += _dot(act.astype(BF), wd_ref[c * FF_CHUNK:(c + 1) * FF_CHUNK, :])

    up(0, 0)
    for c in range(FF_NCHUNK):
        if c + 1 < FF_NCHUNK:
            up((c + 1) % 2, c + 1)
        down(c % 2, c)
    o_ref[...] = x_ref[...] + g2_ref[row:row + 1, :] * acc_ref[...]


def _ffn(x2, h2, mods, pw, l, row):
    rows = x2.shape[0]
    tm = min(512, rows)
    nt = rows // tm
    per = tm // FF_HALO
    last_blk = rows // FF_HALO - 1
    m = tm + 2 * FF_HALO
    rt = lambda n: pl.BlockSpec((tm, n), lambda i: (i, 0))
    once = lambda shape: pl.BlockSpec((None,) + shape, lambda i: (l,) + (0,) * len(shape),
                                      pipeline_mode=pl.Buffered(1))
    return pl.pallas_call(
        functools.partial(_ffn_kernel, row=row),
        out_shape=jax.ShapeDtypeStruct((rows, D_MODEL), F32),
        grid=(nt,),
        in_specs=[rt(D_MODEL), rt(D_MODEL),
                  pl.BlockSpec((FF_HALO, D_MODEL), lambda i: (jnp.maximum(i * per - 1, 0), 0)),
                  pl.BlockSpec((FF_HALO, D_MODEL), lambda i: (jnp.minimum((i + 1) * per, last_blk), 0)),
                  once((D_MODEL, 2 * D_FF)), once((4, 2 * D_FF)), once((D_FF, D_MODEL)),
                  pl.BlockSpec((None, 8, D_MODEL), lambda i: (l, 0, 5))],
        out_specs=rt(D_MODEL),
        scratch_shapes=[pltpu.VMEM((tm, D_MODEL), F32), pltpu.VMEM((m, D_MODEL), BF),
                        pltpu.VMEM((2, 2, m, FF_CHUNK), F32)],
        compiler_params=_cparams("arbitrary"),
        name="conv_ffn",
    )(x2, h2, h2, h2, pw["w12"], pw["conv"], pw["w_down"], mods)


def kernel(x, c, ctx, c_ctx, mod_w, mod_b, norm1_w, norm2_w, w_in, mla_q_lora_norm_w, mla_w_uq, mla_kv_lora_norm_w, mla_w_ukv, mla_q_norm_w, mla_k_norm_w, gla_w_gk_fwd, gla_b_gk_fwd, gla_w_gk_bwd, gla_b_gk_bwd, gla_norm_w, w_out, ffn_w_12, ffn_conv_w, ffn_conv_b, ffn_w_down):
    batch, seq, d = x.shape
    assert batch == 1 and d == D_MODEL
    lc = ctx.shape[1]
    depth = mod_w.shape[0]
    n1 = int(round(math.sqrt(seq)))
    assert n1 * n1 == seq and seq % ROW_TILE == 0 and lc % GLA_CHUNK == 0

    pw = _prep_weights(dict(
        w_in=w_in, mla_w_uq=mla_w_uq, mla_q_lora_norm_w=mla_q_lora_norm_w, mla_kv_lora_norm_w=mla_kv_lora_norm_w,
        mla_w_ukv=mla_w_ukv, mla_q_norm_w=mla_q_norm_w, mla_k_norm_w=mla_k_norm_w, gla_w_gk_fwd=gla_w_gk_fwd,
        gla_b_gk_fwd=gla_b_gk_fwd, gla_w_gk_bwd=gla_w_gk_bwd, gla_b_gk_bwd=gla_b_gk_bwd, gla_norm_w=gla_norm_w,
        w_out=w_out, ffn_w_12=ffn_w_12, ffn_conv_w=ffn_conv_w, ffn_conv_b=ffn_conv_b, ffn_w_down=ffn_w_down,
        norm1_w=norm1_w, norm2_w=norm2_w))
    cos_t, sin_t = _rope_tables(seq)
    ones_c = jnp.ones((lc, HEAD_SLOT), F32)
    w1, tab = _fft_tables(n1, n1)
    tab_c = _ctx_dft_table(lc)

    cond8 = jnp.zeros((8, d), F32).at[0].set(c[0].astype(F32)).at[1].set(c_ctx.astype(F32))
    mods = _adaln(cond8, mod_w, mod_b)

    xs = x[0].astype(F32)
    xc = ctx[0].astype(F32)
    s_zero = jnp.zeros((2, GLA_HEADS, HEAD_SLOT, HEAD_SLOT), F32)
    for l in range(depth):
        last = l == depth - 1
        fa_c, fb_c, q_c, k_c, v_c, gq_c, gk_c, gv_c, gg_c, df_c, db_c = _in_proj(xc, mods, pw, ones_c, ones_c, l, 1, False)
        fa_x, fb_x, q_x, k_x, v_x, gq_x, gk_x, gv_x, gg_x, df_x, db_x = _in_proj(xs, mods, pw, cos_t, sin_t, l, 0, True)

        of_c, ob_c, s_c = _gla(gq_c, gk_c, gv_c, df_c, db_c, s_zero)
        of_x, ob_x, _ = _gla(gq_x, gk_x, gv_x, df_x, db_x, s_c)
        four_x = _fourier_latent(fa_x, fb_x, w1, tab)
        att_x = _attention(q_x, k_c, v_c, k_x, v_x)
        x_mid, h2 = _out_proj(xs, four_x, att_x, of_x, ob_x, gg_x, mods, pw, l, 0)
        xs = _ffn(x_mid, h2, mods, pw, l, 0)
        if not last:
            four_c = _fourier_ctx(fa_c, fb_c, tab_c)
            att_c = _attention(q_c, k_c, v_c)
            c_mid, hc2 = _out_proj(xc, four_c, att_c, of_c, ob_c, gg_c, mods, pw, l, 1)
            xc = _ffn(c_mid, hc2, mods, pw, l, 1)
    return xs[None].astype(x.dtype)
```

```python
import functools
import math

import numpy as np
import jax
import jax.numpy as jnp
from jax import lax
from jax.experimental import pallas as pl
from jax.experimental.pallas import tpu as pltpu

D_MODEL = 1024
DEPTH = 2
GRID_W = 64
EPS = 1e-6

F_GROUPS = 4
F_DIM = 64
F_WIDTH = F_GROUPS * F_DIM

MLA_HEADS = 6
MLA_Q_LORA = 192
MLA_KV_LORA = 128
MLA_NOPE = 64
MLA_ROPE = 32
MLA_V = 64
MLA_QK = MLA_NOPE + MLA_ROPE
MLA_WIDTH = MLA_HEADS * MLA_V
MLA_IN = MLA_Q_LORA + MLA_KV_LORA + MLA_ROPE
ROPE_BASE = 10000.0

GLA_HEADS = 4
GLA_DK = 48
GLA_DV = 96
GLA_GATE_RANK = 16
GLA_GATE_NORM = 16.0
GLA_CHUNK = 64
GLA_STEP = 256
GLA_WIDTH = GLA_HEADS * GLA_DV
GLA_QK_W = GLA_HEADS * GLA_DK

MIX_WIDTH = F_WIDTH + MLA_WIDTH + GLA_WIDTH
D_FF = 2816
N_MOD = 6

LANE = 128
SUBLANE_BF16 = 16
VMEM_LIMIT = 48 * 1024 * 1024

HEAD_SLOT = LANE
V_ROWS = 80
MLA_SLOTS = MLA_HEADS * HEAD_SLOT
GLA_SLOTS = GLA_HEADS * HEAD_SLOT
GLA_QK_SLOT = 64
GLA_QK_SLOTS = GLA_HEADS * GLA_QK_SLOT

COL_F = 0
COL_CQ = COL_F + F_WIDTH
COL_CKV = COL_CQ + 256
COL_KPE = COL_CKV + MLA_KV_LORA
COL_GQ = COL_KPE + LANE
COL_GK = COL_GQ + GLA_QK_SLOTS
COL_GV = COL_GK + GLA_QK_SLOTS
COL_GG = COL_GV + GLA_SLOTS
COL_LR = COL_GG + GLA_SLOTS
IN_PAD = COL_LR + LANE

MIX_PAD = F_WIDTH + MLA_WIDTH + GLA_SLOTS

FF_CHUNKS = (768, 768, 768, 512)
assert sum(FF_CHUNKS) == D_FF
FF_HALO = SUBLANE_BF16
FF_AHEAD = 1

LOG2E = 1.4426950408889634

ROW_TILE = 256
IN_SUB = 256
ATTN_UNROLL = 32
ATTN_HEADS = 2
BF = jnp.bfloat16
F32 = jnp.float32


def _cparams(*sem):
    return pltpu.CompilerParams(dimension_semantics=sem, vmem_limit_bytes=VMEM_LIMIT)


def _dot(a, b):
    return jnp.dot(a, b, preferred_element_type=F32)


def _dot_nt(a, b):
    return lax.dot_general(a, b, (((1,), (1,)), ((), ())), preferred_element_type=F32)


def _dot_tn(a, b):
    return lax.dot_general(a, b, (((0,), (0,)), ((), ())), preferred_element_type=F32)


def _full(shape):
    n = len(shape)
    return pl.BlockSpec(shape, lambda *_: (0,) * n)


def _layer(shape, l):
    n = len(shape)
    return pl.BlockSpec((None,) + tuple(shape), lambda *_: (l,) + (0,) * n)


def _rot_partner(n_rope):
    q = n_rope // 4
    src = np.zeros(n_rope, np.int64)
    sgn = np.zeros(n_rope, np.float32)
    for base in (0, 2 * q):
        for j in range(q):
            src[base + j] = base + q + j
            sgn[base + j] = -1.0
            src[base + q + j] = base + j
            sgn[base + q + j] = 1.0
    return src, sgn


def _take_cols(w, src, sgn=None, axis=-1):
    src = np.asarray(src)
    sgn = np.ones(len(src), np.float32) if sgn is None else np.asarray(sgn, np.float32)
    axis = axis % w.ndim
    pieces = []
    lo = 0
    while lo < len(src):
        hi = lo + 1
        if src[lo] < 0:
            while hi < len(src) and src[hi] < 0:
                hi += 1
            shape = list(w.shape)
            shape[axis] = hi - lo
            pieces.append(jnp.zeros(shape, w.dtype))
        else:
            while hi < len(src) and src[hi] == src[hi - 1] + 1 and sgn[hi] == sgn[lo]:
                hi += 1
            piece = lax.slice_in_dim(w, int(src[lo]), int(src[lo]) + hi - lo, axis=axis)
            pieces.append(-piece if sgn[lo] < 0 else piece)
        lo = hi
    return jnp.concatenate(pieces, axis=axis)


def _pad_rows(w, n):
    pad = [(0, 0)] * w.ndim
    pad[-2] = (0, n - w.shape[-2])
    return jnp.pad(w, pad)


def _in_proj_layout():
    src = -np.ones(IN_PAD, np.int64)
    sgn = np.ones(IN_PAD, np.float32)
    src[COL_F:COL_F + F_WIDTH] = np.arange(F_WIDTH)
    o = F_WIDTH
    src[COL_CQ:COL_CQ + MLA_Q_LORA] = o + np.arange(MLA_Q_LORA)
    o += MLA_Q_LORA
    src[COL_CKV:COL_CKV + MLA_KV_LORA] = o + np.arange(MLA_KV_LORA)
    o += MLA_KV_LORA
    src[COL_KPE:COL_KPE + MLA_ROPE] = o + np.arange(MLA_ROPE)
    psrc, psgn = _rot_partner(MLA_ROPE)
    src[COL_KPE + MLA_ROPE:COL_KPE + 2 * MLA_ROPE] = o + psrc
    sgn[COL_KPE + MLA_ROPE:COL_KPE + 2 * MLA_ROPE] = psgn
    o += MLA_ROPE
    for h in range(GLA_HEADS):
        src[COL_GQ + h * GLA_QK_SLOT:COL_GQ + h * GLA_QK_SLOT + GLA_DK] = o + h * GLA_DK + np.arange(GLA_DK)
    o += GLA_QK_W
    for h in range(GLA_HEADS):
        src[COL_GK + h * GLA_QK_SLOT:COL_GK + h * GLA_QK_SLOT + GLA_DK] = o + h * GLA_DK + np.arange(GLA_DK)
    o += GLA_QK_W
    for h in range(GLA_HEADS):
        src[COL_GV + h * HEAD_SLOT:COL_GV + h * HEAD_SLOT + GLA_DV] = o + h * GLA_DV + np.arange(GLA_DV)
    o += GLA_WIDTH
    src[COL_LR:COL_LR + GLA_GATE_RANK] = o + np.arange(GLA_GATE_RANK)
    o += GLA_GATE_RANK
    for h in range(GLA_HEADS):
        src[COL_GG + h * HEAD_SLOT:COL_GG + h * HEAD_SLOT + GLA_DV] = o + h * GLA_DV + np.arange(GLA_DV)
    return src, sgn


def _head_slots(n_heads, d_src, d_take, src_off=0):
    src = -np.ones(n_heads * HEAD_SLOT, np.int64)
    for h in range(n_heads):
        src[h * HEAD_SLOT:h * HEAD_SLOT + d_take] = h * d_src + src_off + np.arange(d_take)
    return src


def _prep_weights(w):
    f32 = lambda a: a.astype(F32)
    out = {}
    src, sgn = _in_proj_layout()
    out["w_in"] = _take_cols(w["w_in"], src, sgn).astype(BF)

    m = np.arange(F_DIM)
    ang = 2.0 * np.pi * np.outer(m, m) / F_DIM
    c64, s64 = np.cos(ang), np.sin(ang)
    dft = np.zeros((F_WIDTH, 2 * F_WIDTH), np.float32)
    for g in range(F_GROUPS):
        dft[g * F_DIM:(g + 1) * F_DIM, g * F_DIM:(g + 1) * F_DIM] = c64
        dft[g * F_DIM:(g + 1) * F_DIM, F_WIDTH + g * F_DIM:F_WIDTH + (g + 1) * F_DIM] = s64
    out["dft64"] = jnp.asarray(dft, F32).astype(BF)

    psrc, psgn = _rot_partner(MLA_ROPE)
    q_src = _head_slots(MLA_HEADS, MLA_QK, MLA_QK)
    q_part = -np.ones(MLA_SLOTS, np.int64)
    q_psg = np.ones(MLA_SLOTS, np.float32)
    g_part = -np.ones(MLA_SLOTS, np.int64)
    for h in range(MLA_HEADS):
        lo = h * HEAD_SLOT + MLA_NOPE
        q_part[lo:lo + MLA_ROPE] = h * MLA_QK + MLA_NOPE + psrc
        q_psg[lo:lo + MLA_ROPE] = psgn
        g_part[lo:lo + MLA_ROPE] = MLA_NOPE + psrc
    g_src = np.where(q_src >= 0, q_src % MLA_QK, -1)
    wq = jnp.concatenate([_take_cols(w["mla_w_uq"], q_src), _take_cols(w["mla_w_uq"], q_part, q_psg)], axis=-1)
    out["w_uq"] = _pad_rows(wq, 256).astype(BF)
    out["q_lora_w"] = jnp.pad(f32(w["mla_q_lora_norm_w"]), ((0, 0), (0, 256 - MLA_Q_LORA)))[:, None, :]
    out["kv_lora_w"] = f32(w["mla_kv_lora_norm_w"])[:, None, :]
    out["q_gain"] = _take_cols(w["mla_q_norm_w"], g_src)[:, None, :]
    out["q_gain_p"] = _take_cols(w["mla_q_norm_w"], g_part)[:, None, :]
    out["k_gain"] = _take_cols(w["mla_k_norm_w"], g_src)[:, None, :]
    out["k_gain_p"] = _take_cols(w["mla_k_norm_w"], g_part)[:, None, :]

    kn_src = _head_slots(MLA_HEADS, MLA_NOPE + MLA_V, MLA_NOPE)
    v_src = _head_slots(MLA_HEADS, MLA_NOPE + MLA_V, MLA_V, MLA_NOPE)
    out["w_ukv"] = jnp.concatenate([_take_cols(w["mla_w_ukv"], kn_src), _take_cols(w["mla_w_ukv"], v_src)],
                                   axis=-1).astype(BF)

    e2 = np.zeros((LANE, 2 * MLA_SLOTS), np.float32)
    for h in range(MLA_HEADS):
        for j in range(MLA_ROPE):
            e2[j, h * HEAD_SLOT + MLA_NOPE + j] = 1.0
            e2[MLA_ROPE + j, MLA_SLOTS + h * HEAD_SLOT + MLA_NOPE + j] = 1.0
    out["kpe_place"] = jnp.asarray(e2, BF)

    gk_src = -np.ones(GLA_QK_SLOTS, np.int64)
    for h in range(GLA_HEADS):
        gk_src[h * GLA_QK_SLOT:h * GLA_QK_SLOT + GLA_DK] = h * GLA_DK + np.arange(GLA_DK)
    wgk = jnp.concatenate([_take_cols(w["gla_w_gk_fwd"], gk_src), _take_cols(w["gla_w_gk_bwd"], gk_src)], axis=-1)
    out["w_gk"] = _pad_rows(wgk, LANE).astype(BF)
    out["b_gk"] = jnp.concatenate([_take_cols(w["gla_b_gk_fwd"], gk_src), _take_cols(w["gla_b_gk_bwd"], gk_src)],
                                  axis=-1)[:, None, :]
    gv_src = _head_slots(GLA_HEADS, GLA_DV, GLA_DV)
    out["gla_gain"] = _take_cols(w["gla_norm_w"], np.where(gv_src >= 0, gv_src % GLA_DV, -1))[:, None, :]

    row_src = np.concatenate([np.arange(F_WIDTH + MLA_WIDTH), np.where(gv_src >= 0, F_WIDTH + MLA_WIDTH + gv_src, -1)])
    out["w_out"] = _take_cols(w["w_out"], row_src, axis=-2).astype(BF)

    nd = w["ffn_w_12"].shape[0]
    out["w12"] = w["ffn_w_12"].astype(BF)
    out["conv"] = jnp.concatenate([w["ffn_conv_w"], w["ffn_conv_b"][:, None, :]], axis=1).astype(F32)
    out["w_down"] = w["ffn_w_down"].astype(BF)
    out["norm1_w"] = f32(w["norm1_w"])[:, None, :]
    out["norm2_w"] = f32(w["norm2_w"])[:, None, :]
    return out


def _rope_tables(n_tokens):
    q = MLA_ROPE // 4
    t = np.arange(n_tokens)
    row = (t // GRID_W).astype(np.float32)
    col = (t % GRID_W).astype(np.float32)
    axis_dims = MLA_ROPE // 2
    inv_freq = np.power(np.float32(ROPE_BASE), -np.arange(0, axis_dims, 2, dtype=np.float32) / axis_dims)
    ang_r = row[:, None] * inv_freq
    ang_c = col[:, None] * inv_freq
    cos = np.ones((n_tokens, HEAD_SLOT), np.float32)
    sin = np.zeros((n_tokens, HEAD_SLOT), np.float32)
    for k, ang in enumerate((ang_r, ang_r, ang_c, ang_c)):
        lo = MLA_NOPE + k * q
        cos[:, lo:lo + q] = np.cos(ang)
        sin[:, lo:lo + q] = np.sin(ang)
    return jnp.asarray(cos), jnp.asarray(sin)


def _fft_tables(n1, n2):
    k = np.arange(n1)
    ang1 = 2.0 * np.pi * np.outer(k, k) / n1
    fr, fi = np.cos(ang1), -np.sin(ang1)
    w1 = np.block([[fr, fi], [fi, -fr]])
    length = n1 * n2
    kk = (np.arange(n1)[:, None] + n1 * np.arange(n2)[None, :]).astype(np.float64)
    t2 = np.arange(n2, dtype=np.float64)
    ang2 = 2.0 * np.pi * kk[:, :, None] * t2[None, None, :] / length
    norm = 1.0 / math.sqrt(length * F_DIM)
    tab = np.concatenate([np.cos(ang2), np.sin(ang2)], axis=-1) * norm
    return jnp.asarray(w1, F32).astype(BF), jnp.asarray(tab, F32).astype(BF)


def _ctx_dft_table(n):
    k = np.arange(n)
    ang = 2.0 * np.pi * np.outer(k, k) / n
    norm = 1.0 / math.sqrt(n * F_DIM)
    return jnp.asarray(np.concatenate([np.cos(ang), -np.sin(ang)], axis=1) * norm, F32).astype(BF)


def _mod_kernel(c_ref, w_ref, b_ref, o_ref):
    c = c_ref[...]
    s = c * (1.0 / (1.0 + jnp.exp(-c)))
    o_ref[...] = _dot(s.astype(BF), w_ref[...].astype(BF)) + b_ref[...]


def _adaln(cond8, mod_w, mod_b):
    nd, d, n = mod_w.shape
    tn = 1536
    return pl.pallas_call(
        _mod_kernel,
        out_shape=jax.ShapeDtypeStruct((nd, 8, n), F32),
        grid=(nd, n // tn),
        in_specs=[pl.BlockSpec((8, d), lambda l, j: (0, 0)),
                  pl.BlockSpec((None, d, tn), lambda l, j: (l, 0, j)),
                  pl.BlockSpec((None, 1, tn), lambda l, j: (l, 0, j))],
        out_specs=pl.BlockSpec((None, 8, tn), lambda l, j: (l, 0, j)),
        compiler_params=_cparams("arbitrary", "arbitrary"),
        name="adaln",
    )(cond8, mod_w, mod_b.reshape(nd, 1, n))


def _rms(x, n):
    return lax.rsqrt(jnp.sum(x * x, axis=-1, keepdims=True) * (1.0 / n) + EPS)


def _in_kernel(x_ref, sh_ref, sc_ref, n1_ref, win_ref, dft_ref, qlw_ref, wq_ref, kvlw_ref, wkv_ref, place_ref,
               qg_ref, qgp_ref, kg_ref, kgp_ref, cos_ref, sin_ref, wgk_ref, bgk_ref,
               fa_ref, fb_ref, q_ref, k_ref, v_ref, gq_ref, gk_ref, gv_ref, gg_ref, df_ref, db_ref,
               *, row, rope):
    for r0 in range(0, x_ref.shape[0], IN_SUB):
        rs = slice(r0, min(r0 + IN_SUB, x_ref.shape[0]))
        x = x_ref[rs, :]
        xn = x * _rms(x, D_MODEL) * n1_ref[...]
        h = xn * (1.0 + sc_ref[row:row + 1, :]) + sh_ref[row:row + 1, :]
        hb = h.astype(BF)
        p = _dot(hb, win_ref[:, :COL_GQ])

        ab = _dot(p[:, COL_F:COL_F + F_WIDTH].astype(BF), dft_ref[...])
        fa_ref[rs, :] = ab[:, :F_WIDTH].astype(BF)
        fb_ref[rs, :] = ab[:, F_WIDTH:].astype(BF)

        cq = p[:, COL_CQ:COL_CQ + 256]
        cqn = cq * _rms(cq, MLA_Q_LORA) * qlw_ref[...]
        qq = _dot(cqn.astype(BF), wq_ref[...])
        ckv = p[:, COL_CKV:COL_CKV + MLA_KV_LORA]
        ckvn = ckv * _rms(ckv, MLA_KV_LORA) * kvlw_ref[...]
        kk = _dot(ckvn.astype(BF), wkv_ref[...])
        kp = _dot(p[:, COL_KPE:COL_KPE + LANE].astype(BF), place_ref[...])
        pg = _dot(hb, win_ref[:, COL_GQ:])
        v_lane = lax.broadcasted_iota(jnp.int32, (x.shape[0], HEAD_SLOT), 1)

        q_scale = (MLA_QK ** -0.5) * LOG2E
        for hd in range(MLA_HEADS):
            sl = slice(hd * HEAD_SLOT, (hd + 1) * HEAD_SLOT)
            sp = slice(MLA_SLOTS + hd * HEAD_SLOT, MLA_SLOTS + (hd + 1) * HEAD_SLOT)
            qh = qq[:, sl]
            kh = kk[:, sl] + kp[:, sl]
            rq = _rms(qh, MLA_QK) * q_scale
            rk = _rms(kh, MLA_QK)
            if rope:
                cos = cos_ref[rs, :]
                sin = sin_ref[rs, :]
                qo = rq * (qh * (qg_ref[:, sl] * cos) + qq[:, sp] * (qgp_ref[:, sl] * sin))
                ko = rk * (kh * (kg_ref[:, sl] * cos) + kp[:, sp] * (kgp_ref[:, sl] * sin))
            else:
                qo = rq * (qh * qg_ref[:, sl])
                ko = rk * (kh * kg_ref[:, sl])
            q_ref[sl, rs] = qo.T.astype(BF)
            k_ref[rs, sl] = ko.astype(BF)
            vh = jnp.where(v_lane == MLA_V, 1.0, kk[:, MLA_SLOTS + hd * HEAD_SLOT:MLA_SLOTS + (hd + 1) * HEAD_SLOT])
            v_ref[hd * V_ROWS:(hd + 1) * V_ROWS, rs] = vh.T[:V_ROWS, :].astype(BF)

        gq_ref[rs, :] = (pg[:, 0:GLA_QK_SLOTS] * (GLA_DK ** -0.5)).astype(BF)
        gk_ref[rs, :] = pg[:, COL_GK - COL_GQ:COL_GK - COL_GQ + GLA_QK_SLOTS].astype(BF)
        gv_ref[rs, :] = pg[:, COL_GV - COL_GQ:COL_GV - COL_GQ + GLA_SLOTS].astype(BF)
        gg_ref[rs, :] = pg[:, COL_GG - COL_GQ:COL_GG - COL_GQ + GLA_SLOTS].astype(BF)
        z = _dot(pg[:, COL_LR - COL_GQ:COL_LR - COL_GQ + LANE].astype(BF), wgk_ref[...]) + bgk_ref[...]
        ls = (jnp.minimum(z, 0.0) - jnp.log(1.0 + jnp.exp(-jnp.abs(z)))) * (1.0 / GLA_GATE_NORM)
        df_ref[rs, :] = ls[:, :GLA_QK_SLOTS]
        db_ref[rs, :] = ls[:, GLA_QK_SLOTS:]


def _in_proj(x2, mods, pw, cos_t, sin_t, l, row, rope):
    rows = x2.shape[0]
    tm = min(2 * IN_SUB, rows)
    rt = lambda n: pl.BlockSpec((tm, n), lambda i: (i, 0))
    modspec = lambda k: pl.BlockSpec((None, 8, D_MODEL), lambda i: (l, 0, k))
    outs = [(F_WIDTH, BF), (F_WIDTH, BF), None, (MLA_SLOTS, BF), None,
            (GLA_QK_SLOTS, BF), (GLA_QK_SLOTS, BF), (GLA_SLOTS, BF), (GLA_SLOTS, BF), (GLA_QK_SLOTS, F32),
            (GLA_QK_SLOTS, F32)]
    shapes = [jax.ShapeDtypeStruct((rows, o[0]), o[1]) if o else None for o in outs]
    specs = [rt(o[0]) if o else None for o in outs]
    shapes[2] = jax.ShapeDtypeStruct((MLA_SLOTS, rows), BF)
    specs[2] = pl.BlockSpec((MLA_SLOTS, tm), lambda i: (0, i))
    shapes[4] = jax.ShapeDtypeStruct((rows // tm, MLA_HEADS * V_ROWS, tm), BF)
    specs[4] = pl.BlockSpec((None, MLA_HEADS * V_ROWS, tm), lambda i: (i, 0, 0))
    return pl.pallas_call(
        functools.partial(_in_kernel, row=row, rope=rope),
        out_shape=shapes,
        grid=(rows // tm,),
        in_specs=[rt(D_MODEL), modspec(0), modspec(1), _layer((1, D_MODEL), l),
                  _layer((D_MODEL, IN_PAD), l), _full((F_WIDTH, 2 * F_WIDTH)),
                  _layer((1, 256), l), _layer((256, 2 * MLA_SLOTS), l),
                  _layer((1, MLA_KV_LORA), l), _layer((MLA_KV_LORA, 2 * MLA_SLOTS), l),
                  _full((LANE, 2 * MLA_SLOTS)),
                  _layer((1, MLA_SLOTS), l), _layer((1, MLA_SLOTS), l), _layer((1, MLA_SLOTS), l),
                  _layer((1, MLA_SLOTS), l),
                  rt(HEAD_SLOT), rt(HEAD_SLOT),
                  _layer((LANE, 2 * GLA_QK_SLOTS), l), _layer((1, 2 * GLA_QK_SLOTS), l)],
        out_specs=specs,
        compiler_params=_cparams("arbitrary"),
        name="in_proj",
    )(x2, mods, mods, pw["norm1_w"], pw["w_in"], pw["dft64"], pw["q_lora_w"], pw["w_uq"], pw["kv_lora_w"],
      pw["w_ukv"], pw["kpe_place"], pw["q_gain"], pw["q_gain_p"], pw["k_gain"], pw["k_gain_p"],
      cos_t, sin_t, pw["w_gk"], pw["b_gk"])


def _fft1_kernel(a_ref, b_ref, w_ref, g_ref):
    n1 = a_ref.shape[0]
    ab = jnp.concatenate([a_ref[...], b_ref[...]], axis=0)
    g = _dot(w_ref[...], ab)
    g_ref[0] = g[:n1].astype(BF)
    g_ref[1] = g[n1:].astype(BF)


def _fft2_kernel(g_ref, t_ref, o_ref, *, batch):
    for j in range(batch):
        g = jnp.concatenate([g_ref[0, j], g_ref[1, j]], axis=0)
        o_ref[:, j * F_WIDTH:(j + 1) * F_WIDTH] = _dot(t_ref[j], g).astype(BF)


def _fourier_latent(fa, fb, w1, tab):
    length = fa.shape[0]
    n1, n2 = tab.shape[0], tab.shape[1]
    cols = n2 * F_WIDTH
    tn = min(4096, cols)
    g = pl.pallas_call(
        _fft1_kernel,
        out_shape=jax.ShapeDtypeStruct((2, n1, cols), BF),
        grid=(cols // tn,),
        in_specs=[pl.BlockSpec((n1, tn), lambda j: (0, j)), pl.BlockSpec((n1, tn), lambda j: (0, j)),
                  _full((2 * n1, 2 * n1))],
        out_specs=pl.BlockSpec((2, n1, tn), lambda j: (0, 0, j)),
        compiler_params=_cparams("arbitrary"),
        name="fft_stage1",
    )(fa.reshape(n1, cols), fb.reshape(n1, cols), w1)
    batch = 8
    y = pl.pallas_call(
        functools.partial(_fft2_kernel, batch=batch),
        out_shape=jax.ShapeDtypeStruct((n2, n1 * F_WIDTH), BF),
        grid=(n1 // batch,),
        in_specs=[pl.BlockSpec((2, batch, n2, F_WIDTH), lambda i: (0, i, 0, 0)),
                  pl.BlockSpec((batch, n2, 2 * n2), lambda i: (i, 0, 0))],
        out_specs=pl.BlockSpec((n2, batch * F_WIDTH), lambda i: (0, i)),
        compiler_params=_cparams("arbitrary"),
        name="fft_stage2",
    )(g.reshape(2, n1, n2, F_WIDTH), tab)
    return y.reshape(length, F_WIDTH)


def _fctx_kernel(a_ref, b_ref, t_ref, o_ref):
    ab = jnp.concatenate([a_ref[...], b_ref[...]], axis=0)
    o_ref[...] = _dot(t_ref[...], ab).astype(BF)


def _fourier_ctx(fa, fb, tab):
    n = fa.shape[0]
    return pl.pallas_call(
        _fctx_kernel,
        out_shape=jax.ShapeDtypeStruct((n, F_WIDTH), BF),
        grid=(1,),
        in_specs=[_full((n, F_WIDTH)), _full((n, F_WIDTH)), _full((n, 2 * n))],
        out_specs=_full((n, F_WIDTH)),
        compiler_params=_cparams("arbitrary"),
        name="fft_ctx",
    )(fa, fb, tab)


def _attn_kernel(*refs, tk, has_x):
    if has_x:
        q_ref, kc_ref, vc_ref, kx_ref, vx_ref, o_ref, acc_ref, s_ref = refs
    else:
        q_ref, kc_ref, vc_ref, o_ref, acc_ref = refs
    tq = q_ref.shape[1]
    heads = range(q_ref.shape[0] // HEAD_SLOT)
    slot = lambda h: slice(h * HEAD_SLOT, (h + 1) * HEAD_SLOT)
    vrow = lambda h: slice(h * V_ROWS, (h + 1) * V_ROWS)
    qt = [q_ref[slot(h), :] for h in heads]

    def scores(k2):
        return tuple(_dot(k2[:, slot(h)], qt[h]) for h in heads)

    def absorb(s, vts, m):
        sub = vts[0].shape[1]
        m_new = [jnp.maximum(m[h], jnp.max(s[h], axis=0, keepdims=True)) for h in heads]
        p = [jnp.exp2(s[h] - m_new[h]).astype(BF) for h in heads]
        for h in heads:
            alpha = jnp.exp2(m[h] - m_new[h])
            pv = _dot(vts[0][vrow(h), :], p[h][0:sub, :])
            for t in range(1, len(vts)):
                pv += _dot(vts[t][vrow(h), :], p[h][t * sub:(t + 1) * sub, :])
            acc_ref[h] = alpha * acc_ref[h] + pv
        return tuple(m_new)

    acc_ref[...] = jnp.zeros_like(acc_ref)
    m = tuple(jnp.full((1, tq), -1e30, F32) for _ in heads)
    m = absorb(scores(kc_ref[...]), [vc_ref[t] for t in range(vc_ref.shape[0])], m)
    if has_x:
        n = kx_ref.shape[0] // tk
        sub = vx_ref.shape[2]
        assert n % 2 == 0
        unroll = ATTN_UNROLL if n % ATTN_UNROLL == 0 else 2

        def put_scores(buf, j):
            off = pl.multiple_of(j * tk, tk)
            s = scores(kx_ref[pl.ds(off, tk), :])
            for h in heads:
                s_ref[buf, h] = s[h]

        def take(buf, j, u, m):
            if tk >= sub:
                vts = [vx_ref[j * (tk // sub) + t] for t in range(tk // sub)]
            else:
                r = sub // tk
                assert unroll % r == 0
                vts = [vx_ref[j // r, :, (u % r) * tk:(u % r + 1) * tk]]
            return absorb(tuple(s_ref[buf, h] for h in heads), vts, m)

        def body(i, m):
            for u in range(unroll):
                j = unroll * i + u
                nxt = j + 1 if u + 1 < unroll else jnp.minimum(j + 1, n - 1)
                put_scores((u + 1) % 2, nxt)
                m = take(u % 2, j, u, m)
            return m

        put_scores(0, 0)
        m = lax.fori_loop(0, n // unroll, body, m)
    for h in heads:
        acc = acc_ref[h]
        o_ref[h * MLA_V:(h + 1) * MLA_V, :] = (acc[:MLA_V, :] / acc[MLA_V:MLA_V + 1, :]).astype(BF)


def _attention(qt, kc, vct, kx=None, vxt=None):
    rows = qt.shape[1]
    lc = kc.shape[0]
    tq = min(512, rows)
    has_x = kx is not None
    pair = ATTN_HEADS * HEAD_SLOT
    vpair = ATTN_HEADS * V_ROWS
    in_specs = [pl.BlockSpec((pair, tq), lambda p, i: (p, i)),
                pl.BlockSpec((lc, pair), lambda p, i: (0, p)),
                pl.BlockSpec((vct.shape[0], vpair, vct.shape[2]), lambda p, i: (0, p, 0))]
    args = [qt, kc, vct]
    tk = 256
    scratch = [pltpu.VMEM((ATTN_HEADS, V_ROWS, tq), F32)]
    if has_x:
        lx = kx.shape[0]
        tk = min(tk, lx)
        in_specs += [pl.BlockSpec((lx, pair), lambda p, i: (0, p)),
                     pl.BlockSpec((vxt.shape[0], vpair, vxt.shape[2]), lambda p, i: (0, p, 0))]
        args += [kx, vxt]
        scratch.append(pltpu.VMEM((2, ATTN_HEADS, tk, tq), F32))
    return pl.pallas_call(
        functools.partial(_attn_kernel, tk=tk, has_x=has_x),
        out_shape=jax.ShapeDtypeStruct((MLA_WIDTH, rows), BF),
        grid=(MLA_HEADS // ATTN_HEADS, rows // tq),
        in_specs=in_specs,
        out_specs=pl.BlockSpec((ATTN_HEADS * MLA_V, tq), lambda p, i: (p, i)),
        scratch_shapes=scratch,
        compiler_params=_cparams("arbitrary", "arbitrary"),
        name="attention",
    )(*args)


def _gla_kernel(qf_ref, kf_ref, vf_ref, df_ref, qb_ref, kb_ref, vb_ref, db_ref, s0_ref,
                of_ref, ob_ref, sfin_ref, st_ref):
    i = pl.program_id(0)
    t = qf_ref.shape[0]
    c = GLA_CHUNK
    nc = t // c
    assert c & (c - 1) == 0

    @pl.when(i == 0)
    def _():
        st_ref[...] = s0_ref[...]

    r = lax.broadcasted_iota(jnp.int32, (t, t), 0)
    s = lax.broadcasted_iota(jnp.int32, (t, t), 1)
    same_chunk = jnp.bitwise_xor(r, s) < c
    dirs = ((qf_ref, kf_ref, vf_ref, df_ref, of_ref, same_chunk & (s <= r), range(nc), c - 1),
            (qb_ref, kb_ref, vb_ref, db_ref, ob_ref, same_chunk & (s >= r), range(nc - 1, -1, -1), 0))
    heads = range(GLA_HEADS)
    slot = lambda h: slice(h * HEAD_SLOT, (h + 1) * HEAD_SLOT)
    chunk = lambda ch: slice(ch * c, (ch + 1) * c)
    pair = lambda h: slice((h // 2) * HEAD_SLOT, (h // 2 + 1) * HEAD_SLOT)
    lane = lax.broadcasted_iota(jnp.int32, (1, HEAD_SLOT), 1)
    own = [(lane >= (h % 2) * GLA_QK_SLOT) & (lane < (h % 2 + 1) * GLA_QK_SLOT) for h in heads]

    b = []
    for q_ref, k_ref, v_ref, g_ref, o_ref, mask, order, last in dirs:
        tri = mask.astype(BF)
        g = g_ref[...]
        g_hi = g.astype(BF)
        g_r = g - g_hi.astype(F32)
        g_mid = g_r.astype(BF)
        g_lo = (g_r - g_mid.astype(F32)).astype(BF)
        b.append(_dot(tri, g_hi) + _dot(tri, g_mid) + _dot(tri, g_lo))

    work = []
    for d, (q_ref, k_ref, v_ref, g_ref, o_ref, mask, order, last) in enumerate(dirs):
        tot = [b[d][ch * c + last:ch * c + last + 1, :] for ch in range(nc)]
        b_tot = jnp.concatenate([jnp.broadcast_to(tot[ch], (c, GLA_QK_SLOTS)) for ch in range(nc)], axis=0)
        q_in = (q_ref[...].astype(F32) * jnp.exp(b[d])).astype(BF)
        kf = k_ref[...].astype(F32)
        k_in = (kf * jnp.exp(-b[d])).astype(BF)
        k_out = (kf * jnp.exp(b_tot - b[d])).astype(BF)
        v = v_ref[...]
        q_h = [jnp.where(own[h], q_in[:, pair(h)], jnp.zeros_like(q_in[:, pair(h)])) for h in heads]
        ko_h = [jnp.where(own[h], k_out[:, pair(h)], jnp.zeros_like(k_out[:, pair(h)])) for h in heads]
        a = [_dot_nt(q_h[h], k_in[:, pair(h)]) for h in heads]
        inc = [{ch: _dot_tn(v[chunk(ch), slot(h)], ko_h[h][chunk(ch), :]) for ch in order} for h in heads]
        work.append((tot, q_h, v, a, inc))

    o_intra = []
    for d, (q_ref, k_ref, v_ref, g_ref, o_ref, mask, order, last) in enumerate(dirs):
        tot, q_h, v, a, inc = work[d]
        o_intra.append([_dot(jnp.where(mask, a[h], 0.0).astype(BF), v[:, slot(h)]) for h in heads])

    for d, (q_ref, k_ref, v_ref, g_ref, o_ref, mask, order, last) in enumerate(dirs):
        tot, q_h, v, a, inc = work[d]
        entering = []
        for h in heads:
            st = st_ref[d, h]
            ent = {}
            for ch in order:
                ent[ch] = st.astype(BF)
                st = st * jnp.exp(tot[ch][:, pair(h)]) + inc[h][ch]
            st_ref[d, h] = st
            entering.append(ent)
        for h in heads:
            for ch in order:
                o_ref[chunk(ch), slot(h)] = (o_intra[d][h][chunk(ch), :]
                                             + _dot_nt(q_h[h][chunk(ch), :], entering[h][ch])).astype(BF)

    @pl.when(i == pl.num_programs(0) - 1)
    def _():
        sfin_ref[...] = st_ref[...]


def _gla(gq, gk, gv, df, db, s0):
    rows = gq.shape[0]
    c = min(GLA_STEP, rows)
    n = rows // c
    fwd = pl.BlockSpec((c, GLA_SLOTS), lambda i: (i, 0))
    bwd = pl.BlockSpec((c, GLA_SLOTS), lambda i: (n - 1 - i, 0))
    fwd_qk = pl.BlockSpec((c, GLA_QK_SLOTS), lambda i: (i, 0))
    bwd_qk = pl.BlockSpec((c, GLA_QK_SLOTS), lambda i: (n - 1 - i, 0))
    st_shape = (2, GLA_HEADS, HEAD_SLOT, HEAD_SLOT)
    return pl.pallas_call(
        _gla_kernel,
        out_shape=[jax.ShapeDtypeStruct((rows, GLA_SLOTS), BF), jax.ShapeDtypeStruct((rows, GLA_SLOTS), BF),
                   jax.ShapeDtypeStruct(st_shape, F32)],
        grid=(n,),
        in_specs=[fwd_qk, fwd_qk, fwd, fwd_qk, bwd_qk, bwd_qk, bwd, bwd_qk, _full(st_shape)],
        out_specs=[fwd, bwd, _full(st_shape)],
        scratch_shapes=[pltpu.VMEM(st_shape, F32)],
        compiler_params=_cparams("arbitrary"),
        name="gla_scan",
    )(gq, gk, gv, df, gq, gk, gv, db, s0)


def _out_kernel(x_ref, four_ref, att_ref, of_ref, ob_ref, gg_ref, gain_ref, wout_ref, g1_ref, n2_ref,
                sh_ref, sc_ref, xo_ref, h_ref, *, row):
    o = of_ref[...].astype(F32) + ob_ref[...].astype(F32)
    g = gg_ref[...].astype(F32)
    gate = g * (1.0 / (1.0 + jnp.exp(-g)))
    y = _dot(four_ref[...], wout_ref[0:F_WIDTH, :])
    y += _dot_tn(att_ref[...], wout_ref[F_WIDTH:F_WIDTH + MLA_WIDTH, :])
    for h in range(GLA_HEADS):
        sl = slice(h * HEAD_SLOT, (h + 1) * HEAD_SLOT)
        oh = o[:, sl]
        lin = oh * _rms(oh, GLA_DV) * gain_ref[:, sl] * gate[:, sl]
        lo = F_WIDTH + MLA_WIDTH + h * HEAD_SLOT
        y += _dot(lin.astype(BF), wout_ref[lo:lo + HEAD_SLOT, :])
    x = x_ref[...] + g1_ref[row:row + 1, :] * y
    xo_ref[...] = x
    hn = x * _rms(x, D_MODEL) * n2_ref[...]
    h_ref[...] = (hn * (1.0 + sc_ref[row:row + 1, :]) + sh_ref[row:row + 1, :]).astype(BF)


def _out_proj(x2, four, att, o_f, o_b, gg, mods, pw, l, row):
    rows = x2.shape[0]
    tm = min(ROW_TILE, rows)
    rt = lambda n: pl.BlockSpec((tm, n), lambda i: (i, 0))
    modspec = lambda k: pl.BlockSpec((None, 8, D_MODEL), lambda i: (l, 0, k))
    return pl.pallas_call(
        functools.partial(_out_kernel, row=row),
        out_shape=[jax.ShapeDtypeStruct((rows, D_MODEL), F32), jax.ShapeDtypeStruct((rows, D_MODEL), BF)],
        grid=(rows // tm,),
        in_specs=[rt(D_MODEL), rt(F_WIDTH), pl.BlockSpec((MLA_WIDTH, tm), lambda i: (0, i)),
                  rt(GLA_SLOTS), rt(GLA_SLOTS), rt(GLA_SLOTS),
                  _layer((1, GLA_SLOTS), l), _layer((MIX_PAD, D_MODEL), l), modspec(2),
                  _layer((1, D_MODEL), l), modspec(3), modspec(4)],
        out_specs=[rt(D_MODEL), rt(D_MODEL)],
        compiler_params=_cparams("arbitrary"),
        name="out_proj",
    )(x2, four, att, o_f, o_b, gg, pw["gla_gain"], pw["w_out"], mods, pw["norm2_w"], mods, mods)


def _ffn_kernel(x_ref, h_ref, hp_ref, hn_ref, w12_ref, conv_ref, wd_ref, g2_ref, o_ref, acc_ref, hx_ref, u_ref,
                *, row):
    i = pl.program_id(0)
    tm = h_ref.shape[0]
    m = tm + 2 * FF_HALO
    keep_prev = jnp.where(i > 0, 1.0, 0.0)
    keep_next = jnp.where(i < pl.num_programs(0) - 1, 1.0, 0.0)
    hx_ref[0:FF_HALO, :] = (hp_ref[...].astype(F32) * keep_prev).astype(BF)
    hx_ref[FF_HALO:FF_HALO + tm, :] = h_ref[...]
    hx_ref[FF_HALO + tm:m, :] = (hn_ref[...].astype(F32) * keep_next).astype(BF)
    acc_ref[...] = jnp.zeros_like(acc_ref)

    starts = [sum(FF_CHUNKS[:c]) for c in range(len(FF_CHUNKS))]

    def cols(c, half):
        lo = half * D_FF + starts[c]
        return slice(lo, lo + FF_CHUNKS[c])

    def up(buf, c):
        for half in range(2):
            u_ref[buf, half, :, 0:FF_CHUNKS[c]] = _dot(hx_ref[...], w12_ref[:, cols(c, half)])

    def conv(buf, c, half):
        u = u_ref[buf, half, :, 0:FF_CHUNKS[c]]
        cw = conv_ref[:, cols(c, half)]
        u_prev = pltpu.roll(u, 1, axis=0)
        u_next = pltpu.roll(u, m - 1, axis=0)
        uc = u_prev * cw[0:1, :] + u * cw[1:2, :] + u_next * cw[2:3, :] + cw[3:4, :]
        return uc[FF_HALO:FF_HALO + tm, :]

    def down(buf, c):
        a = conv(buf, c, 0)
        act = a * (1.0 / (1.0 + jnp.exp(-a))) * conv(buf, c, 1)
        acc_ref[...] += _dot(act.astype(BF), wd_ref[starts[c]:starts[c] + FF_CHUNKS[c], :])

    nbuf = u_ref.shape[0]
    nchunk = len(FF_CHUNKS)
    for c in range(min(FF_AHEAD, nchunk)):
        up(c % nbuf, c)
    for c in range(nchunk):
        if c + FF_AHEAD < nchunk:
            up((c + FF_AHEAD) % nbuf, c + FF_AHEAD)
        down(c % nbuf, c)
    o_ref[...] = x_ref[...] + g2_ref[row:row + 1, :] * acc_ref[...]


def _ffn(x2, h2, mods, pw, l, row):
    rows = x2.shape[0]
    tm = min(512, rows)
    nt = rows // tm
    per = tm // FF_HALO
    last_blk = rows // FF_HALO - 1
    m = tm + 2 * FF_HALO
    rt = lambda n: pl.BlockSpec((tm, n), lambda i: (i, 0))
    once = lambda shape: pl.BlockSpec((None,) + shape, lambda i: (l,) + (0,) * len(shape),
                                      pipeline_mode=pl.Buffered(1))
    return pl.pallas_call(
        functools.partial(_ffn_kernel, row=row),
        out_shape=jax.ShapeDtypeStruct((rows, D_MODEL), F32),
        grid=(nt,),
        in_specs=[rt(D_MODEL), rt(D_MODEL),
                  pl.BlockSpec((FF_HALO, D_MODEL), lambda i: (jnp.maximum(i * per - 1, 0), 0)),
                  pl.BlockSpec((FF_HALO, D_MODEL), lambda i: (jnp.minimum((i + 1) * per, last_blk), 0)),
                  once((D_MODEL, 2 * D_FF)), once((4, 2 * D_FF)), once((D_FF, D_MODEL)),
                  pl.BlockSpec((None, 8, D_MODEL), lambda i: (l, 0, 5))],
        out_specs=rt(D_MODEL),
        scratch_shapes=[pltpu.VMEM((tm, D_MODEL), F32), pltpu.VMEM((m, D_MODEL), BF),
                        pltpu.VMEM((FF_AHEAD + 1, 2, m, max(FF_CHUNKS)), F32)],
        compiler_params=_cparams("arbitrary"),
        name="conv_ffn",
    )(x2, h2, h2, h2, pw["w12"], pw["conv"], pw["w_down"], mods)


def kernel(x, c, ctx, c_ctx, mod_w, mod_b, norm1_w, norm2_w, w_in, mla_q_lora_norm_w, mla_w_uq, mla_kv_lora_norm_w, mla_w_ukv, mla_q_norm_w, mla_k_norm_w, gla_w_gk_fwd, gla_b_gk_fwd, gla_w_gk_bwd, gla_b_gk_bwd, gla_norm_w, w_out, ffn_w_12, ffn_conv_w, ffn_conv_b, ffn_w_down):
    batch, seq, d = x.shape
    assert batch == 1 and d == D_MODEL
    lc = ctx.shape[1]
    depth = mod_w.shape[0]
    n1 = int(round(math.sqrt(seq)))
    assert n1 * n1 == seq and seq % ROW_TILE == 0 and lc % GLA_CHUNK == 0

    pw = _prep_weights(dict(
        w_in=w_in, mla_w_uq=mla_w_uq, mla_q_lora_norm_w=mla_q_lora_norm_w, mla_kv_lora_norm_w=mla_kv_lora_norm_w,
        mla_w_ukv=mla_w_ukv, mla_q_norm_w=mla_q_norm_w, mla_k_norm_w=mla_k_norm_w, gla_w_gk_fwd=gla_w_gk_fwd,
        gla_b_gk_fwd=gla_b_gk_fwd, gla_w_gk_bwd=gla_w_gk_bwd, gla_b_gk_bwd=gla_b_gk_bwd, gla_norm_w=gla_norm_w,
        w_out=w_out, ffn_w_12=ffn_w_12, ffn_conv_w=ffn_conv_w, ffn_conv_b=ffn_conv_b, ffn_w_down=ffn_w_down,
        norm1_w=norm1_w, norm2_w=norm2_w))
    cos_t, sin_t = _rope_tables(seq)
    ones_c = jnp.ones((lc, HEAD_SLOT), F32)
    w1, tab = _fft_tables(n1, n1)
    tab_c = _ctx_dft_table(lc)

    cond8 = jnp.zeros((8, d), F32).at[0].set(c[0].astype(F32)).at[1].set(c_ctx.astype(F32))
    mods = _adaln(cond8, mod_w, mod_b)

    xs = x[0].astype(F32)
    xc = ctx[0].astype(F32)
    s_zero = jnp.zeros((2, GLA_HEADS, HEAD_SLOT, HEAD_SLOT), F32)
    for l in range(depth):
        last = l == depth - 1
        fa_c, fb_c, q_c, k_c, v_c, gq_c, gk_c, gv_c, gg_c, df_c, db_c = _in_proj(xc, mods, pw, ones_c, ones_c, l, 1, False)
        fa_x, fb_x, q_x, k_x, v_x, gq_x, gk_x, gv_x, gg_x, df_x, db_x = _in_proj(xs, mods, pw, cos_t, sin_t, l, 0, True)

        of_c, ob_c, s_c = _gla(gq_c, gk_c, gv_c, df_c, db_c, s_zero)
        of_x, ob_x, _ = _gla(gq_x, gk_x, gv_x, df_x, db_x, s_c)
        four_x = _fourier_latent(fa_x, fb_x, w1, tab)
        att_x = _attention(q_x, k_c, v_c, k_x, v_x)
        x_mid, h2 = _out_proj(xs, four_x, att_x, of_x, ob_x, gg_x, mods, pw, l, 0)
        xs = _ffn(x_mid, h2, mods, pw, l, 0)
        if not last:
            four_c = _fourier_ctx(fa_c, fb_c, tab_c)
            att_c = _attention(q_c, k_c, v_c)
            c_mid, hc2 = _out_proj(xc, four_c, att_c, of_c, ob_c, gg_c, mods, pw, l, 1)
            xc = _ffn(c_mid, hc2, mods, pw, l, 1)
    return xs[None].astype(x.dtype)
```

```python
import functools
import math

import numpy as np
import jax
import jax.numpy as jnp
from jax import lax
from jax.experimental import pallas as pl
from jax.experimental.pallas import tpu as pltpu

D_MODEL = 1024
DEPTH = 2
GRID_W = 64
EPS = 1e-6

F_GROUPS = 4
F_DIM = 64
F_WIDTH = F_GROUPS * F_DIM

MLA_HEADS = 6
MLA_Q_LORA = 192
MLA_KV_LORA = 128
MLA_NOPE = 64
MLA_ROPE = 32
MLA_V = 64
MLA_QK = MLA_NOPE + MLA_ROPE
MLA_WIDTH = MLA_HEADS * MLA_V
MLA_IN = MLA_Q_LORA + MLA_KV_LORA + MLA_ROPE
ROPE_BASE = 10000.0

GLA_HEADS = 4
GLA_DK = 48
GLA_DV = 96
GLA_GATE_RANK = 16
GLA_GATE_NORM = 16.0
GLA_CHUNK = 64
GLA_STEP = 256
GLA_WIDTH = GLA_HEADS * GLA_DV
GLA_QK_W = GLA_HEADS * GLA_DK

MIX_WIDTH = F_WIDTH + MLA_WIDTH + GLA_WIDTH
D_FF = 2816
N_MOD = 6

LANE = 128
SUBLANE_BF16 = 16
VMEM_LIMIT = 48 * 1024 * 1024

HEAD_SLOT = LANE
V_ROWS = 80
MLA_SLOTS = MLA_HEADS * HEAD_SLOT
GLA_SLOTS = GLA_HEADS * HEAD_SLOT
GLA_QK_SLOT = 64
GLA_QK_SLOTS = GLA_HEADS * GLA_QK_SLOT

COL_F = 0
COL_CQ = COL_F + F_WIDTH
COL_CKV = COL_CQ + 256
COL_KPE = COL_CKV + MLA_KV_LORA
COL_GQ = COL_KPE + LANE
COL_GK = COL_GQ + GLA_QK_SLOTS
COL_GV = COL_GK + GLA_QK_SLOTS
COL_GG = COL_GV + GLA_SLOTS
COL_LR = COL_GG + GLA_SLOTS
IN_PAD = COL_LR + LANE

MIX_PAD = F_WIDTH + MLA_WIDTH + GLA_SLOTS

FF_CHUNKS = (768, 768, 768, 512)
assert sum(FF_CHUNKS) == D_FF
FF_HALO = SUBLANE_BF16
FF_AHEAD = 1

LOG2E = 1.4426950408889634

ROW_TILE = 256
IN_SUB = 256
ATTN_UNROLL = 32
ATTN_HEADS = 2
BF = jnp.bfloat16
F32 = jnp.float32


def _cparams(*sem):
    return pltpu.CompilerParams(dimension_semantics=sem, vmem_limit_bytes=VMEM_LIMIT)


def _dot(a, b):
    return jnp.dot(a, b, preferred_element_type=F32)


def _dot_nt(a, b):
    return lax.dot_general(a, b, (((1,), (1,)), ((), ())), preferred_element_type=F32)


def _dot_tn(a, b):
    return lax.dot_general(a, b, (((0,), (0,)), ((), ())), preferred_element_type=F32)


def _full(shape):
    n = len(shape)
    return pl.BlockSpec(shape, lambda *_: (0,) * n)


def _layer(shape, l):
    n = len(shape)
    return pl.BlockSpec((None,) + tuple(shape), lambda *_: (l,) + (0,) * n)


def _rot_partner(n_rope):
    q = n_rope // 4
    src = np.zeros(n_rope, np.int64)
    sgn = np.zeros(n_rope, np.float32)
    for base in (0, 2 * q):
        for j in range(q):
            src[base + j] = base + q + j
            sgn[base + j] = -1.0
            src[base + q + j] = base + j
            sgn[base + q + j] = 1.0
    return src, sgn


def _take_cols(w, src, sgn=None, axis=-1):
    src = np.asarray(src)
    sgn = np.ones(len(src), np.float32) if sgn is None else np.asarray(sgn, np.float32)
    axis = axis % w.ndim
    pieces = []
    lo = 0
    while lo < len(src):
        hi = lo + 1
        if src[lo] < 0:
            while hi < len(src) and src[hi] < 0:
                hi += 1
            shape = list(w.shape)
            shape[axis] = hi - lo
            pieces.append(jnp.zeros(shape, w.dtype))
        else:
            while hi < len(src) and src[hi] == src[hi - 1] + 1 and sgn[hi] == sgn[lo]:
                hi += 1
            piece = lax.slice_in_dim(w, int(src[lo]), int(src[lo]) + hi - lo, axis=axis)
            pieces.append(-piece if sgn[lo] < 0 else piece)
        lo = hi
    return jnp.concatenate(pieces, axis=axis)


def _pad_rows(w, n):
    pad = [(0, 0)] * w.ndim
    pad[-2] = (0, n - w.shape[-2])
    return jnp.pad(w, pad)


def _in_proj_layout():
    src = -np.ones(IN_PAD, np.int64)
    sgn = np.ones(IN_PAD, np.float32)
    src[COL_F:COL_F + F_WIDTH] = np.arange(F_WIDTH)
    o = F_WIDTH
    src[COL_CQ:COL_CQ + MLA_Q_LORA] = o + np.arange(MLA_Q_LORA)
    o += MLA_Q_LORA
    src[COL_CKV:COL_CKV + MLA_KV_LORA] = o + np.arange(MLA_KV_LORA)
    o += MLA_KV_LORA
    src[COL_KPE:COL_KPE + MLA_ROPE] = o + np.arange(MLA_ROPE)
    psrc, psgn = _rot_partner(MLA_ROPE)
    src[COL_KPE + MLA_ROPE:COL_KPE + 2 * MLA_ROPE] = o + psrc
    sgn[COL_KPE + MLA_ROPE:COL_KPE + 2 * MLA_ROPE] = psgn
    o += MLA_ROPE
    for h in range(GLA_HEADS):
        src[COL_GQ + h * GLA_QK_SLOT:COL_GQ + h * GLA_QK_SLOT + GLA_DK] = o + h * GLA_DK + np.arange(GLA_DK)
    o += GLA_QK_W
    for h in range(GLA_HEADS):
        src[COL_GK + h * GLA_QK_SLOT:COL_GK + h * GLA_QK_SLOT + GLA_DK] = o + h * GLA_DK + np.arange(GLA_DK)
    o += GLA_QK_W
    for h in range(GLA_HEADS):
        src[COL_GV + h * HEAD_SLOT:COL_GV + h * HEAD_SLOT + GLA_DV] = o + h * GLA_DV + np.arange(GLA_DV)
    o += GLA_WIDTH
    src[COL_LR:COL_LR + GLA_GATE_RANK] = o + np.arange(GLA_GATE_RANK)
    o += GLA_GATE_RANK
    for h in range(GLA_HEADS):
        src[COL_GG + h * HEAD_SLOT:COL_GG + h * HEAD_SLOT + GLA_DV] = o + h * GLA_DV + np.arange(GLA_DV)
    return src, sgn


def _head_slots(n_heads, d_src, d_take, src_off=0):
    src = -np.ones(n_heads * HEAD_SLOT, np.int64)
    for h in range(n_heads):
        src[h * HEAD_SLOT:h * HEAD_SLOT + d_take] = h * d_src + src_off + np.arange(d_take)
    return src


def _prep_weights(w):
    f32 = lambda a: a.astype(F32)
    out = {}
    src, sgn = _in_proj_layout()
    out["w_in"] = _take_cols(w["w_in"], src, sgn).astype(BF)

    m = np.arange(F_DIM)
    ang = 2.0 * np.pi * np.outer(m, m) / F_DIM
    c64, s64 = np.cos(ang), np.sin(ang)
    dft = np.zeros((F_WIDTH, 2 * F_WIDTH), np.float32)
    for g in range(F_GROUPS):
        dft[g * F_DIM:(g + 1) * F_DIM, g * F_DIM:(g + 1) * F_DIM] = c64
        dft[g * F_DIM:(g + 1) * F_DIM, F_WIDTH + g * F_DIM:F_WIDTH + (g + 1) * F_DIM] = s64
    out["dft64"] = jnp.asarray(dft, F32).astype(BF)

    psrc, psgn = _rot_partner(MLA_ROPE)
    q_src = _head_slots(MLA_HEADS, MLA_QK, MLA_QK)
    q_part = -np.ones(MLA_SLOTS, np.int64)
    q_psg = np.ones(MLA_SLOTS, np.float32)
    g_part = -np.ones(MLA_SLOTS, np.int64)
    for h in range(MLA_HEADS):
        lo = h * HEAD_SLOT + MLA_NOPE
        q_part[lo:lo + MLA_ROPE] = h * MLA_QK + MLA_NOPE + psrc
        q_psg[lo:lo + MLA_ROPE] = psgn
        g_part[lo:lo + MLA_ROPE] = MLA_NOPE + psrc
    g_src = np.where(q_src >= 0, q_src % MLA_QK, -1)
    wq = jnp.concatenate([_take_cols(w["mla_w_uq"], q_src), _take_cols(w["mla_w_uq"], q_part, q_psg)], axis=-1)
    out["w_uq"] = _pad_rows(wq, 256).astype(BF)
    out["q_lora_w"] = jnp.pad(f32(w["mla_q_lora_norm_w"]), ((0, 0), (0, 256 - MLA_Q_LORA)))[:, None, :]
    out["kv_lora_w"] = f32(w["mla_kv_lora_norm_w"])[:, None, :]
    out["q_gain"] = _take_cols(w["mla_q_norm_w"], g_src)[:, None, :]
    out["q_gain_p"] = _take_cols(w["mla_q_norm_w"], g_part)[:, None, :]
    out["k_gain"] = _take_cols(w["mla_k_norm_w"], g_src)[:, None, :]
    out["k_gain_p"] = _take_cols(w["mla_k_norm_w"], g_part)[:, None, :]

    kn_src = _head_slots(MLA_HEADS, MLA_NOPE + MLA_V, MLA_NOPE)
    v_src = _head_slots(MLA_HEADS, MLA_NOPE + MLA_V, MLA_V, MLA_NOPE)
    out["w_ukv"] = jnp.concatenate([_take_cols(w["mla_w_ukv"], kn_src), _take_cols(w["mla_w_ukv"], v_src)],
                                   axis=-1).astype(BF)

    e2 = np.zeros((LANE, 2 * MLA_SLOTS), np.float32)
    for h in range(MLA_HEADS):
        for j in range(MLA_ROPE):
            e2[j, h * HEAD_SLOT + MLA_NOPE + j] = 1.0
            e2[MLA_ROPE + j, MLA_SLOTS + h * HEAD_SLOT + MLA_NOPE + j] = 1.0
    out["kpe_place"] = jnp.asarray(e2, BF)

    gk_src = -np.ones(GLA_QK_SLOTS, np.int64)
    for h in range(GLA_HEADS):
        gk_src[h * GLA_QK_SLOT:h * GLA_QK_SLOT + GLA_DK] = h * GLA_DK + np.arange(GLA_DK)
    wgk = jnp.concatenate([_take_cols(w["gla_w_gk_fwd"], gk_src), _take_cols(w["gla_w_gk_bwd"], gk_src)], axis=-1)
    out["w_gk"] = _pad_rows(wgk, LANE).astype(BF)
    out["b_gk"] = jnp.concatenate([_take_cols(w["gla_b_gk_fwd"], gk_src), _take_cols(w["gla_b_gk_bwd"], gk_src)],
                                  axis=-1)[:, None, :]
    gv_src = _head_slots(GLA_HEADS, GLA_DV, GLA_DV)
    out["gla_gain"] = _take_cols(w["gla_norm_w"], np.where(gv_src >= 0, gv_src % GLA_DV, -1))[:, None, :]

    row_src = np.concatenate([np.arange(F_WIDTH), np.where(gv_src >= 0, F_WIDTH + MLA_WIDTH + gv_src, -1),
                              F_WIDTH + np.arange(MLA_WIDTH)])
    out["w_out"] = _take_cols(w["w_out"], row_src, axis=-2).astype(BF)

    nd = w["ffn_w_12"].shape[0]
    out["w12"] = w["ffn_w_12"].astype(BF)
    out["conv"] = jnp.concatenate([w["ffn_conv_w"], w["ffn_conv_b"][:, None, :]], axis=1).astype(F32)
    out["w_down"] = w["ffn_w_down"].astype(BF)
    out["norm1_w"] = f32(w["norm1_w"])[:, None, :]
    out["norm2_w"] = f32(w["norm2_w"])[:, None, :]
    return out


def _rope_tables(n_tokens):
    q = MLA_ROPE // 4
    t = np.arange(n_tokens)
    row = (t // GRID_W).astype(np.float32)
    col = (t % GRID_W).astype(np.float32)
    axis_dims = MLA_ROPE // 2
    inv_freq = np.power(np.float32(ROPE_BASE), -np.arange(0, axis_dims, 2, dtype=np.float32) / axis_dims)
    ang_r = row[:, None] * inv_freq
    ang_c = col[:, None] * inv_freq
    cos = np.ones((n_tokens, HEAD_SLOT), np.float32)
    sin = np.zeros((n_tokens, HEAD_SLOT), np.float32)
    for k, ang in enumerate((ang_r, ang_r, ang_c, ang_c)):
        lo = MLA_NOPE + k * q
        cos[:, lo:lo + q] = np.cos(ang)
        sin[:, lo:lo + q] = np.sin(ang)
    return jnp.asarray(cos), jnp.asarray(sin)


def _fft_tables(n1, n2):
    k = np.arange(n1)
    ang1 = 2.0 * np.pi * np.outer(k, k) / n1
    fr, fi = np.cos(ang1), -np.sin(ang1)
    w1 = np.block([[fr, fi], [fi, -fr]])
    length = n1 * n2
    kk = (np.arange(n1)[:, None] + n1 * np.arange(n2)[None, :]).astype(np.float64)
    t2 = np.arange(n2, dtype=np.float64)
    ang2 = 2.0 * np.pi * kk[:, :, None] * t2[None, None, :] / length
    norm = 1.0 / math.sqrt(length * F_DIM)
    tab = np.concatenate([np.cos(ang2), np.sin(ang2)], axis=-1) * norm
    return jnp.asarray(w1, F32).astype(BF), jnp.asarray(tab, F32).astype(BF)


def _ctx_dft_table(n):
    k = np.arange(n)
    ang = 2.0 * np.pi * np.outer(k, k) / n
    norm = 1.0 / math.sqrt(n * F_DIM)
    return jnp.asarray(np.concatenate([np.cos(ang), -np.sin(ang)], axis=1) * norm, F32).astype(BF)


def _mod_kernel(c_ref, w_ref, b_ref, o_ref):
    c = c_ref[...]
    s = c * (1.0 / (1.0 + jnp.exp(-c)))
    o_ref[...] = _dot(s.astype(BF), w_ref[...].astype(BF)) + b_ref[...]


def _adaln(cond8, mod_w, mod_b):
    nd, d, n = mod_w.shape
    tn = 1536
    return pl.pallas_call(
        _mod_kernel,
        out_shape=jax.ShapeDtypeStruct((nd, 8, n), F32),
        grid=(nd, n // tn),
        in_specs=[pl.BlockSpec((8, d), lambda l, j: (0, 0)),
                  pl.BlockSpec((None, d, tn), lambda l, j: (l, 0, j)),
                  pl.BlockSpec((None, 1, tn), lambda l, j: (l, 0, j))],
        out_specs=pl.BlockSpec((None, 8, tn), lambda l, j: (l, 0, j)),
        compiler_params=_cparams("arbitrary", "arbitrary"),
        name="adaln",
    )(cond8, mod_w, mod_b.reshape(nd, 1, n))


def _rms(x, n):
    return lax.rsqrt(jnp.sum(x * x, axis=-1, keepdims=True) * (1.0 / n) + EPS)


def _in_kernel(x_ref, sh_ref, sc_ref, n1_ref, win_ref, dft_ref, qlw_ref, wq_ref, kvlw_ref, wkv_ref, place_ref,
               qg_ref, qgp_ref, kg_ref, kgp_ref, cos_ref, sin_ref, wgk_ref, bgk_ref,
               fa_ref, fb_ref, q_ref, k_ref, v_ref, gq_ref, gk_ref, gv_ref, gg_ref, df_ref, db_ref,
               *, row, rope):
    for r0 in range(0, x_ref.shape[0], IN_SUB):
        rs = slice(r0, min(r0 + IN_SUB, x_ref.shape[0]))
        x = x_ref[rs, :]
        xn = x * _rms(x, D_MODEL) * n1_ref[...]
        h = xn * (1.0 + sc_ref[row:row + 1, :]) + sh_ref[row:row + 1, :]
        hb = h.astype(BF)
        p = _dot(hb, win_ref[:, :COL_GQ])

        ab = _dot(p[:, COL_F:COL_F + F_WIDTH].astype(BF), dft_ref[...])
        fa_ref[rs, :] = ab[:, :F_WIDTH].astype(BF)
        fb_ref[rs, :] = ab[:, F_WIDTH:].astype(BF)

        cq = p[:, COL_CQ:COL_CQ + 256]
        cqn = cq * _rms(cq, MLA_Q_LORA) * qlw_ref[...]
        qq = _dot(cqn.astype(BF), wq_ref[...])
        ckv = p[:, COL_CKV:COL_CKV + MLA_KV_LORA]
        ckvn = ckv * _rms(ckv, MLA_KV_LORA) * kvlw_ref[...]
        kk = _dot(ckvn.astype(BF), wkv_ref[...])
        kp = _dot(p[:, COL_KPE:COL_KPE + LANE].astype(BF), place_ref[...])
        pg = _dot(hb, win_ref[:, COL_GQ:])
        v_lane = lax.broadcasted_iota(jnp.int32, (x.shape[0], HEAD_SLOT), 1)

        q_scale = (MLA_QK ** -0.5) * LOG2E
        for hd in range(MLA_HEADS):
            sl = slice(hd * HEAD_SLOT, (hd + 1) * HEAD_SLOT)
            sp = slice(MLA_SLOTS + hd * HEAD_SLOT, MLA_SLOTS + (hd + 1) * HEAD_SLOT)
            qh = qq[:, sl]
            kh = kk[:, sl] + kp[:, sl]
            rq = _rms(qh, MLA_QK) * q_scale
            rk = _rms(kh, MLA_QK)
            if rope:
                cos = cos_ref[rs, :]
                sin = sin_ref[rs, :]
                qo = rq * (qh * (qg_ref[:, sl] * cos) + qq[:, sp] * (qgp_ref[:, sl] * sin))
                ko = rk * (kh * (kg_ref[:, sl] * cos) + kp[:, sp] * (kgp_ref[:, sl] * sin))
            else:
                qo = rq * (qh * qg_ref[:, sl])
                ko = rk * (kh * kg_ref[:, sl])
            q_ref[sl, rs] = qo.T.astype(BF)
            k_ref[rs, sl] = ko.astype(BF)
            vh = jnp.where(v_lane == MLA_V, 1.0, kk[:, MLA_SLOTS + hd * HEAD_SLOT:MLA_SLOTS + (hd + 1) * HEAD_SLOT])
            v_ref[hd * V_ROWS:(hd + 1) * V_ROWS, rs] = vh.T[:V_ROWS, :].astype(BF)

        gq_ref[rs, :] = (pg[:, 0:GLA_QK_SLOTS] * (GLA_DK ** -0.5)).astype(BF)
        gk_ref[rs, :] = pg[:, COL_GK - COL_GQ:COL_GK - COL_GQ + GLA_QK_SLOTS].astype(BF)
        gv_ref[rs, :] = pg[:, COL_GV - COL_GQ:COL_GV - COL_GQ + GLA_SLOTS].astype(BF)
        gg_ref[rs, :] = pg[:, COL_GG - COL_GQ:COL_GG - COL_GQ + GLA_SLOTS].astype(BF)
        z = _dot(pg[:, COL_LR - COL_GQ:COL_LR - COL_GQ + LANE].astype(BF), wgk_ref[...]) + bgk_ref[...]
        ls = (jnp.minimum(z, 0.0) - jnp.log(1.0 + jnp.exp(-jnp.abs(z)))) * (1.0 / GLA_GATE_NORM)
        df_ref[rs, :] = ls[:, :GLA_QK_SLOTS]
        db_ref[rs, :] = ls[:, GLA_QK_SLOTS:]


def _in_proj(x2, mods, pw, cos_t, sin_t, l, row, rope):
    rows = x2.shape[0]
    tm = min(2 * IN_SUB, rows)
    rt = lambda n: pl.BlockSpec((tm, n), lambda i: (i, 0))
    modspec = lambda k: pl.BlockSpec((None, 8, D_MODEL), lambda i: (l, 0, k))
    outs = [(F_WIDTH, BF), (F_WIDTH, BF), None, (MLA_SLOTS, BF), None,
            (GLA_QK_SLOTS, BF), (GLA_QK_SLOTS, BF), (GLA_SLOTS, BF), (GLA_SLOTS, BF), (GLA_QK_SLOTS, F32),
            (GLA_QK_SLOTS, F32)]
    shapes = [jax.ShapeDtypeStruct((rows, o[0]), o[1]) if o else None for o in outs]
    specs = [rt(o[0]) if o else None for o in outs]
    shapes[2] = jax.ShapeDtypeStruct((MLA_SLOTS, rows), BF)
    specs[2] = pl.BlockSpec((MLA_SLOTS, tm), lambda i: (0, i))
    shapes[4] = jax.ShapeDtypeStruct((rows // tm, MLA_HEADS * V_ROWS, tm), BF)
    specs[4] = pl.BlockSpec((None, MLA_HEADS * V_ROWS, tm), lambda i: (i, 0, 0))
    return pl.pallas_call(
        functools.partial(_in_kernel, row=row, rope=rope),
        out_shape=shapes,
        grid=(rows // tm,),
        in_specs=[rt(D_MODEL), modspec(0), modspec(1), _layer((1, D_MODEL), l),
                  _layer((D_MODEL, IN_PAD), l), _full((F_WIDTH, 2 * F_WIDTH)),
                  _layer((1, 256), l), _layer((256, 2 * MLA_SLOTS), l),
                  _layer((1, MLA_KV_LORA), l), _layer((MLA_KV_LORA, 2 * MLA_SLOTS), l),
                  _full((LANE, 2 * MLA_SLOTS)),
                  _layer((1, MLA_SLOTS), l), _layer((1, MLA_SLOTS), l), _layer((1, MLA_SLOTS), l),
                  _layer((1, MLA_SLOTS), l),
                  rt(HEAD_SLOT), rt(HEAD_SLOT),
                  _layer((LANE, 2 * GLA_QK_SLOTS), l), _layer((1, 2 * GLA_QK_SLOTS), l)],
        out_specs=specs,
        compiler_params=_cparams("arbitrary"),
        name="in_proj",
    )(x2, mods, mods, pw["norm1_w"], pw["w_in"], pw["dft64"], pw["q_lora_w"], pw["w_uq"], pw["kv_lora_w"],
      pw["w_ukv"], pw["kpe_place"], pw["q_gain"], pw["q_gain_p"], pw["k_gain"], pw["k_gain_p"],
      cos_t, sin_t, pw["w_gk"], pw["b_gk"])


def _fft1_kernel(a_ref, b_ref, w_ref, g_ref):
    n1 = a_ref.shape[0]
    ab = jnp.concatenate([a_ref[...], b_ref[...]], axis=0)
    g = _dot(w_ref[...], ab)
    g_ref[0] = g[:n1].astype(BF)
    g_ref[1] = g[n1:].astype(BF)


def _fft2_kernel(g_ref, t_ref, o_ref, *, batch):
    for j in range(batch):
        g = jnp.concatenate([g_ref[0, j], g_ref[1, j]], axis=0)
        o_ref[:, j * F_WIDTH:(j + 1) * F_WIDTH] = _dot(t_ref[j], g).astype(BF)


def _fourier_latent(fa, fb, w1, tab):
    length = fa.shape[0]
    n1, n2 = tab.shape[0], tab.shape[1]
    cols = n2 * F_WIDTH
    tn = min(4096, cols)
    g = pl.pallas_call(
        _fft1_kernel,
        out_shape=jax.ShapeDtypeStruct((2, n1, cols), BF),
        grid=(cols // tn,),
        in_specs=[pl.BlockSpec((n1, tn), lambda j: (0, j)), pl.BlockSpec((n1, tn), lambda j: (0, j)),
                  _full((2 * n1, 2 * n1))],
        out_specs=pl.BlockSpec((2, n1, tn), lambda j: (0, 0, j)),
        compiler_params=_cparams("arbitrary"),
        name="fft_stage1",
    )(fa.reshape(n1, cols), fb.reshape(n1, cols), w1)
    batch = 8
    y = pl.pallas_call(
        functools.partial(_fft2_kernel, batch=batch),
        out_shape=jax.ShapeDtypeStruct((n2, n1 * F_WIDTH), BF),
        grid=(n1 // batch,),
        in_specs=[pl.BlockSpec((2, batch, n2, F_WIDTH), lambda i: (0, i, 0, 0)),
                  pl.BlockSpec((batch, n2, 2 * n2), lambda i: (i, 0, 0))],
        out_specs=pl.BlockSpec((n2, batch * F_WIDTH), lambda i: (0, i)),
        compiler_params=_cparams("arbitrary"),
        name="fft_stage2",
    )(g.reshape(2, n1, n2, F_WIDTH), tab)
    return y.reshape(length, F_WIDTH)


def _fctx_kernel(a_ref, b_ref, t_ref, o_ref):
    ab = jnp.concatenate([a_ref[...], b_ref[...]], axis=0)
    o_ref[...] = _dot(t_ref[...], ab).astype(BF)


def _fourier_ctx(fa, fb, tab):
    n = fa.shape[0]
    return pl.pallas_call(
        _fctx_kernel,
        out_shape=jax.ShapeDtypeStruct((n, F_WIDTH), BF),
        grid=(1,),
        in_specs=[_full((n, F_WIDTH)), _full((n, F_WIDTH)), _full((n, 2 * n))],
        out_specs=_full((n, F_WIDTH)),
        compiler_params=_cparams("arbitrary"),
        name="fft_ctx",
    )(fa, fb, tab)


def _attn_kernel(*refs, tk, has_x):
    if has_x:
        q_ref, kc_ref, vc_ref, kx_ref, vx_ref, o_ref, acc_ref, s_ref = refs
    else:
        q_ref, kc_ref, vc_ref, o_ref, acc_ref = refs
    tq = q_ref.shape[1]
    heads = range(q_ref.shape[0] // HEAD_SLOT)
    slot = lambda h: slice(h * HEAD_SLOT, (h + 1) * HEAD_SLOT)
    vrow = lambda h: slice(h * V_ROWS, (h + 1) * V_ROWS)
    qt = [q_ref[slot(h), :] for h in heads]

    def scores(k2):
        return tuple(_dot(k2[:, slot(h)], qt[h]) for h in heads)

    def absorb(s, vts, m):
        sub = vts[0].shape[1]
        m_new = [jnp.maximum(m[h], jnp.max(s[h], axis=0, keepdims=True)) for h in heads]
        p = [jnp.exp2(s[h] - m_new[h]).astype(BF) for h in heads]
        for h in heads:
            alpha = jnp.exp2(m[h] - m_new[h])
            pv = _dot(vts[0][vrow(h), :], p[h][0:sub, :])
            for t in range(1, len(vts)):
                pv += _dot(vts[t][vrow(h), :], p[h][t * sub:(t + 1) * sub, :])
            acc_ref[h] = alpha * acc_ref[h] + pv
        return tuple(m_new)

    acc_ref[...] = jnp.zeros_like(acc_ref)
    m = tuple(jnp.full((1, tq), -1e30, F32) for _ in heads)
    m = absorb(scores(kc_ref[...]), [vc_ref[t] for t in range(vc_ref.shape[0])], m)
    if has_x:
        n = kx_ref.shape[0] // tk
        sub = vx_ref.shape[2]
        assert n % 2 == 0
        unroll = ATTN_UNROLL if n % ATTN_UNROLL == 0 else 2

        def put_scores(buf, j):
            off = pl.multiple_of(j * tk, tk)
            s = scores(kx_ref[pl.ds(off, tk), :])
            for h in heads:
                s_ref[buf, h] = s[h]

        def take(buf, j, u, m):
            if tk >= sub:
                vts = [vx_ref[j * (tk // sub) + t] for t in range(tk // sub)]
            else:
                r = sub // tk
                assert unroll % r == 0
                vts = [vx_ref[j // r, :, (u % r) * tk:(u % r + 1) * tk]]
            return absorb(tuple(s_ref[buf, h] for h in heads), vts, m)

        def body(i, m):
            for u in range(unroll):
                j = unroll * i + u
                nxt = j + 1 if u + 1 < unroll else jnp.minimum(j + 1, n - 1)
                put_scores((u + 1) % 2, nxt)
                m = take(u % 2, j, u, m)
            return m

        put_scores(0, 0)
        m = lax.fori_loop(0, n // unroll, body, m)
    for h in heads:
        acc = acc_ref[h]
        o_ref[h * MLA_V:(h + 1) * MLA_V, :] = (acc[:MLA_V, :] / acc[MLA_V:MLA_V + 1, :]).astype(BF)


def _attention(qt, kc, vct, kx=None, vxt=None):
    rows = qt.shape[1]
    lc = kc.shape[0]
    tq = min(512, rows)
    has_x = kx is not None
    pair = ATTN_HEADS * HEAD_SLOT
    vpair = ATTN_HEADS * V_ROWS
    in_specs = [pl.BlockSpec((pair, tq), lambda p, i: (p, i)),
                pl.BlockSpec((lc, pair), lambda p, i: (0, p)),
                pl.BlockSpec((vct.shape[0], vpair, vct.shape[2]), lambda p, i: (0, p, 0))]
    args = [qt, kc, vct]
    tk = 256
    scratch = [pltpu.VMEM((ATTN_HEADS, V_ROWS, tq), F32)]
    if has_x:
        lx = kx.shape[0]
        tk = min(tk, lx)
        in_specs += [pl.BlockSpec((lx, pair), lambda p, i: (0, p)),
                     pl.BlockSpec((vxt.shape[0], vpair, vxt.shape[2]), lambda p, i: (0, p, 0))]
        args += [kx, vxt]
        scratch.append(pltpu.VMEM((2, ATTN_HEADS, tk, tq), F32))
    return pl.pallas_call(
        functools.partial(_attn_kernel, tk=tk, has_x=has_x),
        out_shape=jax.ShapeDtypeStruct((MLA_WIDTH, rows), BF),
        grid=(MLA_HEADS // ATTN_HEADS, rows // tq),
        in_specs=in_specs,
        out_specs=pl.BlockSpec((ATTN_HEADS * MLA_V, tq), lambda p, i: (p, i)),
        scratch_shapes=scratch,
        compiler_params=_cparams("arbitrary", "arbitrary"),
        name="attention",
    )(*args)


def _gla_kernel(qf_ref, kf_ref, vf_ref, df_ref, qb_ref, kb_ref, vb_ref, db_ref, s0_ref,
                of_ref, ob_ref, sfin_ref, st_ref):
    i = pl.program_id(0)
    t = qf_ref.shape[0]
    c = GLA_CHUNK
    nc = t // c
    assert c & (c - 1) == 0

    @pl.when(i == 0)
    def _():
        st_ref[...] = s0_ref[...]

    r = lax.broadcasted_iota(jnp.int32, (t, t), 0)
    s = lax.broadcasted_iota(jnp.int32, (t, t), 1)
    same_chunk = jnp.bitwise_xor(r, s) < c
    dirs = ((qf_ref, kf_ref, vf_ref, df_ref, of_ref, same_chunk & (s <= r), range(nc), c - 1),
            (qb_ref, kb_ref, vb_ref, db_ref, ob_ref, same_chunk & (s >= r), range(nc - 1, -1, -1), 0))
    heads = range(GLA_HEADS)
    slot = lambda h: slice(h * HEAD_SLOT, (h + 1) * HEAD_SLOT)
    chunk = lambda ch: slice(ch * c, (ch + 1) * c)
    pair = lambda h: slice((h // 2) * HEAD_SLOT, (h // 2 + 1) * HEAD_SLOT)
    lane = lax.broadcasted_iota(jnp.int32, (1, HEAD_SLOT), 1)
    own = [(lane >= (h % 2) * GLA_QK_SLOT) & (lane < (h % 2 + 1) * GLA_QK_SLOT) for h in heads]

    b = []
    for q_ref, k_ref, v_ref, g_ref, o_ref, mask, order, last in dirs:
        tri = mask.astype(BF)
        g = g_ref[...]
        g_hi = g.astype(BF)
        g_r = g - g_hi.astype(F32)
        g_mid = g_r.astype(BF)
        g_lo = (g_r - g_mid.astype(F32)).astype(BF)
        b.append(_dot(tri, g_hi) + _dot(tri, g_mid) + _dot(tri, g_lo))

    work = []
    for d, (q_ref, k_ref, v_ref, g_ref, o_ref, mask, order, last) in enumerate(dirs):
        tot = [b[d][ch * c + last:ch * c + last + 1, :] for ch in range(nc)]
        b_tot = jnp.concatenate([jnp.broadcast_to(tot[ch], (c, GLA_QK_SLOTS)) for ch in range(nc)], axis=0)
        q_in = (q_ref[...].astype(F32) * jnp.exp(b[d])).astype(BF)
        kf = k_ref[...].astype(F32)
        k_in = (kf * jnp.exp(-b[d])).astype(BF)
        k_out = (kf * jnp.exp(b_tot - b[d])).astype(BF)
        v = v_ref[...]
        q_h = [jnp.where(own[h], q_in[:, pair(h)], jnp.zeros_like(q_in[:, pair(h)])) for h in heads]
        ko_h = [jnp.where(own[h], k_out[:, pair(h)], jnp.zeros_like(k_out[:, pair(h)])) for h in heads]
        a = [_dot_nt(q_h[h], k_in[:, pair(h)]) for h in heads]
        inc = [{ch: _dot_tn(v[chunk(ch), slot(h)], ko_h[h][chunk(ch), :]) for ch in order} for h in heads]
        work.append((tot, q_h, v, a, inc))

    o_intra = []
    for d, (q_ref, k_ref, v_ref, g_ref, o_ref, mask, order, last) in enumerate(dirs):
        tot, q_h, v, a, inc = work[d]
        o_intra.append([_dot(jnp.where(mask, a[h], 0.0).astype(BF), v[:, slot(h)]) for h in heads])

    for d, (q_ref, k_ref, v_ref, g_ref, o_ref, mask, order, last) in enumerate(dirs):
        tot, q_h, v, a, inc = work[d]
        entering = []
        for h in heads:
            st = st_ref[d, h]
            ent = {}
            for ch in order:
                ent[ch] = st.astype(BF)
                st = st * jnp.exp(tot[ch][:, pair(h)]) + inc[h][ch]
            st_ref[d, h] = st
            entering.append(ent)
        for h in heads:
            for ch in order:
                o_ref[chunk(ch), slot(h)] = (o_intra[d][h][chunk(ch), :]
                                             + _dot_nt(q_h[h][chunk(ch), :], entering[h][ch])).astype(BF)

    @pl.when(i == pl.num_programs(0) - 1)
    def _():
        sfin_ref[...] = st_ref[...]


def _gla(gq, gk, gv, df, db, s0):
    rows = gq.shape[0]
    c = min(GLA_STEP, rows)
    n = rows // c
    fwd = pl.BlockSpec((c, GLA_SLOTS), lambda i: (i, 0))
    bwd = pl.BlockSpec((c, GLA_SLOTS), lambda i: (n - 1 - i, 0))
    fwd_qk = pl.BlockSpec((c, GLA_QK_SLOTS), lambda i: (i, 0))
    bwd_qk = pl.BlockSpec((c, GLA_QK_SLOTS), lambda i: (n - 1 - i, 0))
    st_shape = (2, GLA_HEADS, HEAD_SLOT, HEAD_SLOT)
    return pl.pallas_call(
        _gla_kernel,
        out_shape=[jax.ShapeDtypeStruct((rows, GLA_SLOTS), BF), jax.ShapeDtypeStruct((rows, GLA_SLOTS), BF),
                   jax.ShapeDtypeStruct(st_shape, F32)],
        grid=(n,),
        in_specs=[fwd_qk, fwd_qk, fwd, fwd_qk, bwd_qk, bwd_qk, bwd, bwd_qk, _full(st_shape)],
        out_specs=[fwd, bwd, _full(st_shape)],
        scratch_shapes=[pltpu.VMEM(st_shape, F32)],
        compiler_params=_cparams("arbitrary"),
        name="gla_scan",
    )(gq, gk, gv, df, gq, gk, gv, db, s0)


def _out_kernel(x_ref, four_ref, att_ref, of_ref, ob_ref, gg_ref, gain_ref, wout_ref, g1_ref, n2_ref,
                sh_ref, sc_ref, xo_ref, h_ref, *, row):
    o = of_ref[...].astype(F32) + ob_ref[...].astype(F32)
    g = gg_ref[...].astype(F32)
    gate = g * (1.0 / (1.0 + jnp.exp(-g)))
    mixed = [four_ref[...]]
    for h in range(GLA_HEADS):
        sl = slice(h * HEAD_SLOT, (h + 1) * HEAD_SLOT)
        oh = o[:, sl]
        mixed.append((oh * _rms(oh, GLA_DV) * gain_ref[:, sl] * gate[:, sl]).astype(BF))
    tok = F_WIDTH + GLA_SLOTS
    y = _dot(jnp.concatenate(mixed, axis=1), wout_ref[0:tok, :])
    y += _dot_tn(att_ref[...], wout_ref[tok:tok + MLA_WIDTH, :])
    x = x_ref[...] + g1_ref[row:row + 1, :] * y
    xo_ref[...] = x
    hn = x * _rms(x, D_MODEL) * n2_ref[...]
    h_ref[...] = (hn * (1.0 + sc_ref[row:row + 1, :]) + sh_ref[row:row + 1, :]).astype(BF)


def _out_proj(x2, four, att, o_f, o_b, gg, mods, pw, l, row):
    rows = x2.shape[0]
    tm = min(ROW_TILE, rows)
    rt = lambda n: pl.BlockSpec((tm, n), lambda i: (i, 0))
    modspec = lambda k: pl.BlockSpec((None, 8, D_MODEL), lambda i: (l, 0, k))
    return pl.pallas_call(
        functools.partial(_out_kernel, row=row),
        out_shape=[jax.ShapeDtypeStruct((rows, D_MODEL), F32), jax.ShapeDtypeStruct((rows, D_MODEL), BF)],
        grid=(rows // tm,),
        in_specs=[rt(D_MODEL), rt(F_WIDTH), pl.BlockSpec((MLA_WIDTH, tm), lambda i: (0, i)),
                  rt(GLA_SLOTS), rt(GLA_SLOTS), rt(GLA_SLOTS),
                  _layer((1, GLA_SLOTS), l), _layer((MIX_PAD, D_MODEL), l), modspec(2),
                  _layer((1, D_MODEL), l), modspec(3), modspec(4)],
        out_specs=[rt(D_MODEL), rt(D_MODEL)],
        compiler_params=_cparams("arbitrary"),
        name="out_proj",
    )(x2, four, att, o_f, o_b, gg, pw["gla_gain"], pw["w_out"], mods, pw["norm2_w"], mods, mods)


def _ffn_kernel(x_ref, h_ref, hp_ref, hn_ref, w12_ref, conv_ref, wd_ref, g2_ref, o_ref, acc_ref, hx_ref, u_ref,
                *, row):
    i = pl.program_id(0)
    tm = h_ref.shape[0]
    m = tm + 2 * FF_HALO
    keep_prev = jnp.where(i > 0, 1.0, 0.0)
    keep_next = jnp.where(i < pl.num_programs(0) - 1, 1.0, 0.0)
    hx_ref[0:FF_HALO, :] = (hp_ref[...].astype(F32) * keep_prev).astype(BF)
    hx_ref[FF_HALO:FF_HALO + tm, :] = h_ref[...]
    hx_ref[FF_HALO + tm:m, :] = (hn_ref[...].astype(F32) * keep_next).astype(BF)
    acc_ref[...] = jnp.zeros_like(acc_ref)

    starts = [sum(FF_CHUNKS[:c]) for c in range(len(FF_CHUNKS))]

    def cols(c, half):
        lo = half * D_FF + starts[c]
        return slice(lo, lo + FF_CHUNKS[c])

    def up(buf, c):
        for half in range(2):
            u_ref[buf, half, :, 0:FF_CHUNKS[c]] = _dot(hx_ref[...], w12_ref[:, cols(c, half)])

    def conv(buf, c, half):
        u = u_ref[buf, half, :, 0:FF_CHUNKS[c]]
        cw = conv_ref[:, cols(c, half)]
        u_prev = pltpu.roll(u, 1, axis=0)
        u_next = pltpu.roll(u, m - 1, axis=0)
        uc = u_prev * cw[0:1, :] + u * cw[1:2, :] + u_next * cw[2:3, :] + cw[3:4, :]
        return uc[FF_HALO:FF_HALO + tm, :]

    def down(buf, c):
        a = conv(buf, c, 0)
        act = a * (1.0 / (1.0 + jnp.exp(-a))) * conv(buf, c, 1)
        acc_ref[...] += _dot(act.astype(BF), wd_ref[starts[c]:starts[c] + FF_CHUNKS[c], :])

    nbuf = u_ref.shape[0]
    nchunk = len(FF_CHUNKS)
    for c in range(min(FF_AHEAD, nchunk)):
        up(c % nbuf, c)
    for c in range(nchunk):
        if c + FF_AHEAD < nchunk:
            up((c + FF_AHEAD) % nbuf, c + FF_AHEAD)
        down(c % nbuf, c)
    o_ref[...] = x_ref[...] + g2_ref[row:row + 1, :] * acc_ref[...]


def _ffn(x2, h2, mods, pw, l, row):
    rows = x2.shape[0]
    tm = min(512, rows)
    nt = rows // tm
    per = tm // FF_HALO
    last_blk = rows // FF_HALO - 1
    m = tm + 2 * FF_HALO
    rt = lambda n: pl.BlockSpec((tm, n), lambda i: (i, 0))
    once = lambda shape: pl.BlockSpec((None,) + shape, lambda i: (l,) + (0,) * len(shape),
                                      pipeline_mode=pl.Buffered(1))
    return pl.pallas_call(
        functools.partial(_ffn_kernel, row=row),
        out_shape=jax.ShapeDtypeStruct((rows, D_MODEL), F32),
        grid=(nt,),
        in_specs=[rt(D_MODEL), rt(D_MODEL),
                  pl.BlockSpec((FF_HALO, D_MODEL), lambda i: (jnp.maximum(i * per - 1, 0), 0)),
                  pl.BlockSpec((FF_HALO, D_MODEL), lambda i: (jnp.minimum((i + 1) * per, last_blk), 0)),
                  once((D_MODEL, 2 * D_FF)), once((4, 2 * D_FF)), once((D_FF, D_MODEL)),
                  pl.BlockSpec((None, 8, D_MODEL), lambda i: (l, 0, 5))],
        out_specs=rt(D_MODEL),
        scratch_shapes=[pltpu.VMEM((tm, D_MODEL), F32), pltpu.VMEM((m, D_MODEL), BF),
                        pltpu.VMEM((FF_AHEAD + 1, 2, m, max(FF_CHUNKS)), F32)],
        compiler_params=_cparams("arbitrary"),
        name="conv_ffn",
    )(x2, h2, h2, h2, pw["w12"], pw["conv"], pw["w_down"], mods)


def kernel(x, c, ctx, c_ctx, mod_w, mod_b, norm1_w, norm2_w, w_in, mla_q_lora_norm_w, mla_w_uq, mla_kv_lora_norm_w, mla_w_ukv, mla_q_norm_w, mla_k_norm_w, gla_w_gk_fwd, gla_b_gk_fwd, gla_w_gk_bwd, gla_b_gk_bwd, gla_norm_w, w_out, ffn_w_12, ffn_conv_w, ffn_conv_b, ffn_w_down):
    batch, seq, d = x.shape
    assert batch == 1 and d == D_MODEL
    lc = ctx.shape[1]
    depth = mod_w.shape[0]
    n1 = int(round(math.sqrt(seq)))
    assert n1 * n1 == seq and seq % ROW_TILE == 0 and lc % GLA_CHUNK == 0

    pw = _prep_weights(dict(
        w_in=w_in, mla_w_uq=mla_w_uq, mla_q_lora_norm_w=mla_q_lora_norm_w, mla_kv_lora_norm_w=mla_kv_lora_norm_w,
        mla_w_ukv=mla_w_ukv, mla_q_norm_w=mla_q_norm_w, mla_k_norm_w=mla_k_norm_w, gla_w_gk_fwd=gla_w_gk_fwd,
        gla_b_gk_fwd=gla_b_gk_fwd, gla_w_gk_bwd=gla_w_gk_bwd, gla_b_gk_bwd=gla_b_gk_bwd, gla_norm_w=gla_norm_w,
        w_out=w_out, ffn_w_12=ffn_w_12, ffn_conv_w=ffn_conv_w, ffn_conv_b=ffn_conv_b, ffn_w_down=ffn_w_down,
        norm1_w=norm1_w, norm2_w=norm2_w))
    cos_t, sin_t = _rope_tables(seq)
    ones_c = jnp.ones((lc, HEAD_SLOT), F32)
    w1, tab = _fft_tables(n1, n1)
    tab_c = _ctx_dft_table(lc)

    cond8 = jnp.zeros((8, d), F32).at[0].set(c[0].astype(F32)).at[1].set(c_ctx.astype(F32))
    mods = _adaln(cond8, mod_w, mod_b)

    xs = x[0].astype(F32)
    xc = ctx[0].astype(F32)
    s_zero = jnp.zeros((2, GLA_HEADS, HEAD_SLOT, HEAD_SLOT), F32)
    for l in range(depth):
        last = l == depth - 1
        fa_c, fb_c, q_c, k_c, v_c, gq_c, gk_c, gv_c, gg_c, df_c, db_c = _in_proj(xc, mods, pw, ones_c, ones_c, l, 1, False)
        fa_x, fb_x, q_x, k_x, v_x, gq_x, gk_x, gv_x, gg_x, df_x, db_x = _in_proj(xs, mods, pw, cos_t, sin_t, l, 0, True)

        of_c, ob_c, s_c = _gla(gq_c, gk_c, gv_c, df_c, db_c, s_zero)
        of_x, ob_x, _ = _gla(gq_x, gk_x, gv_x, df_x, db_x, s_c)
        four_x = _fourier_latent(fa_x, fb_x, w1, tab)
        att_x = _attention(q_x, k_c, v_c, k_x, v_x)
        x_mid, h2 = _out_proj(xs, four_x, att_x, of_x, ob_x, gg_x, mods, pw, l, 0)
        xs = _ffn(x_mid, h2, mods, pw, l, 0)
        if not last:
            four_c = _fourier_ctx(fa_c, fb_c, tab_c)
            att_c = _attention(q_c, k_c, v_c)
            c_mid, hc2 = _out_proj(xc, four_c, att_c, of_c, ob_c, gg_c, mods, pw, l, 1)
            xc = _ffn(c_mid, hc2, mods, pw, l, 1)
    return xs[None].astype(x.dtype)
```

```python
import functools
import math

import numpy as np
import jax
import jax.numpy as jnp
from jax import lax
from jax.experimental import pallas as pl
from jax.experimental.pallas import tpu as pltpu

D_MODEL = 1024
DEPTH = 2
GRID_W = 64
EPS = 1e-6

F_GROUPS = 4
F_DIM = 64
F_WIDTH = F_GROUPS * F_DIM

MLA_HEADS = 6
MLA_Q_LORA = 192
MLA_KV_LORA = 128
MLA_NOPE = 64
MLA_ROPE = 32
MLA_V = 64
MLA_QK = MLA_NOPE + MLA_ROPE
MLA_WIDTH = MLA_HEADS * MLA_V
MLA_IN = MLA_Q_LORA + MLA_KV_LORA + MLA_ROPE
ROPE_BASE = 10000.0

GLA_HEADS = 4
GLA_DK = 48
GLA_DV = 96
GLA_GATE_RANK = 16
GLA_GATE_NORM = 16.0
GLA_CHUNK = 64
GLA_STEP = 256
GLA_WIDTH = GLA_HEADS * GLA_DV
GLA_QK_W = GLA_HEADS * GLA_DK

MIX_WIDTH = F_WIDTH + MLA_WIDTH + GLA_WIDTH
D_FF = 2816
N_MOD = 6

LANE = 128
SUBLANE_BF16 = 16
VMEM_LIMIT = 48 * 1024 * 1024

HEAD_SLOT = LANE
V_ROWS = 80
MLA_SLOTS = MLA_HEADS * HEAD_SLOT
GLA_SLOTS = GLA_HEADS * HEAD_SLOT
GLA_QK_SLOT = 64
GLA_QK_SLOTS = GLA_HEADS * GLA_QK_SLOT

COL_F = 0
COL_CQ = COL_F + F_WIDTH
COL_CKV = COL_CQ + 256
COL_KPE = COL_CKV + MLA_KV_LORA
COL_GQ = COL_KPE + LANE
COL_GK = COL_GQ + GLA_QK_SLOTS
COL_GV = COL_GK + GLA_QK_SLOTS
COL_GG = COL_GV + GLA_SLOTS
COL_LR = COL_GG + GLA_SLOTS
IN_PAD = COL_LR + LANE

MIX_PAD = F_WIDTH + MLA_WIDTH + GLA_SLOTS

FF_CHUNKS = (768, 768, 768, 512)
assert sum(FF_CHUNKS) == D_FF
FF_HALO = SUBLANE_BF16
FF_AHEAD = 1

LOG2E = 1.4426950408889634

ROW_TILE = 256
IN_SUB = 256
ATTN_UNROLL = 32
ATTN_HEADS = 2
BF = jnp.bfloat16
F32 = jnp.float32


def _cparams(*sem):
    return pltpu.CompilerParams(dimension_semantics=sem, vmem_limit_bytes=VMEM_LIMIT)


def _dot(a, b):
    return jnp.dot(a, b, preferred_element_type=F32)


def _dot_nt(a, b):
    return lax.dot_general(a, b, (((1,), (1,)), ((), ())), preferred_element_type=F32)


def _dot_tn(a, b):
    return lax.dot_general(a, b, (((0,), (0,)), ((), ())), preferred_element_type=F32)


def _full(shape):
    n = len(shape)
    return pl.BlockSpec(shape, lambda *_: (0,) * n)


def _layer(shape, l):
    n = len(shape)
    return pl.BlockSpec((None,) + tuple(shape), lambda *_: (l,) + (0,) * n)


def _rot_partner(n_rope):
    q = n_rope // 4
    src = np.zeros(n_rope, np.int64)
    sgn = np.zeros(n_rope, np.float32)
    for base in (0, 2 * q):
        for j in range(q):
            src[base + j] = base + q + j
            sgn[base + j] = -1.0
            src[base + q + j] = base + j
            sgn[base + q + j] = 1.0
    return src, sgn


def _take_cols(w, src, sgn=None, axis=-1):
    src = np.asarray(src)
    sgn = np.ones(len(src), np.float32) if sgn is None else np.asarray(sgn, np.float32)
    axis = axis % w.ndim
    pieces = []
    lo = 0
    while lo < len(src):
        hi = lo + 1
        if src[lo] < 0:
            while hi < len(src) and src[hi] < 0:
                hi += 1
            shape = list(w.shape)
            shape[axis] = hi - lo
            pieces.append(jnp.zeros(shape, w.dtype))
        else:
            while hi < len(src) and src[hi] == src[hi - 1] + 1 and sgn[hi] == sgn[lo]:
                hi += 1
            piece = lax.slice_in_dim(w, int(src[lo]), int(src[lo]) + hi - lo, axis=axis)
            pieces.append(-piece if sgn[lo] < 0 else piece)
        lo = hi
    return jnp.concatenate(pieces, axis=axis)


def _pad_rows(w, n):
    pad = [(0, 0)] * w.ndim
    pad[-2] = (0, n - w.shape[-2])
    return jnp.pad(w, pad)


def _in_proj_layout():
    src = -np.ones(IN_PAD, np.int64)
    sgn = np.ones(IN_PAD, np.float32)
    src[COL_F:COL_F + F_WIDTH] = np.arange(F_WIDTH)
    o = F_WIDTH
    src[COL_CQ:COL_CQ + MLA_Q_LORA] = o + np.arange(MLA_Q_LORA)
    o += MLA_Q_LORA
    src[COL_CKV:COL_CKV + MLA_KV_LORA] = o + np.arange(MLA_KV_LORA)
    o += MLA_KV_LORA
    src[COL_KPE:COL_KPE + MLA_ROPE] = o + np.arange(MLA_ROPE)
    psrc, psgn = _rot_partner(MLA_ROPE)
    src[COL_KPE + MLA_ROPE:COL_KPE + 2 * MLA_ROPE] = o + psrc
    sgn[COL_KPE + MLA_ROPE:COL_KPE + 2 * MLA_ROPE] = psgn
    o += MLA_ROPE
    for h in range(GLA_HEADS):
        src[COL_GQ + h * GLA_QK_SLOT:COL_GQ + h * GLA_QK_SLOT + GLA_DK] = o + h * GLA_DK + np.arange(GLA_DK)
    o += GLA_QK_W
    for h in range(GLA_HEADS):
        src[COL_GK + h * GLA_QK_SLOT:COL_GK + h * GLA_QK_SLOT + GLA_DK] = o + h * GLA_DK + np.arange(GLA_DK)
    o += GLA_QK_W
    for h in range(GLA_HEADS):
        src[COL_GV + h * HEAD_SLOT:COL_GV + h * HEAD_SLOT + GLA_DV] = o + h * GLA_DV + np.arange(GLA_DV)
    o += GLA_WIDTH
    src[COL_LR:COL_LR + GLA_GATE_RANK] = o + np.arange(GLA_GATE_RANK)
    o += GLA_GATE_RANK
    for h in range(GLA_HEADS):
        src[COL_GG + h * HEAD_SLOT:COL_GG + h * HEAD_SLOT + GLA_DV] = o + h * GLA_DV + np.arange(GLA_DV)
    return src, sgn


def _head_slots(n_heads, d_src, d_take, src_off=0):
    src = -np.ones(n_heads * HEAD_SLOT, np.int64)
    for h in range(n_heads):
        src[h * HEAD_SLOT:h * HEAD_SLOT + d_take] = h * d_src + src_off + np.arange(d_take)
    return src


def _prep_weights(w):
    f32 = lambda a: a.astype(F32)
    out = {}
    src, sgn = _in_proj_layout()
    out["w_in"] = _take_cols(w["w_in"], src, sgn).astype(BF)

    m = np.arange(F_DIM)
    ang = 2.0 * np.pi * np.outer(m, m) / F_DIM
    c64, s64 = np.cos(ang), np.sin(ang)
    dft = np.zeros((F_WIDTH, 2 * F_WIDTH), np.float32)
    for g in range(F_GROUPS):
        dft[g * F_DIM:(g + 1) * F_DIM, g * F_DIM:(g + 1) * F_DIM] = c64
        dft[g * F_DIM:(g + 1) * F_DIM, F_WIDTH + g * F_DIM:F_WIDTH + (g + 1) * F_DIM] = s64
    out["dft64"] = jnp.asarray(dft, F32).astype(BF)

    psrc, psgn = _rot_partner(MLA_ROPE)
    q_src = _head_slots(MLA_HEADS, MLA_QK, MLA_QK)
    q_part = -np.ones(MLA_SLOTS, np.int64)
    q_psg = np.ones(MLA_SLOTS, np.float32)
    g_part = -np.ones(MLA_SLOTS, np.int64)
    for h in range(MLA_HEADS):
        lo = h * HEAD_SLOT + MLA_NOPE
        q_part[lo:lo + MLA_ROPE] = h * MLA_QK + MLA_NOPE + psrc
        q_psg[lo:lo + MLA_ROPE] = psgn
        g_part[lo:lo + MLA_ROPE] = MLA_NOPE + psrc
    g_src = np.where(q_src >= 0, q_src % MLA_QK, -1)
    wq = jnp.concatenate([_take_cols(w["mla_w_uq"], q_src), _take_cols(w["mla_w_uq"], q_part, q_psg)], axis=-1)
    out["w_uq"] = _pad_rows(wq, 256).astype(BF)
    out["q_lora_w"] = jnp.pad(f32(w["mla_q_lora_norm_w"]), ((0, 0), (0, 256 - MLA_Q_LORA)))[:, None, :]
    out["kv_lora_w"] = f32(w["mla_kv_lora_norm_w"])[:, None, :]
    out["q_gain"] = _take_cols(w["mla_q_norm_w"], g_src)[:, None, :]
    out["q_gain_p"] = _take_cols(w["mla_q_norm_w"], g_part)[:, None, :]
    out["k_gain"] = _take_cols(w["mla_k_norm_w"], g_src)[:, None, :]
    out["k_gain_p"] = _take_cols(w["mla_k_norm_w"], g_part)[:, None, :]

    kn_src = _head_slots(MLA_HEADS, MLA_NOPE + MLA_V, MLA_NOPE)
    v_src = _head_slots(MLA_HEADS, MLA_NOPE + MLA_V, MLA_V, MLA_NOPE)
    nd = w["mla_w_ukv"].shape[0]
    top = jnp.concatenate([_take_cols(w["mla_w_ukv"], kn_src), _take_cols(w["mla_w_ukv"], v_src),
                           jnp.zeros((nd, MLA_KV_LORA, MLA_SLOTS), w["mla_w_ukv"].dtype)], axis=-1)
    e2 = np.zeros((LANE, 3 * MLA_SLOTS), np.float32)
    for h in range(MLA_HEADS):
        for j in range(MLA_ROPE):
            e2[j, h * HEAD_SLOT + MLA_NOPE + j] = 1.0
            e2[MLA_ROPE + j, 2 * MLA_SLOTS + h * HEAD_SLOT + MLA_NOPE + j] = 1.0
    place = jnp.broadcast_to(jnp.asarray(e2, top.dtype), (nd,) + e2.shape)
    out["w_ukv"] = jnp.concatenate([top, place], axis=-2).astype(BF)

    gk_src = -np.ones(GLA_QK_SLOTS, np.int64)
    for h in range(GLA_HEADS):
        gk_src[h * GLA_QK_SLOT:h * GLA_QK_SLOT + GLA_DK] = h * GLA_DK + np.arange(GLA_DK)
    wgk = jnp.concatenate([_take_cols(w["gla_w_gk_fwd"], gk_src), _take_cols(w["gla_w_gk_bwd"], gk_src)], axis=-1)
    out["w_gk"] = _pad_rows(wgk, LANE).astype(BF)
    out["b_gk"] = jnp.concatenate([_take_cols(w["gla_b_gk_fwd"], gk_src), _take_cols(w["gla_b_gk_bwd"], gk_src)],
                                  axis=-1)[:, None, :]
    gv_src = _head_slots(GLA_HEADS, GLA_DV, GLA_DV)
    out["gla_gain"] = _take_cols(w["gla_norm_w"], np.where(gv_src >= 0, gv_src % GLA_DV, -1))[:, None, :]

    row_src = np.concatenate([np.arange(F_WIDTH), np.where(gv_src >= 0, F_WIDTH + MLA_WIDTH + gv_src, -1),
                              F_WIDTH + np.arange(MLA_WIDTH)])
    out["w_out"] = _take_cols(w["w_out"], row_src, axis=-2).astype(BF)

    nd = w["ffn_w_12"].shape[0]
    out["w12"] = w["ffn_w_12"].astype(BF)
    out["conv"] = jnp.concatenate([w["ffn_conv_w"], w["ffn_conv_b"][:, None, :]], axis=1).astype(F32)
    out["w_down"] = w["ffn_w_down"].astype(BF)
    out["norm1_w"] = f32(w["norm1_w"])[:, None, :]
    out["norm2_w"] = f32(w["norm2_w"])[:, None, :]
    return out


def _rope_tables(n_tokens):
    q = MLA_ROPE // 4
    t = np.arange(n_tokens)
    row = (t // GRID_W).astype(np.float32)
    col = (t % GRID_W).astype(np.float32)
    axis_dims = MLA_ROPE // 2
    inv_freq = np.power(np.float32(ROPE_BASE), -np.arange(0, axis_dims, 2, dtype=np.float32) / axis_dims)
    ang_r = row[:, None] * inv_freq
    ang_c = col[:, None] * inv_freq
    cos = np.ones((n_tokens, HEAD_SLOT), np.float32)
    sin = np.zeros((n_tokens, HEAD_SLOT), np.float32)
    for k, ang in enumerate((ang_r, ang_r, ang_c, ang_c)):
        lo = MLA_NOPE + k * q
        cos[:, lo:lo + q] = np.cos(ang)
        sin[:, lo:lo + q] = np.sin(ang)
    return jnp.asarray(cos), jnp.asarray(sin)


def _fft_tables(n1, n2):
    k = np.arange(n1)
    ang1 = 2.0 * np.pi * np.outer(k, k) / n1
    fr, fi = np.cos(ang1), -np.sin(ang1)
    w1 = np.block([[fr, fi], [fi, -fr]])
    length = n1 * n2
    kk = (np.arange(n1)[:, None] + n1 * np.arange(n2)[None, :]).astype(np.float64)
    t2 = np.arange(n2, dtype=np.float64)
    ang2 = 2.0 * np.pi * kk[:, :, None] * t2[None, None, :] / length
    norm = 1.0 / math.sqrt(length * F_DIM)
    tab = np.concatenate([np.cos(ang2), np.sin(ang2)], axis=-1) * norm
    return jnp.asarray(w1, F32).astype(BF), jnp.asarray(tab, F32).astype(BF)


def _ctx_dft_table(n):
    k = np.arange(n)
    ang = 2.0 * np.pi * np.outer(k, k) / n
    norm = 1.0 / math.sqrt(n * F_DIM)
    return jnp.asarray(np.concatenate([np.cos(ang), -np.sin(ang)], axis=1) * norm, F32).astype(BF)


def _mod_kernel(c_ref, w_ref, b_ref, o_ref):
    c = c_ref[...]
    s = c * (1.0 / (1.0 + jnp.exp(-c)))
    o_ref[...] = _dot(s.astype(BF), w_ref[...].astype(BF)) + b_ref[...]


def _adaln(cond8, mod_w, mod_b):
    nd, d, n = mod_w.shape
    tn = 1536
    return pl.pallas_call(
        _mod_kernel,
        out_shape=jax.ShapeDtypeStruct((nd, 8, n), F32),
        grid=(nd, n // tn),
        in_specs=[pl.BlockSpec((8, d), lambda l, j: (0, 0)),
                  pl.BlockSpec((None, d, tn), lambda l, j: (l, 0, j)),
                  pl.BlockSpec((None, 1, tn), lambda l, j: (l, 0, j))],
        out_specs=pl.BlockSpec((None, 8, tn), lambda l, j: (l, 0, j)),
        compiler_params=_cparams("arbitrary", "arbitrary"),
        name="adaln",
    )(cond8, mod_w, mod_b.reshape(nd, 1, n))


def _rms(x, n):
    return lax.rsqrt(jnp.sum(x * x, axis=-1, keepdims=True) * (1.0 / n) + EPS)


def _in_kernel(x_ref, sh_ref, sc_ref, n1_ref, win_ref, dft_ref, qlw_ref, wq_ref, kvlw_ref, wkv_ref,
               qg_ref, qgp_ref, kg_ref, kgp_ref, cos_ref, sin_ref, wgk_ref, bgk_ref,
               fa_ref, fb_ref, q_ref, k_ref, v_ref, gq_ref, gk_ref, gv_ref, gg_ref, df_ref, db_ref,
               *, row, rope):
    for r0 in range(0, x_ref.shape[0], IN_SUB):
        rs = slice(r0, min(r0 + IN_SUB, x_ref.shape[0]))
        x = x_ref[rs, :]
        xn = x * _rms(x, D_MODEL) * n1_ref[...]
        h = xn * (1.0 + sc_ref[row:row + 1, :]) + sh_ref[row:row + 1, :]
        hb = h.astype(BF)
        p = _dot(hb, win_ref[:, :COL_GQ])

        ab = _dot(p[:, COL_F:COL_F + F_WIDTH].astype(BF), dft_ref[...])
        fa_ref[rs, :] = ab[:, :F_WIDTH].astype(BF)
        fb_ref[rs, :] = ab[:, F_WIDTH:].astype(BF)

        cq = p[:, COL_CQ:COL_CQ + 256]
        cqn = cq * _rms(cq, MLA_Q_LORA) * qlw_ref[...]
        qq = _dot(cqn.astype(BF), wq_ref[...])
        ckv = p[:, COL_CKV:COL_CKV + MLA_KV_LORA]
        ckvn = ckv * _rms(ckv, MLA_KV_LORA) * kvlw_ref[...]
        kv_in = jnp.concatenate([ckvn.astype(BF), p[:, COL_KPE:COL_KPE + LANE].astype(BF)], axis=1)
        kk = _dot(kv_in, wkv_ref[...])
        pg = _dot(hb, win_ref[:, COL_GQ:])
        v_lane = lax.broadcasted_iota(jnp.int32, (x.shape[0], HEAD_SLOT), 1)

        q_scale = (MLA_QK ** -0.5) * LOG2E
        for hd in range(MLA_HEADS):
            sl = slice(hd * HEAD_SLOT, (hd + 1) * HEAD_SLOT)
            sp = slice(MLA_SLOTS + hd * HEAD_SLOT, MLA_SLOTS + (hd + 1) * HEAD_SLOT)
            qh = qq[:, sl]
            kh = kk[:, sl]
            kpart = kk[:, 2 * MLA_SLOTS + hd * HEAD_SLOT:2 * MLA_SLOTS + (hd + 1) * HEAD_SLOT]
            rq = _rms(qh, MLA_QK) * q_scale
            rk = _rms(kh, MLA_QK)
            if rope:
                cos = cos_ref[rs, :]
                sin = sin_ref[rs, :]
                qo = rq * (qh * (qg_ref[:, sl] * cos) + qq[:, sp] * (qgp_ref[:, sl] * sin))
                ko = rk * (kh * (kg_ref[:, sl] * cos) + kpart * (kgp_ref[:, sl] * sin))
            else:
                qo = rq * (qh * qg_ref[:, sl])
                ko = rk * (kh * kg_ref[:, sl])
            q_ref[sl, rs] = qo.T.astype(BF)
            k_ref[rs, sl] = ko.astype(BF)
            vh = jnp.where(v_lane == MLA_V, 1.0, kk[:, MLA_SLOTS + hd * HEAD_SLOT:MLA_SLOTS + (hd + 1) * HEAD_SLOT])
            v_ref[hd * V_ROWS:(hd + 1) * V_ROWS, rs] = vh.T[:V_ROWS, :].astype(BF)

        gq_ref[rs, :] = (pg[:, 0:GLA_QK_SLOTS] * (GLA_DK ** -0.5)).astype(BF)
        gk_ref[rs, :] = pg[:, COL_GK - COL_GQ:COL_GK - COL_GQ + GLA_QK_SLOTS].astype(BF)
        gv_ref[rs, :] = pg[:, COL_GV - COL_GQ:COL_GV - COL_GQ + GLA_SLOTS].astype(BF)
        gg_ref[rs, :] = pg[:, COL_GG - COL_GQ:COL_GG - COL_GQ + GLA_SLOTS].astype(BF)
        z = _dot(pg[:, COL_LR - COL_GQ:COL_LR - COL_GQ + LANE].astype(BF), wgk_ref[...]) + bgk_ref[...]
        ls = (jnp.minimum(z, 0.0) - jnp.log(1.0 + jnp.exp(-jnp.abs(z)))) * (1.0 / GLA_GATE_NORM)
        df_ref[rs, :] = ls[:, :GLA_QK_SLOTS]
        db_ref[rs, :] = ls[:, GLA_QK_SLOTS:]


def _in_proj(x2, mods, pw, cos_t, sin_t, l, row, rope):
    rows = x2.shape[0]
    tm = min(2 * IN_SUB, rows)
    rt = lambda n: pl.BlockSpec((tm, n), lambda i: (i, 0))
    modspec = lambda k: pl.BlockSpec((None, 8, D_MODEL), lambda i: (l, 0, k))
    outs = [(F_WIDTH, BF), (F_WIDTH, BF), None, (MLA_SLOTS, BF), None,
            (GLA_QK_SLOTS, BF), (GLA_QK_SLOTS, BF), (GLA_SLOTS, BF), (GLA_SLOTS, BF), (GLA_QK_SLOTS, F32),
            (GLA_QK_SLOTS, F32)]
    shapes = [jax.ShapeDtypeStruct((rows, o[0]), o[1]) if o else None for o in outs]
    specs = [rt(o[0]) if o else None for o in outs]
    shapes[2] = jax.ShapeDtypeStruct((MLA_SLOTS, rows), BF)
    specs[2] = pl.BlockSpec((MLA_SLOTS, tm), lambda i: (0, i))
    shapes[4] = jax.ShapeDtypeStruct((rows // tm, MLA_HEADS * V_ROWS, tm), BF)
    specs[4] = pl.BlockSpec((None, MLA_HEADS * V_ROWS, tm), lambda i: (i, 0, 0))
    return pl.pallas_call(
        functools.partial(_in_kernel, row=row, rope=rope),
        out_shape=shapes,
        grid=(rows // tm,),
        in_specs=[rt(D_MODEL), modspec(0), modspec(1), _layer((1, D_MODEL), l),
                  _layer((D_MODEL, IN_PAD), l), _full((F_WIDTH, 2 * F_WIDTH)),
                  _layer((1, 256), l), _layer((256, 2 * MLA_SLOTS), l),
                  _layer((1, MLA_KV_LORA), l), _layer((MLA_KV_LORA + LANE, 3 * MLA_SLOTS), l),
                  _layer((1, MLA_SLOTS), l), _layer((1, MLA_SLOTS), l), _layer((1, MLA_SLOTS), l),
                  _layer((1, MLA_SLOTS), l),
                  rt(HEAD_SLOT), rt(HEAD_SLOT),
                  _layer((LANE, 2 * GLA_QK_SLOTS), l), _layer((1, 2 * GLA_QK_SLOTS), l)],
        out_specs=specs,
        compiler_params=_cparams("arbitrary"),
        name="in_proj",
    )(x2, mods, mods, pw["norm1_w"], pw["w_in"], pw["dft64"], pw["q_lora_w"], pw["w_uq"], pw["kv_lora_w"],
      pw["w_ukv"], pw["q_gain"], pw["q_gain_p"], pw["k_gain"], pw["k_gain_p"],
      cos_t, sin_t, pw["w_gk"], pw["b_gk"])


def _fft1_kernel(a_ref, b_ref, w_ref, g_ref):
    n1 = a_ref.shape[0]
    ab = jnp.concatenate([a_ref[...], b_ref[...]], axis=0)
    g = _dot(w_ref[...], ab)
    g_ref[0] = g[:n1].astype(BF)
    g_ref[1] = g[n1:].astype(BF)


def _fft2_kernel(g_ref, t_ref, o_ref, *, batch):
    for j in range(batch):
        g = jnp.concatenate([g_ref[0, j], g_ref[1, j]], axis=0)
        o_ref[:, j * F_WIDTH:(j + 1) * F_WIDTH] = _dot(t_ref[j], g).astype(BF)


def _fourier_latent(fa, fb, w1, tab):
    length = fa.shape[0]
    n1, n2 = tab.shape[0], tab.shape[1]
    cols = n2 * F_WIDTH
    tn = min(4096, cols)
    g = pl.pallas_call(
        _fft1_kernel,
        out_shape=jax.ShapeDtypeStruct((2, n1, cols), BF),
        grid=(cols // tn,),
        in_specs=[pl.BlockSpec((n1, tn), lambda j: (0, j)), pl.BlockSpec((n1, tn), lambda j: (0, j)),
                  _full((2 * n1, 2 * n1))],
        out_specs=pl.BlockSpec((2, n1, tn), lambda j: (0, 0, j)),
        compiler_params=_cparams("arbitrary"),
        name="fft_stage1",
    )(fa.reshape(n1, cols), fb.reshape(n1, cols), w1)
    batch = 8
    y = pl.pallas_call(
        functools.partial(_fft2_kernel, batch=batch),
        out_shape=jax.ShapeDtypeStruct((n2, n1 * F_WIDTH), BF),
        grid=(n1 // batch,),
        in_specs=[pl.BlockSpec((2, batch, n2, F_WIDTH), lambda i: (0, i, 0, 0)),
                  pl.BlockSpec((batch, n2, 2 * n2), lambda i: (i, 0, 0))],
        out_specs=pl.BlockSpec((n2, batch * F_WIDTH), lambda i: (0, i)),
        compiler_params=_cparams("arbitrary"),
        name="fft_stage2",
    )(g.reshape(2, n1, n2, F_WIDTH), tab)
    return y.reshape(length, F_WIDTH)


def _fctx_kernel(a_ref, b_ref, t_ref, o_ref):
    ab = jnp.concatenate([a_ref[...], b_ref[...]], axis=0)
    o_ref[...] = _dot(t_ref[...], ab).astype(BF)


def _fourier_ctx(fa, fb, tab):
    n = fa.shape[0]
    return pl.pallas_call(
        _fctx_kernel,
        out_shape=jax.ShapeDtypeStruct((n, F_WIDTH), BF),
        grid=(1,),
        in_specs=[_full((n, F_WIDTH)), _full((n, F_WIDTH)), _full((n, 2 * n))],
        out_specs=_full((n, F_WIDTH)),
        compiler_params=_cparams("arbitrary"),
        name="fft_ctx",
    )(fa, fb, tab)


def _attn_kernel(*refs, tk, has_x):
    if has_x:
        q_ref, kc_ref, vc_ref, kx_ref, vx_ref, o_ref, acc_ref, s_ref = refs
    else:
        q_ref, kc_ref, vc_ref, o_ref, acc_ref = refs
    tq = q_ref.shape[1]
    heads = range(q_ref.shape[0] // HEAD_SLOT)
    slot = lambda h: slice(h * HEAD_SLOT, (h + 1) * HEAD_SLOT)
    vrow = lambda h: slice(h * V_ROWS, (h + 1) * V_ROWS)
    qt = [q_ref[slot(h), :] for h in heads]

    def scores(k2):
        return tuple(_dot(k2[:, slot(h)], qt[h]) for h in heads)

    def absorb(s, vts, m):
        sub = vts[0].shape[1]
        m_new = [jnp.maximum(m[h], jnp.max(s[h], axis=0, keepdims=True)) for h in heads]
        p = [jnp.exp2(s[h] - m_new[h]).astype(BF) for h in heads]
        for h in heads:
            alpha = jnp.exp2(m[h] - m_new[h])
            pv = _dot(vts[0][vrow(h), :], p[h][0:sub, :])
            for t in range(1, len(vts)):
                pv += _dot(vts[t][vrow(h), :], p[h][t * sub:(t + 1) * sub, :])
            acc_ref[h] = alpha * acc_ref[h] + pv
        return tuple(m_new)

    acc_ref[...] = jnp.zeros_like(acc_ref)
    m = tuple(jnp.full((1, tq), -1e30, F32) for _ in heads)
    m = absorb(scores(kc_ref[...]), [vc_ref[t] for t in range(vc_ref.shape[0])], m)
    if has_x:
        n = kx_ref.shape[0] // tk
        sub = vx_ref.shape[2]
        assert n % 2 == 0
        unroll = ATTN_UNROLL if n % ATTN_UNROLL == 0 else 2

        def put_scores(buf, j):
            off = pl.multiple_of(j * tk, tk)
            s = scores(kx_ref[pl.ds(off, tk), :])
            for h in heads:
                s_ref[buf, h] = s[h]

        def take(buf, j, u, m):
            if tk >= sub:
                vts = [vx_ref[j * (tk // sub) + t] for t in range(tk // sub)]
            else:
                r = sub // tk
                assert unroll % r == 0
                vts = [vx_ref[j // r, :, (u % r) * tk:(u % r + 1) * tk]]
            return absorb(tuple(s_ref[buf, h] for h in heads), vts, m)

        def body(i, m):
            for u in range(unroll):
                j = unroll * i + u
                nxt = j + 1 if u + 1 < unroll else jnp.minimum(j + 1, n - 1)
                put_scores((u + 1) % 2, nxt)
                m = take(u % 2, j, u, m)
            return m

        put_scores(0, 0)
        m = lax.fori_loop(0, n // unroll, body, m)
    for h in heads:
        acc = acc_ref[h]
        o_ref[h * MLA_V:(h + 1) * MLA_V, :] = (acc[:MLA_V, :] / acc[MLA_V:MLA_V + 1, :]).astype(BF)


def _attention(qt, kc, vct, kx=None, vxt=None):
    rows = qt.shape[1]
    lc = kc.shape[0]
    tq = min(512, rows)
    has_x = kx is not None
    pair = ATTN_HEADS * HEAD_SLOT
    vpair = ATTN_HEADS * V_ROWS
    in_specs = [pl.BlockSpec((pair, tq), lambda p, i: (p, i)),
                pl.BlockSpec((lc, pair), lambda p, i: (0, p)),
                pl.BlockSpec((vct.shape[0], vpair, vct.shape[2]), lambda p, i: (0, p, 0))]
    args = [qt, kc, vct]
    tk = 256
    scratch = [pltpu.VMEM((ATTN_HEADS, V_ROWS, tq), F32)]
    if has_x:
        lx = kx.shape[0]
        tk = min(tk, lx)
        in_specs += [pl.BlockSpec((lx, pair), lambda p, i: (0, p)),
                     pl.BlockSpec((vxt.shape[0], vpair, vxt.shape[2]), lambda p, i: (0, p, 0))]
        args += [kx, vxt]
        scratch.append(pltpu.VMEM((2, ATTN_HEADS, tk, tq), F32))
    return pl.pallas_call(
        functools.partial(_attn_kernel, tk=tk, has_x=has_x),
        out_shape=jax.ShapeDtypeStruct((MLA_WIDTH, rows), BF),
        grid=(MLA_HEADS // ATTN_HEADS, rows // tq),
        in_specs=in_specs,
        out_specs=pl.BlockSpec((ATTN_HEADS * MLA_V, tq), lambda p, i: (p, i)),
        scratch_shapes=scratch,
        compiler_params=_cparams("arbitrary", "arbitrary"),
        name="attention",
    )(*args)


def _gla_kernel(qf_ref, kf_ref, vf_ref, df_ref, qb_ref, kb_ref, vb_ref, db_ref, s0_ref,
                of_ref, ob_ref, sfin_ref, st_ref):
    i = pl.program_id(0)
    t = qf_ref.shape[0]
    c = GLA_CHUNK
    nc = t // c
    assert c & (c - 1) == 0

    @pl.when(i == 0)
    def _():
        st_ref[...] = s0_ref[...]

    r = lax.broadcasted_iota(jnp.int32, (t, t), 0)
    s = lax.broadcasted_iota(jnp.int32, (t, t), 1)
    same_chunk = jnp.bitwise_xor(r, s) < c
    dirs = ((qf_ref, kf_ref, vf_ref, df_ref, of_ref, same_chunk & (s <= r), range(nc), c - 1),
            (qb_ref, kb_ref, vb_ref, db_ref, ob_ref, same_chunk & (s >= r), range(nc - 1, -1, -1), 0))
    heads = range(GLA_HEADS)
    slot = lambda h: slice(h * HEAD_SLOT, (h + 1) * HEAD_SLOT)
    chunk = lambda ch: slice(ch * c, (ch + 1) * c)
    pair = lambda h: slice((h // 2) * HEAD_SLOT, (h // 2 + 1) * HEAD_SLOT)
    lane = lax.broadcasted_iota(jnp.int32, (1, HEAD_SLOT), 1)
    own = [(lane >= (h % 2) * GLA_QK_SLOT) & (lane < (h % 2 + 1) * GLA_QK_SLOT) for h in heads]

    b = []
    for q_ref, k_ref, v_ref, g_ref, o_ref, mask, order, last in dirs:
        tri = mask.astype(BF)
        g = g_ref[...]
        g_hi = g.astype(BF)
        g_r = g - g_hi.astype(F32)
        g_mid = g_r.astype(BF)
        g_lo = (g_r - g_mid.astype(F32)).astype(BF)
        b.append(_dot(tri, g_hi) + _dot(tri, g_mid) + _dot(tri, g_lo))

    work = []
    for d, (q_ref, k_ref, v_ref, g_ref, o_ref, mask, order, last) in enumerate(dirs):
        tot = [b[d][ch * c + last:ch * c + last + 1, :] for ch in range(nc)]
        b_tot = jnp.concatenate([jnp.broadcast_to(tot[ch], (c, GLA_QK_SLOTS)) for ch in range(nc)], axis=0)
        q_in = (q_ref[...].astype(F32) * jnp.exp(b[d])).astype(BF)
        kf = k_ref[...].astype(F32)
        k_in = (kf * jnp.exp(-b[d])).astype(BF)
        k_out = (kf * jnp.exp(b_tot - b[d])).astype(BF)
        v = v_ref[...]
        q_h = [jnp.where(own[h], q_in[:, pair(h)], jnp.zeros_like(q_in[:, pair(h)])) for h in heads]
        ko_h = [jnp.where(own[h], k_out[:, pair(h)], jnp.zeros_like(k_out[:, pair(h)])) for h in heads]
        a = [_dot_nt(q_h[h], k_in[:, pair(h)]) for h in heads]
        inc = [{ch: _dot_tn(v[chunk(ch), slot(h)], ko_h[h][chunk(ch), :]) for ch in order} for h in heads]
        work.append((tot, q_h, v, a, inc))

    o_intra = []
    for d, (q_ref, k_ref, v_ref, g_ref, o_ref, mask, order, last) in enumerate(dirs):
        tot, q_h, v, a, inc = work[d]
        o_intra.append([_dot(jnp.where(mask, a[h], 0.0).astype(BF), v[:, slot(h)]) for h in heads])

    for d, (q_ref, k_ref, v_ref, g_ref, o_ref, mask, order, last) in enumerate(dirs):
        tot, q_h, v, a, inc = work[d]
        entering = []
        for h in heads:
            st = st_ref[d, h]
            ent = {}
            for ch in order:
                ent[ch] = st.astype(BF)
                st = st * jnp.exp(tot[ch][:, pair(h)]) + inc[h][ch]
            st_ref[d, h] = st
            entering.append(ent)
        for h in heads:
            for ch in order:
                o_ref[chunk(ch), slot(h)] = (o_intra[d][h][chunk(ch), :]
                                             + _dot_nt(q_h[h][chunk(ch), :], entering[h][ch])).astype(BF)

    @pl.when(i == pl.num_programs(0) - 1)
    def _():
        sfin_ref[...] = st_ref[...]


def _gla(gq, gk, gv, df, db, s0):
    rows = gq.shape[0]
    c = min(GLA_STEP, rows)
    n = rows // c
    fwd = pl.BlockSpec((c, GLA_SLOTS), lambda i: (i, 0))
    bwd = pl.BlockSpec((c, GLA_SLOTS), lambda i: (n - 1 - i, 0))
    fwd_qk = pl.BlockSpec((c, GLA_QK_SLOTS), lambda i: (i, 0))
    bwd_qk = pl.BlockSpec((c, GLA_QK_SLOTS), lambda i: (n - 1 - i, 0))
    st_shape = (2, GLA_HEADS, HEAD_SLOT, HEAD_SLOT)
    return pl.pallas_call(
        _gla_kernel,
        out_shape=[jax.ShapeDtypeStruct((rows, GLA_SLOTS), BF), jax.ShapeDtypeStruct((rows, GLA_SLOTS), BF),
                   jax.ShapeDtypeStruct(st_shape, F32)],
        grid=(n,),
        in_specs=[fwd_qk, fwd_qk, fwd, fwd_qk, bwd_qk, bwd_qk, bwd, bwd_qk, _full(st_shape)],
        out_specs=[fwd, bwd, _full(st_shape)],
        scratch_shapes=[pltpu.VMEM(st_shape, F32)],
        compiler_params=_cparams("arbitrary"),
        name="gla_scan",
    )(gq, gk, gv, df, gq, gk, gv, db, s0)


def _out_kernel(x_ref, four_ref, att_ref, of_ref, ob_ref, gg_ref, gain_ref, wout_ref, g1_ref, n2_ref,
                sh_ref, sc_ref, xo_ref, h_ref, *, row):
    o = of_ref[...].astype(F32) + ob_ref[...].astype(F32)
    g = gg_ref[...].astype(F32)
    gate = g * (1.0 / (1.0 + jnp.exp(-g)))
    mixed = [four_ref[...]]
    for h in range(GLA_HEADS):
        sl = slice(h * HEAD_SLOT, (h + 1) * HEAD_SLOT)
        oh = o[:, sl]
        mixed.append((oh * _rms(oh, GLA_DV) * gain_ref[:, sl] * gate[:, sl]).astype(BF))
    tok = F_WIDTH + GLA_SLOTS
    y = _dot(jnp.concatenate(mixed, axis=1), wout_ref[0:tok, :])
    y += _dot_tn(att_ref[...], wout_ref[tok:tok + MLA_WIDTH, :])
    x = x_ref[...] + g1_ref[row:row + 1, :] * y
    xo_ref[...] = x
    hn = x * _rms(x, D_MODEL) * n2_ref[...]
    h_ref[...] = (hn * (1.0 + sc_ref[row:row + 1, :]) + sh_ref[row:row + 1, :]).astype(BF)


def _out_proj(x2, four, att, o_f, o_b, gg, mods, pw, l, row):
    rows = x2.shape[0]
    tm = min(2 * ROW_TILE, rows)
    rt = lambda n: pl.BlockSpec((tm, n), lambda i: (i, 0))
    modspec = lambda k: pl.BlockSpec((None, 8, D_MODEL), lambda i: (l, 0, k))
    return pl.pallas_call(
        functools.partial(_out_kernel, row=row),
        out_shape=[jax.ShapeDtypeStruct((rows, D_MODEL), F32), jax.ShapeDtypeStruct((rows, D_MODEL), BF)],
        grid=(rows // tm,),
        in_specs=[rt(D_MODEL), rt(F_WIDTH), pl.BlockSpec((MLA_WIDTH, tm), lambda i: (0, i)),
                  rt(GLA_SLOTS), rt(GLA_SLOTS), rt(GLA_SLOTS),
                  _layer((1, GLA_SLOTS), l), _layer((MIX_PAD, D_MODEL), l), modspec(2),
                  _layer((1, D_MODEL), l), modspec(3), modspec(4)],
        out_specs=[rt(D_MODEL), rt(D_MODEL)],
        compiler_params=_cparams("arbitrary"),
        name="out_proj",
    )(x2, four, att, o_f, o_b, gg, pw["gla_gain"], pw["w_out"], mods, pw["norm2_w"], mods, mods)


def _ffn_kernel(x_ref, h_ref, hp_ref, hn_ref, w12_ref, conv_ref, wd_ref, g2_ref, o_ref, acc_ref, hx_ref, u_ref,
                *, row):
    i = pl.program_id(0)
    tm = h_ref.shape[0]
    m = tm + 2 * FF_HALO
    keep_prev = jnp.where(i > 0, 1.0, 0.0)
    keep_next = jnp.where(i < pl.num_programs(0) - 1, 1.0, 0.0)
    hx_ref[0:FF_HALO, :] = (hp_ref[...].astype(F32) * keep_prev).astype(BF)
    hx_ref[FF_HALO:FF_HALO + tm, :] = h_ref[...]
    hx_ref[FF_HALO + tm:m, :] = (hn_ref[...].astype(F32) * keep_next).astype(BF)
    acc_ref[...] = jnp.zeros_like(acc_ref)

    starts = [sum(FF_CHUNKS[:c]) for c in range(len(FF_CHUNKS))]

    def cols(c, half):
        lo = half * D_FF + starts[c]
        return slice(lo, lo + FF_CHUNKS[c])

    def up(buf, c):
        for half in range(2):
            u_ref[buf, half, :, 0:FF_CHUNKS[c]] = _dot(hx_ref[...], w12_ref[:, cols(c, half)])

    def conv(buf, c, half):
        u = u_ref[buf, half, :, 0:FF_CHUNKS[c]]
        cw = conv_ref[:, cols(c, half)]
        u_prev = pltpu.roll(u, 1, axis=0)
        u_next = pltpu.roll(u, m - 1, axis=0)
        uc = u_prev * cw[0:1, :] + u * cw[1:2, :] + u_next * cw[2:3, :] + cw[3:4, :]
        return uc[FF_HALO:FF_HALO + tm, :]

    def down(buf, c):
        a = conv(buf, c, 0)
        act = a * (1.0 / (1.0 + jnp.exp(-a))) * conv(buf, c, 1)
        acc_ref[...] += _dot(act.astype(BF), wd_ref[starts[c]:starts[c] + FF_CHUNKS[c], :])

    nbuf = u_ref.shape[0]
    nchunk = len(FF_CHUNKS)
    for c in range(min(FF_AHEAD, nchunk)):
        up(c % nbuf, c)
    for c in range(nchunk):
        if c + FF_AHEAD < nchunk:
            up((c + FF_AHEAD) % nbuf, c + FF_AHEAD)
        down(c % nbuf, c)
    o_ref[...] = x_ref[...] + g2_ref[row:row + 1, :] * acc_ref[...]


def _ffn(x2, h2, mods, pw, l, row):
    rows = x2.shape[0]
    tm = min(512, rows)
    nt = rows // tm
    per = tm // FF_HALO
    last_blk = rows // FF_HALO - 1
    m = tm + 2 * FF_HALO
    rt = lambda n: pl.BlockSpec((tm, n), lambda i: (i, 0))
    once = lambda shape: pl.BlockSpec((None,) + shape, lambda i: (l,) + (0,) * len(shape),
                                      pipeline_mode=pl.Buffered(1))
    return pl.pallas_call(
        functools.partial(_ffn_kernel, row=row),
        out_shape=jax.ShapeDtypeStruct((rows, D_MODEL), F32),
        grid=(nt,),
        in_specs=[rt(D_MODEL), rt(D_MODEL),
                  pl.BlockSpec((FF_HALO, D_MODEL), lambda i: (jnp.maximum(i * per - 1, 0), 0)),
                  pl.BlockSpec((FF_HALO, D_MODEL), lambda i: (jnp.minimum((i + 1) * per, last_blk), 0)),
                  once((D_MODEL, 2 * D_FF)), once((4, 2 * D_FF)), once((D_FF, D_MODEL)),
                  pl.BlockSpec((None, 8, D_MODEL), lambda i: (l, 0, 5))],
        out_specs=rt(D_MODEL),
        scratch_shapes=[pltpu.VMEM((tm, D_MODEL), F32), pltpu.VMEM((m, D_MODEL), BF),
                        pltpu.VMEM((FF_AHEAD + 1, 2, m, max(FF_CHUNKS)), F32)],
        compiler_params=_cparams("arbitrary"),
        name="conv_ffn",
    )(x2, h2, h2, h2, pw["w12"], pw["conv"], pw["w_down"], mods)


def kernel(x, c, ctx, c_ctx, mod_w, mod_b, norm1_w, norm2_w, w_in, mla_q_lora_norm_w, mla_w_uq, mla_kv_lora_norm_w, mla_w_ukv, mla_q_norm_w, mla_k_norm_w, gla_w_gk_fwd, gla_b_gk_fwd, gla_w_gk_bwd, gla_b_gk_bwd, gla_norm_w, w_out, ffn_w_12, ffn_conv_w, ffn_conv_b, ffn_w_down):
    batch, seq, d = x.shape
    assert batch == 1 and d == D_MODEL
    lc = ctx.shape[1]
    depth = mod_w.shape[0]
    n1 = int(round(math.sqrt(seq)))
    assert n1 * n1 == seq and seq % ROW_TILE == 0 and lc % GLA_CHUNK == 0

    pw = _prep_weights(dict(
        w_in=w_in, mla_w_uq=mla_w_uq, mla_q_lora_norm_w=mla_q_lora_norm_w, mla_kv_lora_norm_w=mla_kv_lora_norm_w,
        mla_w_ukv=mla_w_ukv, mla_q_norm_w=mla_q_norm_w, mla_k_norm_w=mla_k_norm_w, gla_w_gk_fwd=gla_w_gk_fwd,
        gla_b_gk_fwd=gla_b_gk_fwd, gla_w_gk_bwd=gla_w_gk_bwd, gla_b_gk_bwd=gla_b_gk_bwd, gla_norm_w=gla_norm_w,
        w_out=w_out, ffn_w_12=ffn_w_12, ffn_conv_w=ffn_conv_w, ffn_conv_b=ffn_conv_b, ffn_w_down=ffn_w_down,
        norm1_w=norm1_w, norm2_w=norm2_w))
    cos_t, sin_t = _rope_tables(seq)
    ones_c = jnp.ones((lc, HEAD_SLOT), F32)
    w1, tab = _fft_tables(n1, n1)
    tab_c = _ctx_dft_table(lc)

    cond8 = jnp.zeros((8, d), F32).at[0].set(c[0].astype(F32)).at[1].set(c_ctx.astype(F32))
    mods = _adaln(cond8, mod_w, mod_b)

    xs = x[0].astype(F32)
    xc = ctx[0].astype(F32)
    s_zero = jnp.zeros((2, GLA_HEADS, HEAD_SLOT, HEAD_SLOT), F32)
    for l in range(depth):
        last = l == depth - 1
        fa_c, fb_c, q_c, k_c, v_c, gq_c, gk_c, gv_c, gg_c, df_c, db_c = _in_proj(xc, mods, pw, ones_c, ones_c, l, 1, False)
        fa_x, fb_x, q_x, k_x, v_x, gq_x, gk_x, gv_x, gg_x, df_x, db_x = _in_proj(xs, mods, pw, cos_t, sin_t, l, 0, True)

        of_c, ob_c, s_c = _gla(gq_c, gk_c, gv_c, df_c, db_c, s_zero)
        of_x, ob_x, _ = _gla(gq_x, gk_x, gv_x, df_x, db_x, s_c)
        four_x = _fourier_latent(fa_x, fb_x, w1, tab)
        att_x = _attention(q_x, k_c, v_c, k_x, v_x)
        x_mid, h2 = _out_proj(xs, four_x, att_x, of_x, ob_x, gg_x, mods, pw, l, 0)
        xs = _ffn(x_mid, h2, mods, pw, l, 0)
        if not last:
            four_c = _fourier_ctx(fa_c, fb_c, tab_c)
            att_c = _attention(q_c, k_c, v_c)
            c_mid, hc2 = _out_proj(xc, four_c, att_c, of_c, ob_c, gg_c, mods, pw, l, 1)
            xc = _ffn(c_mid, hc2, mods, pw, l, 1)
    return xs[None].astype(x.dtype)
```

```python
import functools
import math

import numpy as np
import jax
import jax.numpy as jnp
from jax import lax
from jax.experimental import pallas as pl
from jax.experimental.pallas import tpu as pltpu

D_MODEL = 1024
DEPTH = 2
GRID_W = 64
EPS = 1e-6

F_GROUPS = 4
F_DIM = 64
F_WIDTH = F_GROUPS * F_DIM

MLA_HEADS = 6
MLA_Q_LORA = 192
MLA_KV_LORA = 128
MLA_NOPE = 64
MLA_ROPE = 32
MLA_V = 64
MLA_QK = MLA_NOPE + MLA_ROPE
MLA_WIDTH = MLA_HEADS * MLA_V
MLA_IN = MLA_Q_LORA + MLA_KV_LORA + MLA_ROPE
ROPE_BASE = 10000.0

GLA_HEADS = 4
GLA_DK = 48
GLA_DV = 96
GLA_GATE_RANK = 16
GLA_GATE_NORM = 16.0
GLA_CHUNK = 64
GLA_STEP = 256
GLA_WIDTH = GLA_HEADS * GLA_DV
GLA_QK_W = GLA_HEADS * GLA_DK

MIX_WIDTH = F_WIDTH + MLA_WIDTH + GLA_WIDTH
D_FF = 2816
N_MOD = 6

LANE = 128
SUBLANE_BF16 = 16
VMEM_LIMIT = 48 * 1024 * 1024

HEAD_SLOT = LANE
V_ROWS = 80
MLA_SLOTS = MLA_HEADS * HEAD_SLOT
GLA_SLOTS = GLA_HEADS * HEAD_SLOT
GLA_QK_SLOT = 64
GLA_QK_SLOTS = GLA_HEADS * GLA_QK_SLOT

COL_F = 0
COL_CQ = COL_F + F_WIDTH
COL_CKV = COL_CQ + 256
COL_KPE = COL_CKV + MLA_KV_LORA
COL_GQ = COL_KPE + LANE
COL_GK = COL_GQ + GLA_QK_SLOTS
COL_GV = COL_GK + GLA_QK_SLOTS
COL_GG = COL_GV + GLA_SLOTS
COL_LR = COL_GG + GLA_SLOTS
IN_PAD = COL_LR + LANE

MIX_PAD = F_WIDTH + MLA_WIDTH + GLA_SLOTS

FF_CHUNKS = (768, 768, 768, 512)
assert sum(FF_CHUNKS) == D_FF
FF_HALO = SUBLANE_BF16
FF_AHEAD = 1

LOG2E = 1.4426950408889634

ROW_TILE = 256
IN_SUB = 256
ATTN_UNROLL = 32
ATTN_HEADS = 2
BF = jnp.bfloat16
F32 = jnp.float32


def _cparams(*sem):
    return pltpu.CompilerParams(dimension_semantics=sem, vmem_limit_bytes=VMEM_LIMIT)


def _dot(a, b):
    return jnp.dot(a, b, preferred_element_type=F32)


def _dot_nt(a, b):
    return lax.dot_general(a, b, (((1,), (1,)), ((), ())), preferred_element_type=F32)


def _dot_tn(a, b):
    return lax.dot_general(a, b, (((0,), (0,)), ((), ())), preferred_element_type=F32)


def _full(shape):
    n = len(shape)
    return pl.BlockSpec(shape, lambda *_: (0,) * n)


def _layer(shape, l):
    n = len(shape)
    return pl.BlockSpec((None,) + tuple(shape), lambda *_: (l,) + (0,) * n)


def _rot_partner(n_rope):
    q = n_rope // 4
    src = np.zeros(n_rope, np.int64)
    sgn = np.zeros(n_rope, np.float32)
    for base in (0, 2 * q):
        for j in range(q):
            src[base + j] = base + q + j
            sgn[base + j] = -1.0
            src[base + q + j] = base + j
            sgn[base + q + j] = 1.0
    return src, sgn


def _take_cols(w, src, sgn=None, axis=-1):
    src = np.asarray(src)
    sgn = np.ones(len(src), np.float32) if sgn is None else np.asarray(sgn, np.float32)
    axis = axis % w.ndim
    pieces = []
    lo = 0
    while lo < len(src):
        hi = lo + 1
        if src[lo] < 0:
            while hi < len(src) and src[hi] < 0:
                hi += 1
            shape = list(w.shape)
            shape[axis] = hi - lo
            pieces.append(jnp.zeros(shape, w.dtype))
        else:
            while hi < len(src) and src[hi] == src[hi - 1] + 1 and sgn[hi] == sgn[lo]:
                hi += 1
            piece = lax.slice_in_dim(w, int(src[lo]), int(src[lo]) + hi - lo, axis=axis)
            pieces.append(-piece if sgn[lo] < 0 else piece)
        lo = hi
    return jnp.concatenate(pieces, axis=axis)


def _pad_rows(w, n):
    pad = [(0, 0)] * w.ndim
    pad[-2] = (0, n - w.shape[-2])
    return jnp.pad(w, pad)


def _in_proj_layout():
    src = -np.ones(IN_PAD, np.int64)
    sgn = np.ones(IN_PAD, np.float32)
    src[COL_F:COL_F + F_WIDTH] = np.arange(F_WIDTH)
    o = F_WIDTH
    src[COL_CQ:COL_CQ + MLA_Q_LORA] = o + np.arange(MLA_Q_LORA)
    o += MLA_Q_LORA
    src[COL_CKV:COL_CKV + MLA_KV_LORA] = o + np.arange(MLA_KV_LORA)
    o += MLA_KV_LORA
    src[COL_KPE:COL_KPE + MLA_ROPE] = o + np.arange(MLA_ROPE)
    psrc, psgn = _rot_partner(MLA_ROPE)
    src[COL_KPE + MLA_ROPE:COL_KPE + 2 * MLA_ROPE] = o + psrc
    sgn[COL_KPE + MLA_ROPE:COL_KPE + 2 * MLA_ROPE] = psgn
    o += MLA_ROPE
    for h in range(GLA_HEADS):
        src[COL_GQ + h * GLA_QK_SLOT:COL_GQ + h * GLA_QK_SLOT + GLA_DK] = o + h * GLA_DK + np.arange(GLA_DK)
    o += GLA_QK_W
    for h in range(GLA_HEADS):
        src[COL_GK + h * GLA_QK_SLOT:COL_GK + h * GLA_QK_SLOT + GLA_DK] = o + h * GLA_DK + np.arange(GLA_DK)
    o += GLA_QK_W
    for h in range(GLA_HEADS):
        src[COL_GV + h * HEAD_SLOT:COL_GV + h * HEAD_SLOT + GLA_DV] = o + h * GLA_DV + np.arange(GLA_DV)
    o += GLA_WIDTH
    src[COL_LR:COL_LR + GLA_GATE_RANK] = o + np.arange(GLA_GATE_RANK)
    o += GLA_GATE_RANK
    for h in range(GLA_HEADS):
        src[COL_GG + h * HEAD_SLOT:COL_GG + h * HEAD_SLOT + GLA_DV] = o + h * GLA_DV + np.arange(GLA_DV)
    return src, sgn


def _head_slots(n_heads, d_src, d_take, src_off=0):
    src = -np.ones(n_heads * HEAD_SLOT, np.int64)
    for h in range(n_heads):
        src[h * HEAD_SLOT:h * HEAD_SLOT + d_take] = h * d_src + src_off + np.arange(d_take)
    return src


def _prep_weights(w):
    f32 = lambda a: a.astype(F32)
    out = {}
    src, sgn = _in_proj_layout()
    out["w_in"] = _take_cols(w["w_in"], src, sgn).astype(BF)

    m = np.arange(F_DIM)
    ang = 2.0 * np.pi * np.outer(m, m) / F_DIM
    c64, s64 = np.cos(ang), np.sin(ang)
    dft = np.zeros((F_WIDTH, 2 * F_WIDTH), np.float32)
    for g in range(F_GROUPS):
        dft[g * F_DIM:(g + 1) * F_DIM, g * F_DIM:(g + 1) * F_DIM] = c64
        dft[g * F_DIM:(g + 1) * F_DIM, F_WIDTH + g * F_DIM:F_WIDTH + (g + 1) * F_DIM] = s64
    out["dft64"] = jnp.asarray(dft, F32).astype(BF)

    psrc, psgn = _rot_partner(MLA_ROPE)
    q_src = _head_slots(MLA_HEADS, MLA_QK, MLA_QK)
    q_part = -np.ones(MLA_SLOTS, np.int64)
    q_psg = np.ones(MLA_SLOTS, np.float32)
    g_part = -np.ones(MLA_SLOTS, np.int64)
    for h in range(MLA_HEADS):
        lo = h * HEAD_SLOT + MLA_NOPE
        q_part[lo:lo + MLA_ROPE] = h * MLA_QK + MLA_NOPE + psrc
        q_psg[lo:lo + MLA_ROPE] = psgn
        g_part[lo:lo + MLA_ROPE] = MLA_NOPE + psrc
    g_src = np.where(q_src >= 0, q_src % MLA_QK, -1)
    wq = jnp.concatenate([_take_cols(w["mla_w_uq"], q_src), _take_cols(w["mla_w_uq"], q_part, q_psg)], axis=-1)
    out["w_uq"] = _pad_rows(wq, 256).astype(BF)
    out["q_lora_w"] = jnp.pad(f32(w["mla_q_lora_norm_w"]), ((0, 0), (0, 256 - MLA_Q_LORA)))[:, None, :]
    out["kv_lora_w"] = f32(w["mla_kv_lora_norm_w"])[:, None, :]
    out["q_gain"] = _take_cols(w["mla_q_norm_w"], g_src)[:, None, :]
    out["q_gain_p"] = _take_cols(w["mla_q_norm_w"], g_part)[:, None, :]
    out["k_gain"] = _take_cols(w["mla_k_norm_w"], g_src)[:, None, :]
    out["k_gain_p"] = _take_cols(w["mla_k_norm_w"], g_part)[:, None, :]

    kn_src = _head_slots(MLA_HEADS, MLA_NOPE + MLA_V, MLA_NOPE)
    v_src = _head_slots(MLA_HEADS, MLA_NOPE + MLA_V, MLA_V, MLA_NOPE)
    out["w_ukv"] = jnp.concatenate([_take_cols(w["mla_w_ukv"], kn_src), _take_cols(w["mla_w_ukv"], v_src)],
                                   axis=-1).astype(BF)

    e2 = np.zeros((LANE, 2 * MLA_SLOTS), np.float32)
    for h in range(MLA_HEADS):
        for j in range(MLA_ROPE):
            e2[j, h * HEAD_SLOT + MLA_NOPE + j] = 1.0
            e2[MLA_ROPE + j, MLA_SLOTS + h * HEAD_SLOT + MLA_NOPE + j] = 1.0
    out["kpe_place"] = jnp.asarray(e2, BF)

    gk_src = -np.ones(GLA_QK_SLOTS, np.int64)
    for h in range(GLA_HEADS):
        gk_src[h * GLA_QK_SLOT:h * GLA_QK_SLOT + GLA_DK] = h * GLA_DK + np.arange(GLA_DK)
    wgk = jnp.concatenate([_take_cols(w["gla_w_gk_fwd"], gk_src), _take_cols(w["gla_w_gk_bwd"], gk_src)], axis=-1)
    out["w_gk"] = _pad_rows(wgk, LANE).astype(BF)
    out["b_gk"] = jnp.concatenate([_take_cols(w["gla_b_gk_fwd"], gk_src), _take_cols(w["gla_b_gk_bwd"], gk_src)],
                                  axis=-1)[:, None, :]
    gv_src = _head_slots(GLA_HEADS, GLA_DV, GLA_DV)
    out["gla_gain"] = _take_cols(w["gla_norm_w"], np.where(gv_src >= 0, gv_src % GLA_DV, -1))[:, None, :]

    row_src = np.concatenate([np.arange(F_WIDTH), np.where(gv_src >= 0, F_WIDTH + MLA_WIDTH + gv_src, -1),
                              F_WIDTH + np.arange(MLA_WIDTH)])
    out["w_out"] = _take_cols(w["w_out"], row_src, axis=-2).astype(BF)

    nd = w["ffn_w_12"].shape[0]
    out["w12"] = w["ffn_w_12"].astype(BF)
    out["conv"] = jnp.concatenate([w["ffn_conv_w"], w["ffn_conv_b"][:, None, :]], axis=1).astype(F32)
    out["w_down"] = w["ffn_w_down"].astype(BF)
    out["norm1_w"] = f32(w["norm1_w"])[:, None, :]
    out["norm2_w"] = f32(w["norm2_w"])[:, None, :]
    return out


def _rope_tables(n_tokens):
    q = MLA_ROPE // 4
    t = np.arange(n_tokens)
    row = (t // GRID_W).astype(np.float32)
    col = (t % GRID_W).astype(np.float32)
    axis_dims = MLA_ROPE // 2
    inv_freq = np.power(np.float32(ROPE_BASE), -np.arange(0, axis_dims, 2, dtype=np.float32) / axis_dims)
    ang_r = row[:, None] * inv_freq
    ang_c = col[:, None] * inv_freq
    cos = np.ones((n_tokens, HEAD_SLOT), np.float32)
    sin = np.zeros((n_tokens, HEAD_SLOT), np.float32)
    for k, ang in enumerate((ang_r, ang_r, ang_c, ang_c)):
        lo = MLA_NOPE + k * q
        cos[:, lo:lo + q] = np.cos(ang)
        sin[:, lo:lo + q] = np.sin(ang)
    return jnp.asarray(cos), jnp.asarray(sin)


def _fft_tables(n1, n2):
    k = np.arange(n1)
    ang1 = 2.0 * np.pi * np.outer(k, k) / n1
    fr, fi = np.cos(ang1), -np.sin(ang1)
    w1 = np.block([[fr, fi], [fi, -fr]])
    length = n1 * n2
    kk = (np.arange(n1)[:, None] + n1 * np.arange(n2)[None, :]).astype(np.float64)
    t2 = np.arange(n2, dtype=np.float64)
    ang2 = 2.0 * np.pi * kk[:, :, None] * t2[None, None, :] / length
    norm = 1.0 / math.sqrt(length * F_DIM)
    tab = np.concatenate([np.cos(ang2), np.sin(ang2)], axis=-1) * norm
    return jnp.asarray(w1, F32).astype(BF), jnp.asarray(tab, F32).astype(BF)


def _ctx_dft_table(n):
    k = np.arange(n)
    ang = 2.0 * np.pi * np.outer(k, k) / n
    norm = 1.0 / math.sqrt(n * F_DIM)
    return jnp.asarray(np.concatenate([np.cos(ang), -np.sin(ang)], axis=1) * norm, F32).astype(BF)


def _mod_kernel(c_ref, w_ref, b_ref, o_ref):
    c = c_ref[...]
    s = c * (1.0 / (1.0 + jnp.exp(-c)))
    o_ref[...] = _dot(s.astype(BF), w_ref[...].astype(BF)) + b_ref[...]


def _adaln(cond8, mod_w, mod_b):
    nd, d, n = mod_w.shape
    tn = 1536
    return pl.pallas_call(
        _mod_kernel,
        out_shape=jax.ShapeDtypeStruct((nd, 8, n), F32),
        grid=(nd, n // tn),
        in_specs=[pl.BlockSpec((8, d), lambda l, j: (0, 0)),
                  pl.BlockSpec((None, d, tn), lambda l, j: (l, 0, j)),
                  pl.BlockSpec((None, 1, tn), lambda l, j: (l, 0, j))],
        out_specs=pl.BlockSpec((None, 8, tn), lambda l, j: (l, 0, j)),
        compiler_params=_cparams("arbitrary", "arbitrary"),
        name="adaln",
    )(cond8, mod_w, mod_b.reshape(nd, 1, n))


def _rms(x, n):
    return lax.rsqrt(jnp.sum(x * x, axis=-1, keepdims=True) * (1.0 / n) + EPS)


def _in_kernel(x_ref, sh_ref, sc_ref, n1_ref, win_ref, dft_ref, qlw_ref, wq_ref, kvlw_ref, wkv_ref, place_ref,
               qg_ref, qgp_ref, kg_ref, kgp_ref, cos_ref, sin_ref, wgk_ref, bgk_ref,
               fa_ref, fb_ref, q_ref, k_ref, v_ref, gq_ref, gk_ref, gv_ref, gg_ref, df_ref, db_ref,
               *, row, rope):
    for r0 in range(0, x_ref.shape[0], IN_SUB):
        rs = slice(r0, min(r0 + IN_SUB, x_ref.shape[0]))
        x = x_ref[rs, :]
        xn = x * _rms(x, D_MODEL) * n1_ref[...]
        h = xn * (1.0 + sc_ref[row:row + 1, :]) + sh_ref[row:row + 1, :]
        hb = h.astype(BF)
        p = _dot(hb, win_ref[:, :COL_GQ])

        ab = _dot(p[:, COL_F:COL_F + F_WIDTH].astype(BF), dft_ref[...])
        fa_ref[rs, :] = ab[:, :F_WIDTH].astype(BF)
        fb_ref[rs, :] = ab[:, F_WIDTH:].astype(BF)

        cq = p[:, COL_CQ:COL_CQ + 256]
        cqn = cq * _rms(cq, MLA_Q_LORA) * qlw_ref[...]
        qq = _dot(cqn.astype(BF), wq_ref[...])
        ckv = p[:, COL_CKV:COL_CKV + MLA_KV_LORA]
        ckvn = ckv * _rms(ckv, MLA_KV_LORA) * kvlw_ref[...]
        kk = _dot(ckvn.astype(BF), wkv_ref[...])
        kp = _dot(p[:, COL_KPE:COL_KPE + LANE].astype(BF), place_ref[...])
        pg = _dot(hb, win_ref[:, COL_GQ:])
        v_lane = lax.broadcasted_iota(jnp.int32, (x.shape[0], HEAD_SLOT), 1)

        q_scale = (MLA_QK ** -0.5) * LOG2E
        for hd in range(MLA_HEADS):
            sl = slice(hd * HEAD_SLOT, (hd + 1) * HEAD_SLOT)
            sp = slice(MLA_SLOTS + hd * HEAD_SLOT, MLA_SLOTS + (hd + 1) * HEAD_SLOT)
            qh = qq[:, sl]
            kh = kk[:, sl] + kp[:, sl]
            rq = _rms(qh, MLA_QK) * q_scale
            rk = _rms(kh, MLA_QK)
            if rope:
                cos = cos_ref[rs, :]
                sin = sin_ref[rs, :]
                qo = rq * (qh * (qg_ref[:, sl] * cos) + qq[:, sp] * (qgp_ref[:, sl] * sin))
                ko = rk * (kh * (kg_ref[:, sl] * cos) + kp[:, sp] * (kgp_ref[:, sl] * sin))
            else:
                qo = rq * (qh * qg_ref[:, sl])
                ko = rk * (kh * kg_ref[:, sl])
            q_ref[sl, rs] = qo.T.astype(BF)
            k_ref[rs, sl] = ko.astype(BF)
            vh = jnp.where(v_lane == MLA_V, 1.0, kk[:, MLA_SLOTS + hd * HEAD_SLOT:MLA_SLOTS + (hd + 1) * HEAD_SLOT])
            v_ref[hd * V_ROWS:(hd + 1) * V_ROWS, rs] = vh.T[:V_ROWS, :].astype(BF)

        gq_ref[rs, :] = (pg[:, 0:GLA_QK_SLOTS] * (GLA_DK ** -0.5)).astype(BF)
        gk_ref[rs, :] = pg[:, COL_GK - COL_GQ:COL_GK - COL_GQ + GLA_QK_SLOTS].astype(BF)
        gv_ref[rs, :] = pg[:, COL_GV - COL_GQ:COL_GV - COL_GQ + GLA_SLOTS].astype(BF)
        gg_ref[rs, :] = pg[:, COL_GG - COL_GQ:COL_GG - COL_GQ + GLA_SLOTS].astype(BF)
        z = _dot(pg[:, COL_LR - COL_GQ:COL_LR - COL_GQ + LANE].astype(BF), wgk_ref[...]) + bgk_ref[...]
        ls = (jnp.minimum(z, 0.0) - jnp.log(1.0 + jnp.exp(-jnp.abs(z)))) * (1.0 / GLA_GATE_NORM)
        df_ref[rs, :] = ls[:, :GLA_QK_SLOTS]
        db_ref[rs, :] = ls[:, GLA_QK_SLOTS:]


def _in_proj(x2, mods, pw, cos_t, sin_t, l, row, rope):
    rows = x2.shape[0]
    tm = min(2 * IN_SUB, rows)
    rt = lambda n: pl.BlockSpec((tm, n), lambda i: (i, 0))
    modspec = lambda k: pl.BlockSpec((None, 8, D_MODEL), lambda i: (l, 0, k))
    outs = [(F_WIDTH, BF), (F_WIDTH, BF), None, (MLA_SLOTS, BF), None,
            (GLA_QK_SLOTS, BF), (GLA_QK_SLOTS, BF), (GLA_SLOTS, BF), (GLA_SLOTS, BF), (GLA_QK_SLOTS, F32),
            (GLA_QK_SLOTS, F32)]
    shapes = [jax.ShapeDtypeStruct((rows, o[0]), o[1]) if o else None for o in outs]
    specs = [rt(o[0]) if o else None for o in outs]
    shapes[2] = jax.ShapeDtypeStruct((MLA_SLOTS, rows), BF)
    specs[2] = pl.BlockSpec((MLA_SLOTS, tm), lambda i: (0, i))
    shapes[4] = jax.ShapeDtypeStruct((rows // tm, MLA_HEADS * V_ROWS, tm), BF)
    specs[4] = pl.BlockSpec((None, MLA_HEADS * V_ROWS, tm), lambda i: (i, 0, 0))
    return pl.pallas_call(
        functools.partial(_in_kernel, row=row, rope=rope),
        out_shape=shapes,
        grid=(rows // tm,),
        in_specs=[rt(D_MODEL), modspec(0), modspec(1), _layer((1, D_MODEL), l),
                  _layer((D_MODEL, IN_PAD), l), _full((F_WIDTH, 2 * F_WIDTH)),
                  _layer((1, 256), l), _layer((256, 2 * MLA_SLOTS), l),
                  _layer((1, MLA_KV_LORA), l), _layer((MLA_KV_LORA, 2 * MLA_SLOTS), l),
                  _full((LANE, 2 * MLA_SLOTS)),
                  _layer((1, MLA_SLOTS), l), _layer((1, MLA_SLOTS), l), _layer((1, MLA_SLOTS), l),
                  _layer((1, MLA_SLOTS), l),
                  rt(HEAD_SLOT), rt(HEAD_SLOT),
                  _layer((LANE, 2 * GLA_QK_SLOTS), l), _layer((1, 2 * GLA_QK_SLOTS), l)],
        out_specs=specs,
        compiler_params=_cparams("arbitrary"),
        name="in_proj",
    )(x2, mods, mods, pw["norm1_w"], pw["w_in"], pw["dft64"], pw["q_lora_w"], pw["w_uq"], pw["kv_lora_w"],
      pw["w_ukv"], pw["kpe_place"], pw["q_gain"], pw["q_gain_p"], pw["k_gain"], pw["k_gain_p"],
      cos_t, sin_t, pw["w_gk"], pw["b_gk"])


def _fft1_kernel(a_ref, b_ref, w_ref, g_ref):
    n1 = a_ref.shape[0]
    ab = jnp.concatenate([a_ref[...], b_ref[...]], axis=0)
    g = _dot(w_ref[...], ab)
    g_ref[0] = g[:n1].astype(BF)
    g_ref[1] = g[n1:].astype(BF)


def _fft2_kernel(g_ref, t_ref, o_ref, *, batch):
    for j in range(batch):
        g = jnp.concatenate([g_ref[0, j], g_ref[1, j]], axis=0)
        o_ref[:, j * F_WIDTH:(j + 1) * F_WIDTH] = _dot(t_ref[j], g).astype(BF)


def _fourier_latent(fa, fb, w1, tab):
    length = fa.shape[0]
    n1, n2 = tab.shape[0], tab.shape[1]
    cols = n2 * F_WIDTH
    tn = min(4096, cols)
    g = pl.pallas_call(
        _fft1_kernel,
        out_shape=jax.ShapeDtypeStruct((2, n1, cols), BF),
        grid=(cols // tn,),
        in_specs=[pl.BlockSpec((n1, tn), lambda j: (0, j)), pl.BlockSpec((n1, tn), lambda j: (0, j)),
                  _full((2 * n1, 2 * n1))],
        out_specs=pl.BlockSpec((2, n1, tn), lambda j: (0, 0, j)),
        compiler_params=_cparams("arbitrary"),
        name="fft_stage1",
    )(fa.reshape(n1, cols), fb.reshape(n1, cols), w1)
    batch = 8
    y = pl.pallas_call(
        functools.partial(_fft2_kernel, batch=batch),
        out_shape=jax.ShapeDtypeStruct((n2, n1 * F_WIDTH), BF),
        grid=(n1 // batch,),
        in_specs=[pl.BlockSpec((2, batch, n2, F_WIDTH), lambda i: (0, i, 0, 0)),
                  pl.BlockSpec((batch, n2, 2 * n2), lambda i: (i, 0, 0))],
        out_specs=pl.BlockSpec((n2, batch * F_WIDTH), lambda i: (0, i)),
        compiler_params=_cparams("arbitrary"),
        name="fft_stage2",
    )(g.reshape(2, n1, n2, F_WIDTH), tab)
    return y.reshape(length, F_WIDTH)


def _fctx_kernel(a_ref, b_ref, t_ref, o_ref):
    ab = jnp.concatenate([a_ref[...], b_ref[...]], axis=0)
    o_ref[...] = _dot(t_ref[...], ab).astype(BF)


def _fourier_ctx(fa, fb, tab):
    n = fa.shape[0]
    return pl.pallas_call(
        _fctx_kernel,
        out_shape=jax.ShapeDtypeStruct((n, F_WIDTH), BF),
        grid=(1,),
        in_specs=[_full((n, F_WIDTH)), _full((n, F_WIDTH)), _full((n, 2 * n))],
        out_specs=_full((n, F_WIDTH)),
        compiler_params=_cparams("arbitrary"),
        name="fft_ctx",
    )(fa, fb, tab)


def _attn_kernel(*refs, tk, has_x):
    if has_x:
        q_ref, kc_ref, vc_ref, kx_ref, vx_ref, o_ref, acc_ref, s_ref = refs
    else:
        q_ref, kc_ref, vc_ref, o_ref, acc_ref = refs
    tq = q_ref.shape[1]
    heads = range(q_ref.shape[0] // HEAD_SLOT)
    slot = lambda h: slice(h * HEAD_SLOT, (h + 1) * HEAD_SLOT)
    vrow = lambda h: slice(h * V_ROWS, (h + 1) * V_ROWS)
    qt = [q_ref[slot(h), :] for h in heads]

    def scores(k2):
        return tuple(_dot(k2[:, slot(h)], qt[h]) for h in heads)

    def absorb(s, vts, m):
        sub = vts[0].shape[1]
        m_new = [jnp.maximum(m[h], jnp.max(s[h], axis=0, keepdims=True)) for h in heads]
        p = [jnp.exp2(s[h] - m_new[h]).astype(BF) for h in heads]
        for h in heads:
            alpha = jnp.exp2(m[h] - m_new[h])
            pv = _dot(vts[0][vrow(h), :], p[h][0:sub, :])
            for t in range(1, len(vts)):
                pv += _dot(vts[t][vrow(h), :], p[h][t * sub:(t + 1) * sub, :])
            acc_ref[h] = alpha * acc_ref[h] + pv
        return tuple(m_new)

    acc_ref[...] = jnp.zeros_like(acc_ref)
    m = tuple(jnp.full((1, tq), -1e30, F32) for _ in heads)
    m = absorb(scores(kc_ref[...]), [vc_ref[t] for t in range(vc_ref.shape[0])], m)
    if has_x:
        n = kx_ref.shape[0] // tk
        sub = vx_ref.shape[2]
        assert n % 2 == 0
        unroll = ATTN_UNROLL if n % ATTN_UNROLL == 0 else 2

        def put_scores(buf, j):
            off = pl.multiple_of(j * tk, tk)
            s = scores(kx_ref[pl.ds(off, tk), :])
            for h in heads:
                s_ref[buf, h] = s[h]

        def take(buf, j, u, m):
            if tk >= sub:
                vts = [vx_ref[j * (tk // sub) + t] for t in range(tk // sub)]
            else:
                r = sub // tk
                assert unroll % r == 0
                vts = [vx_ref[j // r, :, (u % r) * tk:(u % r + 1) * tk]]
            return absorb(tuple(s_ref[buf, h] for h in heads), vts, m)

        def body(i, m):
            for u in range(unroll):
                j = unroll * i + u
                nxt = j + 1 if u + 1 < unroll else jnp.minimum(j + 1, n - 1)
                put_scores((u + 1) % 2, nxt)
                m = take(u % 2, j, u, m)
            return m

        put_scores(0, 0)
        m = lax.fori_loop(0, n // unroll, body, m)
    for h in heads:
        acc = acc_ref[h]
        o_ref[h * MLA_V:(h + 1) * MLA_V, :] = (acc[:MLA_V, :] / acc[MLA_V:MLA_V + 1, :]).astype(BF)


def _attention(qt, kc, vct, kx=None, vxt=None):
    rows = qt.shape[1]
    lc = kc.shape[0]
    tq = min(512, rows)
    has_x = kx is not None
    pair = ATTN_HEADS * HEAD_SLOT
    vpair = ATTN_HEADS * V_ROWS
    in_specs = [pl.BlockSpec((pair, tq), lambda p, i: (p, i)),
                pl.BlockSpec((lc, pair), lambda p, i: (0, p)),
                pl.BlockSpec((vct.shape[0], vpair, vct.shape[2]), lambda p, i: (0, p, 0))]
    args = [qt, kc, vct]
    tk = 256
    scratch = [pltpu.VMEM((ATTN_HEADS, V_ROWS, tq), F32)]
    if has_x:
        lx = kx.shape[0]
        tk = min(tk, lx)
        in_specs += [pl.BlockSpec((lx, pair), lambda p, i: (0, p)),
                     pl.BlockSpec((vxt.shape[0], vpair, vxt.shape[2]), lambda p, i: (0, p, 0))]
        args += [kx, vxt]
        scratch.append(pltpu.VMEM((2, ATTN_HEADS, tk, tq), F32))
    return pl.pallas_call(
        functools.partial(_attn_kernel, tk=tk, has_x=has_x),
        out_shape=jax.ShapeDtypeStruct((MLA_WIDTH, rows), BF),
        grid=(MLA_HEADS // ATTN_HEADS, rows // tq),
        in_specs=in_specs,
        out_specs=pl.BlockSpec((ATTN_HEADS * MLA_V, tq), lambda p, i: (p, i)),
        scratch_shapes=scratch,
        compiler_params=_cparams("arbitrary", "arbitrary"),
        name="attention",
    )(*args)


def _gla_kernel(qf_ref, kf_ref, vf_ref, df_ref, qb_ref, kb_ref, vb_ref, db_ref, s0_ref,
                of_ref, ob_ref, sfin_ref, st_ref):
    i = pl.program_id(0)
    t = qf_ref.shape[0]
    c = GLA_CHUNK
    nc = t // c
    assert c & (c - 1) == 0

    @pl.when(i == 0)
    def _():
        st_ref[...] = s0_ref[...]

    r = lax.broadcasted_iota(jnp.int32, (t, t), 0)
    s = lax.broadcasted_iota(jnp.int32, (t, t), 1)
    same_chunk = jnp.bitwise_xor(r, s) < c
    dirs = ((qf_ref, kf_ref, vf_ref, df_ref, of_ref, same_chunk & (s <= r), range(nc), c - 1),
            (qb_ref, kb_ref, vb_ref, db_ref, ob_ref, same_chunk & (s >= r), range(nc - 1, -1, -1), 0))
    heads = range(GLA_HEADS)
    slot = lambda h: slice(h * HEAD_SLOT, (h + 1) * HEAD_SLOT)
    chunk = lambda ch: slice(ch * c, (ch + 1) * c)
    pair = lambda h: slice((h // 2) * HEAD_SLOT, (h // 2 + 1) * HEAD_SLOT)
    lane = lax.broadcasted_iota(jnp.int32, (1, HEAD_SLOT), 1)
    own = [(lane >= (h % 2) * GLA_QK_SLOT) & (lane < (h % 2 + 1) * GLA_QK_SLOT) for h in heads]

    b = []
    for q_ref, k_ref, v_ref, g_ref, o_ref, mask, order, last in dirs:
        tri = mask.astype(BF)
        g = g_ref[...]
        g_hi = g.astype(BF)
        g_r = g - g_hi.astype(F32)
        g_mid = g_r.astype(BF)
        g_lo = (g_r - g_mid.astype(F32)).astype(BF)
        b.append(_dot(tri, g_hi) + _dot(tri, g_mid) + _dot(tri, g_lo))

    work = []
    for d, (q_ref, k_ref, v_ref, g_ref, o_ref, mask, order, last) in enumerate(dirs):
        tot = [b[d][ch * c + last:ch * c + last + 1, :] for ch in range(nc)]
        b_tot = jnp.concatenate([jnp.broadcast_to(tot[ch], (c, GLA_QK_SLOTS)) for ch in range(nc)], axis=0)
        q_in = (q_ref[...].astype(F32) * jnp.exp(b[d])).astype(BF)
        kf = k_ref[...].astype(F32)
        k_in = (kf * jnp.exp(-b[d])).astype(BF)
        k_out = (kf * jnp.exp(b_tot - b[d])).astype(BF)
        v = v_ref[...]
        q_h = [jnp.where(own[h], q_in[:, pair(h)], jnp.zeros_like(q_in[:, pair(h)])) for h in heads]
        ko_h = [jnp.where(own[h], k_out[:, pair(h)], jnp.zeros_like(k_out[:, pair(h)])) for h in heads]
        a = [_dot_nt(q_h[h], k_in[:, pair(h)]) for h in heads]
        inc = [{ch: _dot_tn(v[chunk(ch), slot(h)], ko_h[h][chunk(ch), :]) for ch in order} for h in heads]
        work.append((tot, q_h, v, a, inc))

    o_intra = []
    for d, (q_ref, k_ref, v_ref, g_ref, o_ref, mask, order, last) in enumerate(dirs):
        tot, q_h, v, a, inc = work[d]
        o_intra.append([_dot(jnp.where(mask, a[h], 0.0).astype(BF), v[:, slot(h)]) for h in heads])

    for d, (q_ref, k_ref, v_ref, g_ref, o_ref, mask, order, last) in enumerate(dirs):
        tot, q_h, v, a, inc = work[d]
        entering = []
        for h in heads:
            st = st_ref[d, h]
            ent = {}
            for ch in order:
                ent[ch] = st.astype(BF)
                st = st * jnp.exp(tot[ch][:, pair(h)]) + inc[h][ch]
            st_ref[d, h] = st
            entering.append(ent)
        for h in heads:
            for ch in order:
                o_ref[chunk(ch), slot(h)] = (o_intra[d][h][chunk(ch), :]
                                             + _dot_nt(q_h[h][chunk(ch), :], entering[h][ch])).astype(BF)

    @pl.when(i == pl.num_programs(0) - 1)
    def _():
        sfin_ref[...] = st_ref[...]


def _gla(gq, gk, gv, df, db, s0):
    rows = gq.shape[0]
    c = min(GLA_STEP, rows)
    n = rows // c
    fwd = pl.BlockSpec((c, GLA_SLOTS), lambda i: (i, 0))
    bwd = pl.BlockSpec((c, GLA_SLOTS), lambda i: (n - 1 - i, 0))
    fwd_qk = pl.BlockSpec((c, GLA_QK_SLOTS), lambda i: (i, 0))
    bwd_qk = pl.BlockSpec((c, GLA_QK_SLOTS), lambda i: (n - 1 - i, 0))
    st_shape = (2, GLA_HEADS, HEAD_SLOT, HEAD_SLOT)
    return pl.pallas_call(
        _gla_kernel,
        out_shape=[jax.ShapeDtypeStruct((rows, GLA_SLOTS), BF), jax.ShapeDtypeStruct((rows, GLA_SLOTS), BF),
                   jax.ShapeDtypeStruct(st_shape, F32)],
        grid=(n,),
        in_specs=[fwd_qk, fwd_qk, fwd, fwd_qk, bwd_qk, bwd_qk, bwd, bwd_qk, _full(st_shape)],
        out_specs=[fwd, bwd, _full(st_shape)],
        scratch_shapes=[pltpu.VMEM(st_shape, F32)],
        compiler_params=_cparams("arbitrary"),
        name="gla_scan",
    )(gq, gk, gv, df, gq, gk, gv, db, s0)


def _out_kernel(x_ref, four_ref, att_ref, of_ref, ob_ref, gg_ref, gain_ref, wout_ref, g1_ref, n2_ref,
                sh_ref, sc_ref, xo_ref, h_ref, *, row):
    o = of_ref[...].astype(F32) + ob_ref[...].astype(F32)
    g = gg_ref[...].astype(F32)
    gate = g * (1.0 / (1.0 + jnp.exp(-g)))
    mixed = [four_ref[...]]
    for h in range(GLA_HEADS):
        sl = slice(h * HEAD_SLOT, (h + 1) * HEAD_SLOT)
        oh = o[:, sl]
        mixed.append((oh * _rms(oh, GLA_DV) * gain_ref[:, sl] * gate[:, sl]).astype(BF))
    tok = F_WIDTH + GLA_SLOTS
    y = _dot(jnp.concatenate(mixed, axis=1), wout_ref[0:tok, :])
    y += _dot_tn(att_ref[...], wout_ref[tok:tok + MLA_WIDTH, :])
    x = x_ref[...] + g1_ref[row:row + 1, :] * y
    xo_ref[...] = x
    hn = x * _rms(x, D_MODEL) * n2_ref[...]
    h_ref[...] = (hn * (1.0 + sc_ref[row:row + 1, :]) + sh_ref[row:row + 1, :]).astype(BF)


def _out_proj(x2, four, att, o_f, o_b, gg, mods, pw, l, row):
    rows = x2.shape[0]
    tm = min(2 * ROW_TILE, rows)
    rt = lambda n: pl.BlockSpec((tm, n), lambda i: (i, 0))
    modspec = lambda k: pl.BlockSpec((None, 8, D_MODEL), lambda i: (l, 0, k))
    return pl.pallas_call(
        functools.partial(_out_kernel, row=row),
        out_shape=[jax.ShapeDtypeStruct((rows, D_MODEL), F32), jax.ShapeDtypeStruct((rows, D_MODEL), BF)],
        grid=(rows // tm,),
        in_specs=[rt(D_MODEL), rt(F_WIDTH), pl.BlockSpec((MLA_WIDTH, tm), lambda i: (0, i)),
                  rt(GLA_SLOTS), rt(GLA_SLOTS), rt(GLA_SLOTS),
                  _layer((1, GLA_SLOTS), l), _layer((MIX_PAD, D_MODEL), l), modspec(2),
                  _layer((1, D_MODEL), l), modspec(3), modspec(4)],
        out_specs=[rt(D_MODEL), rt(D_MODEL)],
        compiler_params=_cparams("arbitrary"),
        name="out_proj",
    )(x2, four, att, o_f, o_b, gg, pw["gla_gain"], pw["w_out"], mods, pw["norm2_w"], mods, mods)


def _ffn_kernel(x_ref, h_ref, hp_ref, hn_ref, w12_ref, conv_ref, wd_ref, g2_ref, o_ref, acc_ref, hx_ref, u_ref,
                *, row):
    i = pl.program_id(0)
    tm = h_ref.shape[0]
    m = tm + 2 * FF_HALO
    keep_prev = jnp.where(i > 0, 1.0, 0.0)
    keep_next = jnp.where(i < pl.num_programs(0) - 1, 1.0, 0.0)
    hx_ref[0:FF_HALO, :] = (hp_ref[...].astype(F32) * keep_prev).astype(BF)
    hx_ref[FF_HALO:FF_HALO + tm, :] = h_ref[...]
    hx_ref[FF_HALO + tm:m, :] = (hn_ref[...].astype(F32) * keep_next).astype(BF)
    acc_ref[...] = jnp.zeros_like(acc_ref)

    starts = [sum(FF_CHUNKS[:c]) for c in range(len(FF_CHUNKS))]

    def cols(c, half):
        lo = half * D_FF + starts[c]
        return slice(lo, lo + FF_CHUNKS[c])

    def up(buf, c):
        for half in range(2):
            u_ref[buf, half, :, 0:FF_CHUNKS[c]] = _dot(hx_ref[...], w12_ref[:, cols(c, half)])

    def conv(buf, c, half):
        u = u_ref[buf, half, :, 0:FF_CHUNKS[c]]
        cw = conv_ref[:, cols(c, half)]
        u_prev = pltpu.roll(u, 1, axis=0)
        u_next = pltpu.roll(u, m - 1, axis=0)
        uc = u_prev * cw[0:1, :] + u * cw[1:2, :] + u_next * cw[2:3, :] + cw[3:4, :]
        return uc[FF_HALO:FF_HALO + tm, :]

    def down(buf, c):
        a = conv(buf, c, 0)
        act = a * (1.0 / (1.0 + jnp.exp(-a))) * conv(buf, c, 1)
        acc_ref[...] += _dot(act.astype(BF), wd_ref[starts[c]:starts[c] + FF_CHUNKS[c], :])

    nbuf = u_ref.shape[0]
    nchunk = len(FF_CHUNKS)
    for c in range(min(FF_AHEAD, nchunk)):
        up(c % nbuf, c)
    for c in range(nchunk):
        if c + FF_AHEAD < nchunk:
            up((c + FF_AHEAD) % nbuf, c + FF_AHEAD)
        down(c % nbuf, c)
    o_ref[...] = x_ref[...] + g2_ref[row:row + 1, :] * acc_ref[...]


def _ffn(x2, h2, mods, pw, l, row):
    rows = x2.shape[0]
    tm = min(512, rows)
    nt = rows // tm
    per = tm // FF_HALO
    last_blk = rows // FF_HALO - 1
    m = tm + 2 * FF_HALO
    rt = lambda n: pl.BlockSpec((tm, n), lambda i: (i, 0))
    once = lambda shape: pl.BlockSpec((None,) + shape, lambda i: (l,) + (0,) * len(shape),
                                      pipeline_mode=pl.Buffered(1))
    return pl.pallas_call(
        functools.partial(_ffn_kernel, row=row),
        out_shape=jax.ShapeDtypeStruct((rows, D_MODEL), F32),
        grid=(nt,),
        in_specs=[rt(D_MODEL), rt(D_MODEL),
                  pl.BlockSpec((FF_HALO, D_MODEL), lambda i: (jnp.maximum(i * per - 1, 0), 0)),
                  pl.BlockSpec((FF_HALO, D_MODEL), lambda i: (jnp.minimum((i + 1) * per, last_blk), 0)),
                  once((D_MODEL, 2 * D_FF)), once((4, 2 * D_FF)), once((D_FF, D_MODEL)),
                  pl.BlockSpec((None, 8, D_MODEL), lambda i: (l, 0, 5))],
        out_specs=rt(D_MODEL),
        scratch_shapes=[pltpu.VMEM((tm, D_MODEL), F32), pltpu.VMEM((m, D_MODEL), BF),
                        pltpu.VMEM((FF_AHEAD + 1, 2, m, max(FF_CHUNKS)), F32)],
        compiler_params=_cparams("arbitrary"),
        name="conv_ffn",
    )(x2, h2, h2, h2, pw["w12"], pw["conv"], pw["w_down"], mods)


def kernel(x, c, ctx, c_ctx, mod_w, mod_b, norm1_w, norm2_w, w_in, mla_q_lora_norm_w, mla_w_uq, mla_kv_lora_norm_w, mla_w_ukv, mla_q_norm_w, mla_k_norm_w, gla_w_gk_fwd, gla_b_gk_fwd, gla_w_gk_bwd, gla_b_gk_bwd, gla_norm_w, w_out, ffn_w_12, ffn_conv_w, ffn_conv_b, ffn_w_down):
    batch, seq, d = x.shape
    assert batch == 1 and d == D_MODEL
    lc = ctx.shape[1]
    depth = mod_w.shape[0]
    n1 = int(round(math.sqrt(seq)))
    assert n1 * n1 == seq and seq % ROW_TILE == 0 and lc % GLA_CHUNK == 0

    pw = _prep_weights(dict(
        w_in=w_in, mla_w_uq=mla_w_uq, mla_q_lora_norm_w=mla_q_lora_norm_w, mla_kv_lora_norm_w=mla_kv_lora_norm_w,
        mla_w_ukv=mla_w_ukv, mla_q_norm_w=mla_q_norm_w, mla_k_norm_w=mla_k_norm_w, gla_w_gk_fwd=gla_w_gk_fwd,
        gla_b_gk_fwd=gla_b_gk_fwd, gla_w_gk_bwd=gla_w_gk_bwd, gla_b_gk_bwd=gla_b_gk_bwd, gla_norm_w=gla_norm_w,
        w_out=w_out, ffn_w_12=ffn_w_12, ffn_conv_w=ffn_conv_w, ffn_conv_b=ffn_conv_b, ffn_w_down=ffn_w_down,
        norm1_w=norm1_w, norm2_w=norm2_w))
    cos_t, sin_t = _rope_tables(seq)
    ones_c = jnp.ones((lc, HEAD_SLOT), F32)
    w1, tab = _fft_tables(n1, n1)
    tab_c = _ctx_dft_table(lc)

    cond8 = jnp.zeros((8, d), F32).at[0].set(c[0].astype(F32)).at[1].set(c_ctx.astype(F32))
    mods = _adaln(cond8, mod_w, mod_b)

    xs = x[0].astype(F32)
    xc = ctx[0].astype(F32)
    s_zero = jnp.zeros((2, GLA_HEADS, HEAD_SLOT, HEAD_SLOT), F32)
    for l in range(depth):
        last = l == depth - 1
        fa_c, fb_c, q_c, k_c, v_c, gq_c, gk_c, gv_c, gg_c, df_c, db_c = _in_proj(xc, mods, pw, ones_c, ones_c, l, 1, False)
        fa_x, fb_x, q_x, k_x, v_x, gq_x, gk_x, gv_x, gg_x, df_x, db_x = _in_proj(xs, mods, pw, cos_t, sin_t, l, 0, True)

        of_c, ob_c, s_c = _gla(gq_c, gk_c, gv_c, df_c, db_c, s_zero)
        of_x, ob_x, _ = _gla(gq_x, gk_x, gv_x, df_x, db_x, s_c)
        four_x = _fourier_latent(fa_x, fb_x, w1, tab)
        att_x = _attention(q_x, k_c, v_c, k_x, v_x)
        x_mid, h2 = _out_proj(xs, four_x, att_x, of_x, ob_x, gg_x, mods, pw, l, 0)
        xs = _ffn(x_mid, h2, mods, pw, l, 0)
        if not last:
            four_c = _fourier_ctx(fa_c, fb_c, tab_c)
            att_c = _attention(q_c, k_c, v_c)
            c_mid, hc2 = _out_proj(xc, four_c, att_c, of_c, ob_c, gg_c, mods, pw, l, 1)
            xc = _ffn(c_mid, hc2, mods, pw, l, 1)
    return xs[None].astype(x.dtype)
```
